```python
import math
import jax, jax.numpy as jnp
from jax import lax
import numpy as np

D_MODEL = 1024
BATCH = 2
SEQ = 8192
DEPTH = 2

CTX_LEN = 256
GRID_W = 64
N_MOD = 9
D_FF = 2816
EPS = 1e-6
NEG_INF = -1e30
Q_BLOCK = 128
ROPE_THETA = 10000.0

NA_HEADS = 4
NA_HEAD_DIM = 64
NA_WIN_ROWS = 8
NA_WIN_COLS = 16
NA_QCOLS = NA_WIN_COLS
NA_BAND_COLS = 2 * NA_WIN_COLS
NA_WIDTH = NA_HEADS * NA_HEAD_DIM

POOL_WINDOWS = (2, 4, 8, 16)
POOL_GROUPS = len(POOL_WINDOWS)
POOL_GROUP_DIM = 64
POOL_WIDTH = POOL_GROUPS * POOL_GROUP_DIM

DIFF_HEADS = 4
DIFF_QK_DIM = 64
DIFF_V_DIM = 2 * DIFF_QK_DIM
DIFF_QK_WIDTH = DIFF_HEADS * 2 * DIFF_QK_DIM
DIFF_WIDTH = DIFF_HEADS * DIFF_V_DIM

MIX_WIDTH = NA_WIDTH + POOL_WIDTH + DIFF_WIDTH
IN_WIDTH = 3 * NA_WIDTH + POOL_WIDTH + 2 * DIFF_QK_WIDTH + DIFF_WIDTH
IN_SPLITS = (NA_WIDTH, 2 * NA_WIDTH, 3 * NA_WIDTH, 3 * NA_WIDTH + POOL_WIDTH,
             3 * NA_WIDTH + POOL_WIDTH + DIFF_QK_WIDTH, 3 * NA_WIDTH + POOL_WIDTH + 2 * DIFF_QK_WIDTH)

kernel_name = 'hybrid_natten_pool_diffattn_dit'


def rms_norm(x, g):
    xf = x.astype(jnp.float32)
    y = xf * lax.rsqrt(jnp.mean(xf * xf, axis=-1, keepdims=True) + EPS)
    return (y * g.astype(jnp.float32)).astype(x.dtype)


def sandwich_in(x, mods, idx, g_pre):
    shift, scale = mods[3 * idx], mods[3 * idx + 1]
    return rms_norm(x, g_pre) * (1 + scale) + shift


def sandwich_out(x, y, mods, idx, g_post, res_w):
    gate = mods[3 * idx + 2]
    return x + res_w * gate * rms_norm(y, g_post)


def swiglu(h, w1, w2):
    a, b = jnp.split(h @ w1, 2, axis=-1)
    return (jax.nn.silu(a) * b) @ w2


def axial_rope_tables(n_tokens, dim):
    t = jnp.arange(n_tokens, dtype=jnp.int32)
    row = (t // GRID_W).astype(jnp.float32)
    col = (t % GRID_W).astype(jnp.float32)
    n_freq = dim // 4
    inv_freq = jnp.power(ROPE_THETA, -jnp.arange(n_freq, dtype=jnp.float32) / n_freq)
    ang = jnp.concatenate([row[:, None] * inv_freq, col[:, None] * inv_freq], axis=-1)
    return jnp.cos(ang), jnp.sin(ang)


def apply_rope(x, cos, sin):
    x1, x2 = jnp.split(x, 2, axis=-1)
    c = cos[None, :, None, :].astype(x.dtype)
    s = sin[None, :, None, :].astype(x.dtype)
    return jnp.concatenate([x1 * c - x2 * s, x1 * s + x2 * c], axis=-1)


def dense_attention(q, k, v):
    s = jnp.einsum('bqhd,bkhd->bhqk', q, k, preferred_element_type=jnp.float32)
    p = jax.nn.softmax(s, axis=-1).astype(v.dtype)
    return jnp.einsum('bhqk,bkhd->bqhd', p, v)


def neighbourhood_attention(q, k, v, k_ctx, v_ctx, rpb):
    B, S, H, dh = q.shape
    rows = S // GRID_W
    win_r = min(NA_WIN_ROWS, rows)
    n_cb = GRID_W // NA_QCOLS
    r = jnp.arange(rows)
    row_idx = jnp.clip(r - win_r // 2, 0, rows - win_r)[:, None] + jnp.arange(win_r)[None, :]
    cb = jnp.arange(n_cb)
    band_idx = (jnp.clip(cb * NA_QCOLS - NA_WIN_COLS // 2, 0, GRID_W - NA_BAND_COLS)[:, None]
                + jnp.arange(NA_BAND_COLS)[None, :])
    q_col = cb[:, None] * NA_QCOLS + jnp.arange(NA_QCOLS)[None, :]
    win_c0 = jnp.clip(q_col - NA_WIN_COLS // 2, 0, GRID_W - NA_WIN_COLS)[:, :, None]
    key_col = band_idx[:, None, :]
    in_win = (key_col >= win_c0) & (key_col < win_c0 + NA_WIN_COLS)
    d_row = (row_idx - r[:, None]) + NA_WIN_ROWS - 1
    d_col = jnp.clip(key_col - q_col[:, :, None], 1 - NA_WIN_COLS, NA_WIN_COLS - 1) + NA_WIN_COLS - 1
    bias = rpb.astype(jnp.float32)[:, d_row[:, None, None, :, None], d_col[None, :, :, None, :]]
    bias = jnp.where(in_win[None, None, :, :, None, :], bias, NEG_INF)

    qg = q.reshape(B, rows, n_cb, NA_QCOLS, H, dh)
    kg = k.reshape(B, rows, GRID_W, H, dh)
    vg = v.reshape(B, rows, GRID_W, H, dh)
    gi_r = row_idx[:, None, :, None]
    gi_c = band_idx[None, :, None, :]
    kb = kg[:, gi_r, gi_c]
    vb = vg[:, gi_r, gi_c]
    n_loc = win_r * NA_BAND_COLS
    s_loc = jnp.einsum('brnqhd,brnkwhd->bhrnqkw', qg, kb, preferred_element_type=jnp.float32) + bias
    s_loc = s_loc.reshape(B, H, rows, n_cb, NA_QCOLS, n_loc)
    s_ctx = jnp.einsum('brnqhd,bchd->bhrnqc', qg, k_ctx, preferred_element_type=jnp.float32)
    p = jax.nn.softmax(jnp.concatenate([s_loc, s_ctx], axis=-1), axis=-1).astype(v.dtype)
    p_loc = p[..., :n_loc].reshape(B, H, rows, n_cb, NA_QCOLS, win_r, NA_BAND_COLS)
    p_ctx = p[..., n_loc:]
    out = (jnp.einsum('bhrnqkw,brnkwhd->brnqhd', p_loc, vb)
           + jnp.einsum('bhrnqc,bchd->brnqhd', p_ctx, v_ctx))
    return out.reshape(B, S, H * dh)


def pool_mix(u, w, scale):
    B, L, _ = u.shape
    ug = u.reshape(B, L, POOL_GROUPS, POOL_GROUP_DIM).astype(jnp.float32)
    cs = jnp.concatenate([jnp.zeros_like(ug[:, :1]), jnp.cumsum(ug, axis=1)], axis=1)
    half = jnp.array(POOL_WINDOWS, dtype=jnp.int32) // 2
    t = jnp.arange(L, dtype=jnp.int32)[:, None]
    lo = jnp.clip(t - half[None, :], 0, L)
    hi = jnp.clip(t + half[None, :], 0, L)
    gi = jnp.arange(POOL_GROUPS)[None, :]
    win_mean = (cs[:, hi, gi] - cs[:, lo, gi]) / (hi - lo).astype(jnp.float32)[None, :, :, None]
    pooled = (win_mean - ug).astype(u.dtype)
    y = jnp.einsum('blgc,gcd->blgd', pooled, w)
    return y.reshape(B, L, POOL_WIDTH) * scale


def diff_attention(q1, q2, k1, k2, v, lam):
    B, L, H, d = q1.shape
    nb = L // Q_BLOCK

    def to_blocks(t):
        return t.reshape(B, nb, Q_BLOCK, H, d).transpose(1, 0, 2, 3, 4)

    def one_block(qb):
        qb1, qb2 = qb
        a1 = jax.nn.softmax(jnp.einsum('bqhd,bkhd->bhqk', qb1, k1, preferred_element_type=jnp.float32), axis=-1)
        a2 = jax.nn.softmax(jnp.einsum('bqhd,bkhd->bhqk', qb2, k2, preferred_element_type=jnp.float32), axis=-1)
        p = (a1 - lam * a2).astype(v.dtype)
        return jnp.einsum('bhqk,bkhe->bqhe', p, v)

    out = lax.map(one_block, (to_blocks(q1), to_blocks(q2)))
    return out.transpose(1, 0, 2, 3, 4).reshape(B, L, H, v.shape[-1])


def project_heads(h, w_in):
    naq, nak, nav, pin, dq, dk, dv = jnp.split(h @ w_in, IN_SPLITS, axis=-1)
    B, L = h.shape[:2]
    dq = dq.reshape(B, L, DIFF_HEADS, 2, DIFF_QK_DIM)
    dk = dk.reshape(B, L, DIFF_HEADS, 2, DIFF_QK_DIM)
    return (naq.reshape(B, L, NA_HEADS, NA_HEAD_DIM) * (NA_HEAD_DIM ** -0.5),
            nak.reshape(B, L, NA_HEADS, NA_HEAD_DIM),
            nav.reshape(B, L, NA_HEADS, NA_HEAD_DIM),
            pin,
            dq[..., 0, :] * (DIFF_QK_DIM ** -0.5), dq[..., 1, :] * (DIFF_QK_DIM ** -0.5),
            dk[..., 0, :], dk[..., 1, :],
            dv.reshape(B, L, DIFF_HEADS, DIFF_V_DIM))


def token_mixing(h_lat, h_ctx, w_in, w_out, rpb, pool_w, pool_scale, lam_vec, subln_g, lam_init, cos, sin, with_ctx_out):
    B, S, _ = h_lat.shape
    C = h_ctx.shape[1]
    lf = lam_vec.astype(jnp.float32)
    lam = jnp.exp(jnp.sum(lf[0] * lf[1])) - jnp.exp(jnp.sum(lf[2] * lf[3])) + lam_init
    aq, ak, av, pin, q1, q2, k1, k2, v = project_heads(h_lat, w_in)
    aqc, akc, avc, pinc, q1c, q2c, k1c, k2c, vc = project_heads(h_ctx, w_in)

    a_lat = neighbourhood_attention(aq, ak, av, akc, avc, rpb)
    b_lat = pool_mix(pin, pool_w, pool_scale)
    q1, q2, k1, k2 = [apply_rope(t, cos, sin) for t in (q1, q2, k1, k2)]
    c_lat = diff_attention(q1, q2,
                           jnp.concatenate([k1, k1c], axis=1),
                           jnp.concatenate([k2, k2c], axis=1),
                           jnp.concatenate([v, vc], axis=1), lam)
    c_lat = (rms_norm(c_lat, subln_g) * (1 - lam_init)).reshape(B, S, DIFF_WIDTH)
    y_lat = jnp.concatenate([a_lat, b_lat, c_lat], axis=-1) @ w_out
    if not with_ctx_out:
        return y_lat, None

    a_ctx = dense_attention(aqc, akc, avc).reshape(B, C, NA_WIDTH)
    b_ctx = pool_mix(pinc, pool_w, pool_scale)
    c_ctx = diff_attention(q1c, q2c, k1c, k2c, vc, lam)
    c_ctx = (rms_norm(c_ctx, subln_g) * (1 - lam_init)).reshape(B, C, DIFF_WIDTH)
    y_ctx = jnp.concatenate([a_ctx, b_ctx, c_ctx], axis=-1) @ w_out
    return y_lat, y_ctx


def setup_inputs(seed: int = 0) -> dict:
    key = jax.random.key(seed)
    ks = jax.random.split(key, 16)
    f32 = jnp.float32
    nrm = lambda k, shape: jax.random.normal(k, shape, dtype=f32)
    n_rpb_r, n_rpb_c = 2 * NA_WIN_ROWS - 1, 2 * NA_WIN_COLS - 1
    return {
        'x': nrm(ks[0], (BATCH, SEQ, D_MODEL)),
        'c': nrm(ks[1], (BATCH, D_MODEL)),
        'ctx': nrm(ks[2], (BATCH, CTX_LEN, D_MODEL)),
        'c_ctx': nrm(ks[3], (D_MODEL,)),
        'w_ada': nrm(ks[4], (DEPTH, D_MODEL, N_MOD * D_MODEL)) * D_MODEL ** -0.5,
        'b_ada': nrm(ks[5], (DEPTH, N_MOD * D_MODEL)) * 0.02,
        'norm_g': 1.0 + 0.05 * nrm(ks[6], (DEPTH, 6, D_MODEL)),
        'ffn_w1': nrm(ks[7], (DEPTH, 2, D_MODEL, 2 * D_FF)) * D_MODEL ** -0.5,
        'ffn_w2': nrm(ks[8], (DEPTH, 2, D_FF, D_MODEL)) * D_FF ** -0.5,
        'w_in': nrm(ks[9], (DEPTH, D_MODEL, IN_WIDTH)) * D_MODEL ** -0.5,
        'w_out': nrm(ks[10], (DEPTH, MIX_WIDTH, D_MODEL)) * MIX_WIDTH ** -0.5,
        'na_rpb': nrm(ks[11], (DEPTH, NA_HEADS, n_rpb_r, n_rpb_c)) * 0.2,
        'pool_w': nrm(ks[12], (DEPTH, POOL_GROUPS, POOL_GROUP_DIM, POOL_GROUP_DIM)) * POOL_GROUP_DIM ** -0.5,
        'pool_scale': 1.0 + 0.1 * nrm(ks[13], (DEPTH, POOL_WIDTH)),
        'diff_lambda': nrm(ks[14], (DEPTH, 4, DIFF_QK_DIM)) * 0.1,
        'diff_subln_g': 1.0 + 0.05 * nrm(ks[15], (DEPTH, DIFF_V_DIM)),
    }


def reference(x, c, ctx, c_ctx, w_ada, b_ada, norm_g, ffn_w1, ffn_w2, w_in, w_out, na_rpb, pool_w, pool_scale, diff_lambda, diff_subln_g):
    S = x.shape[1]
    cos, sin = axial_rope_tables(S, DIFF_QK_DIM)
    silu_c = jax.nn.silu(c)
    silu_cc = jax.nn.silu(c_ctx)
    x_lat, x_ctx = x, ctx
    for layer in range(DEPTH):
        last = layer == DEPTH - 1
        m_lat = jnp.split((silu_c @ w_ada[layer] + b_ada[layer])[:, None, :], N_MOD, axis=-1)
        m_ctx = jnp.split(silu_cc @ w_ada[layer] + b_ada[layer], N_MOD, axis=-1)
        g = norm_g[layer]
        lam_init = 0.8 - 0.6 * math.exp(-0.3 * layer)

        x_lat = sandwich_out(x_lat, swiglu(sandwich_in(x_lat, m_lat, 0, g[0]), ffn_w1[layer, 0], ffn_w2[layer, 0]), m_lat, 0, g[1], 0.5)
        x_ctx = sandwich_out(x_ctx, swiglu(sandwich_in(x_ctx, m_ctx, 0, g[0]), ffn_w1[layer, 0], ffn_w2[layer, 0]), m_ctx, 0, g[1], 0.5)

        h_lat = sandwich_in(x_lat, m_lat, 1, g[2])
        h_ctx = sandwich_in(x_ctx, m_ctx, 1, g[2])
        y_lat, y_ctx = token_mixing(h_lat, h_ctx, w_in[layer], w_out[layer], na_rpb[layer], pool_w[layer], pool_scale[layer],
                                    diff_lambda[layer], diff_subln_g[layer], lam_init, cos, sin, not last)
        x_lat = sandwich_out(x_lat, y_lat, m_lat, 1, g[3], 1.0)

        x_lat = sandwich_out(x_lat, swiglu(sandwich_in(x_lat, m_lat, 2, g[4]), ffn_w1[layer, 1], ffn_w2[layer, 1]), m_lat, 2, g[5], 0.5)
        if not last:
            x_ctx = sandwich_out(x_ctx, y_ctx, m_ctx, 1, g[3], 1.0)
            x_ctx = sandwich_out(x_ctx, swiglu(sandwich_in(x_ctx, m_ctx, 2, g[4]), ffn_w1[layer, 1], ffn_w2[layer, 1]), m_ctx, 2, g[5], 0.5)
    return x_lat
```

```python
import functools
import math

import numpy as np
import jax
import jax.numpy as jnp
from jax import lax
from jax.experimental import pallas as pl
from jax.experimental.pallas import tpu as pltpu

D_MODEL = 1024
DEPTH = 2
GRID_W = 64
N_MOD = 9
D_FF = 2816
EPS = 1e-6
NEG_INF = -1e30
ROPE_THETA = 10000.0

NA_HEADS = 4
NA_HEAD_DIM = 64
NA_WIN_ROWS = 8
NA_WIN_COLS = 16
NA_WIDTH = NA_HEADS * NA_HEAD_DIM

POOL_WINDOWS = (2, 4, 8, 16)
POOL_GROUP_DIM = 64
POOL_WIDTH = len(POOL_WINDOWS) * POOL_GROUP_DIM
POOL_HALO = max(POOL_WINDOWS) // 2

DIFF_HEADS = 4
DIFF_QK_DIM = 64
DIFF_V_DIM = 2 * DIFF_QK_DIM
DIFF_QK_WIDTH = DIFF_HEADS * 2 * DIFF_QK_DIM
DIFF_WIDTH = DIFF_HEADS * DIFF_V_DIM

MIX_WIDTH = NA_WIDTH + POOL_WIDTH + DIFF_WIDTH
IN_WIDTH = 3 * NA_WIDTH + POOL_WIDTH + 2 * DIFF_QK_WIDTH + DIFF_WIDTH

BF16 = jnp.bfloat16
F32 = jnp.float32

V7X_VMEM_BYTES = 64 * 1024 * 1024
VMEM_LIMIT = V7X_VMEM_BYTES * 3 // 4

TOKEN_TILE = 512
FF_CHUNK = 256
ADA_COLS = 1152
NA_QROWS = 4
NA_KROWS = NA_QROWS + NA_WIN_ROWS
DIFF_TQ = 512
DIFF_TK = 1024

_NT = (((1,), (1,)), ((), ()))


def _params(*sem):
    return pltpu.CompilerParams(dimension_semantics=sem, vmem_limit_bytes=VMEM_LIMIT)


def _resident(shape, index_map):
    return pl.BlockSpec(shape, index_map, pipeline_mode=pl.Buffered(1))


def _rms(x, g):
    return x * lax.rsqrt(jnp.mean(x * x, axis=-1, keepdims=True) + EPS) * g


def _modulate(x, mod_ref, g):
    shift, scale = mod_ref[0, 0:1, :], mod_ref[0, 1:2, :]
    return _rms(x, g) * (1.0 + scale) + shift


def _ada_kernel(c_ref, w_ref, b_ref, o_ref):
    cv = c_ref[...]
    o_ref[0] = jnp.dot(cv * jax.nn.sigmoid(cv), w_ref[0], preferred_element_type=F32,
                       precision=lax.Precision.HIGHEST) + b_ref[0]


def _ada(cvec, w_ada, b_ada):
    n_out = N_MOD * D_MODEL
    return pl.pallas_call(
        _ada_kernel,
        out_shape=jax.ShapeDtypeStruct((DEPTH, 8, n_out), F32),
        grid=(DEPTH, n_out // ADA_COLS),
        in_specs=[pl.BlockSpec((8, D_MODEL), lambda l, j: (0, 0)),
                  pl.BlockSpec((1, D_MODEL, ADA_COLS), lambda l, j: (l, 0, j)),
                  pl.BlockSpec((1, 1, ADA_COLS), lambda l, j: (l, 0, j))],
        out_specs=pl.BlockSpec((1, 8, ADA_COLS), lambda l, j: (l, 0, j)),
        compiler_params=_params("parallel", "parallel"),
        name="ada_mod",
    )(cvec, w_ada, b_ada.reshape(DEPTH, 1, n_out))


def _ffn_kernel(x_ref, mod_ref, g_ref, w1_ref, w2_ref, o_ref, act_ref, *, res_w):
    x = x_ref[0]
    h = _modulate(x, mod_ref, g_ref[0:1, :]).astype(BF16)
    for c in range(D_FF // FF_CHUNK):
        lo = c * FF_CHUNK
        a = jnp.dot(h, w1_ref[:, lo:lo + FF_CHUNK], preferred_element_type=F32)
        b = jnp.dot(h, w1_ref[:, D_FF + lo:D_FF + lo + FF_CHUNK], preferred_element_type=F32)
        act_ref[:, lo:lo + FF_CHUNK] = (a * jax.nn.sigmoid(a) * b).astype(BF16)
    y = jnp.dot(act_ref[...], w2_ref[...], preferred_element_type=F32)
    o_ref[0] = x + res_w * mod_ref[0, 2:3, :] * _rms(y, g_ref[1:2, :])


def _ffn(x, mod, mod_row, g, w1, w2, res_w):
    nb, n, _ = x.shape
    tm = min(TOKEN_TILE, n)
    return pl.pallas_call(
        functools.partial(_ffn_kernel, res_w=res_w),
        out_shape=jax.ShapeDtypeStruct(x.shape, F32),
        grid=(nb, n // tm),
        in_specs=[pl.BlockSpec((1, tm, D_MODEL), lambda b, i: (b, i, 0)),
                  pl.BlockSpec((1, 3, D_MODEL), lambda b, i: (mod_row(b), 0, 0)),
                  pl.BlockSpec((2, D_MODEL), lambda b, i: (0, 0)),
                  _resident((D_MODEL, 2 * D_FF), lambda b, i: (0, 0)),
                  _resident((D_FF, D_MODEL), lambda b, i: (0, 0))],
        out_specs=pl.BlockSpec((1, tm, D_MODEL), lambda b, i: (b, i, 0)),
        scratch_shapes=[pltpu.VMEM((tm, D_FF), BF16)],
        compiler_params=_params("parallel", "parallel"),
        name="ffn",
    )(x, mod, g, w1, w2)


def _inproj_kernel(x_ref, mod_ref, g_ref, w_ref, cos_ref, sin_ref,
                   naq_ref, nak_ref, nav_ref, pin_ref, dq_ref, dk_ref, dv_ref):
    x = x_ref[0]
    h = _modulate(x, mod_ref, g_ref[0:1, :]).astype(BF16)

    def proj(lo, width):
        return jnp.dot(h, w_ref[:, lo:lo + width], preferred_element_type=F32)

    naq_ref[0] = (proj(0, NA_WIDTH) * (NA_HEAD_DIM ** -0.5)).astype(BF16)
    nak_ref[0] = proj(NA_WIDTH, NA_WIDTH).astype(BF16)
    nav_ref[0] = proj(2 * NA_WIDTH, NA_WIDTH).astype(BF16)
    pin_ref[0] = proj(3 * NA_WIDTH, POOL_WIDTH)

    cos, sin = cos_ref[...], sin_ref[...]
    lane = lax.broadcasted_iota(jnp.int32, cos.shape, 1)
    first_half = (lane % DIFF_QK_DIM) < DIFF_QK_DIM // 2
    head_w = 2 * DIFF_QK_DIM

    def rope(t):
        partner = jnp.where(first_half, pltpu.roll(t, head_w - DIFF_QK_DIM // 2, 1),
                            pltpu.roll(t, DIFF_QK_DIM // 2, 1))
        return t * cos + partner * sin

    off = 3 * NA_WIDTH + POOL_WIDTH
    q = proj(off, DIFF_QK_WIDTH) * (DIFF_QK_DIM ** -0.5)
    k = proj(off + DIFF_QK_WIDTH, DIFF_QK_WIDTH)
    for hd in range(DIFF_HEADS):
        sl = slice(hd * head_w, (hd + 1) * head_w)
        dq_ref[0, :, sl] = rope(q[:, sl]).astype(BF16)
        dk_ref[0, :, sl] = rope(k[:, sl]).astype(BF16)
    dv_ref[0] = proj(off + 2 * DIFF_QK_WIDTH, DIFF_WIDTH).astype(BF16)


def _inproj(x, mod, mod_row, g, w, cos, sin):
    nb, n, _ = x.shape
    tm = min(TOKEN_TILE, n)
    tok = lambda width: pl.BlockSpec((1, tm, width), lambda b, i: (b, i, 0))
    out = lambda width, dt: jax.ShapeDtypeStruct((nb, n, width), dt)
    return pl.pallas_call(
        _inproj_kernel,
        out_shape=(out(NA_WIDTH, BF16), out(NA_WIDTH, BF16), out(NA_WIDTH, BF16), out(POOL_WIDTH, F32),
                   out(DIFF_QK_WIDTH, BF16), out(DIFF_QK_WIDTH, BF16), out(DIFF_WIDTH, BF16)),
        grid=(nb, n // tm),
        in_specs=[tok(D_MODEL),
                  pl.BlockSpec((1, 3, D_MODEL), lambda b, i: (mod_row(b), 0, 0)),
                  pl.BlockSpec((1, D_MODEL), lambda b, i: (0, 0)),
                  _resident((D_MODEL, IN_WIDTH), lambda b, i: (0, 0)),
                  pl.BlockSpec((tm, 2 * DIFF_QK_DIM), lambda b, i: (i, 0)),
                  pl.BlockSpec((tm, 2 * DIFF_QK_DIM), lambda b, i: (i, 0))],
        out_specs=(tok(NA_WIDTH), tok(NA_WIDTH), tok(NA_WIDTH), tok(POOL_WIDTH),
                   tok(DIFF_QK_WIDTH), tok(DIFF_QK_WIDTH), tok(DIFF_WIDTH)),
        compiler_params=_params("parallel", "parallel"),
        name="mixer_in_proj",
    )(x, mod, g, w, cos, sin)


def _na_kernel(q_ref, k_ref, v_ref, kc_ref, vc_ref, bias_ref, o_ref, *, rows):
    i = pl.program_id(1)
    start = jnp.clip(i * NA_QROWS - NA_WIN_ROWS // 2, 0, rows - NA_KROWS) * GRID_W
    start = pl.multiple_of(start, GRID_W)
    n_keys = NA_KROWS * GRID_W
    q = q_ref[0]
    kb = k_ref[0, pl.ds(start, n_keys), :]
    vb = v_ref[0, pl.ds(start, n_keys), :]
    kc, vc = kc_ref[0], vc_ref[0]
    for hd in range(NA_HEADS):
        sl = slice(hd * NA_HEAD_DIM, (hd + 1) * NA_HEAD_DIM)
        qh = q[:, sl]
        s_loc = lax.dot_general(qh, kb[:, sl], _NT, preferred_element_type=F32) + bias_ref[0, hd]
        s_ctx = lax.dot_general(qh, kc[:, sl], _NT, preferred_element_type=F32)
        m = jnp.maximum(jnp.max(s_loc, axis=1, keepdims=True), jnp.max(s_ctx, axis=1, keepdims=True))
        p_loc = jnp.exp(s_loc - m)
        p_ctx = jnp.exp(s_ctx - m)
        denom = jnp.sum(p_loc, axis=1, keepdims=True) + jnp.sum(p_ctx, axis=1, keepdims=True)
        o = (jnp.dot(p_loc.astype(BF16), vb[:, sl], preferred_element_type=F32)
             + jnp.dot(p_ctx.astype(BF16), vc[:, sl], preferred_element_type=F32))
        o_ref[0, :, sl] = (o / denom).astype(BF16)


def _na(q, k, v, kc, vc, bias):
    nb, s, _ = q.shape
    rows = s // GRID_W
    n_steps = rows // NA_QROWS
    tq = NA_QROWS * GRID_W
    n_ctx = kc.shape[1]

    def pattern(b, i):
        return (jnp.where(i == 0, 0, jnp.where(i == n_steps - 1, 2, 1)), 0, 0, 0)

    return pl.pallas_call(
        functools.partial(_na_kernel, rows=rows),
        out_shape=jax.ShapeDtypeStruct((nb, s, NA_WIDTH), BF16),
        grid=(nb, n_steps),
        in_specs=[pl.BlockSpec((1, tq, NA_WIDTH), lambda b, i: (b, i, 0)),
                  _resident((1, s, NA_WIDTH), lambda b, i: (b, 0, 0)),
                  _resident((1, s, NA_WIDTH), lambda b, i: (b, 0, 0)),
                  pl.BlockSpec((1, n_ctx, NA_WIDTH), lambda b, i: (b, 0, 0)),
                  pl.BlockSpec((1, n_ctx, NA_WIDTH), lambda b, i: (b, 0, 0)),
                  pl.BlockSpec((1, NA_HEADS, tq, NA_KROWS * GRID_W), pattern)],
        out_specs=pl.BlockSpec((1, tq, NA_WIDTH), lambda b, i: (b, i, 0)),
        compiler_params=_params("parallel", "arbitrary"),
        name="neighbourhood_attn",
    )(q, k, v, kc, vc, bias)


def _na_bias_tables(rpb, rows):
    n_h = rpb.shape[0]
    n_dr = 2 * NA_WIN_ROWS - 1
    pad = GRID_W - NA_WIN_COLS
    ext = jnp.pad(rpb.astype(F32), ((0, 0), (0, 0), (pad, pad)), mode="edge")
    toe = jnp.stack([ext[:, :, GRID_W - 1 - qc: 2 * GRID_W - 1 - qc] for qc in range(GRID_W)], axis=2)
    qc = np.arange(GRID_W)[:, None]
    kc = np.arange(GRID_W)[None, :]
    win_c0 = np.clip(qc - NA_WIN_COLS // 2, 0, GRID_W - NA_WIN_COLS)
    in_cols = (kc >= win_c0) & (kc < win_c0 + NA_WIN_COLS)
    toe = jnp.where(jnp.asarray(in_cols)[None, None], toe, NEG_INF)
    masked = jnp.full((n_h, GRID_W, GRID_W), NEG_INF, F32)
    tables = []
    for r0 in (0, NA_QROWS, rows - NA_QROWS):
        start = int(np.clip(r0 - NA_WIN_ROWS // 2, 0, rows - NA_KROWS))
        q_rows = []
        for qr in range(NA_QROWS):
            r = r0 + qr
            ws = int(np.clip(r - NA_WIN_ROWS // 2, 0, rows - NA_WIN_ROWS))
            tiles = []
            for kl in range(NA_KROWS):
                kr = start + kl
                tiles.append(toe[:, kr - r + NA_WIN_ROWS - 1] if ws <= kr < ws + NA_WIN_ROWS else masked)
            q_rows.append(jnp.concatenate(tiles, axis=2))
        tables.append(jnp.concatenate(q_rows, axis=1))
    return jnp.stack(tables, axis=0)


def _diff_kernel(*refs, has_ctx, lam_init, n_kv):
    if has_ctx:
        q_ref, k_ref, v_ref, kc_ref, vc_ref, lam_ref, g_ref, o_ref, m_ref, l_ref, acc_ref = refs
    else:
        q_ref, k_ref, v_ref, lam_ref, g_ref, o_ref, m_ref, l_ref, acc_ref = refs
    j = pl.program_id(2)

    def update(kblk, vblk):
        for hd in range(DIFF_HEADS):
            vh = vblk[:, hd * DIFF_V_DIM:(hd + 1) * DIFF_V_DIM]
            for st in range(2):
                idx = 2 * hd + st
                sl = slice(idx * DIFF_QK_DIM, (idx + 1) * DIFF_QK_DIM)
                s = lax.dot_general(q_ref[0, :, sl], kblk[:, sl], _NT, preferred_element_type=F32)
                m_old = m_ref[idx]
                m_new = jnp.maximum(m_old, jnp.max(s, axis=1, keepdims=True))
                alpha = jnp.exp(m_old - m_new)
                p = jnp.exp(s - m_new)
                l_ref[idx] = alpha * l_ref[idx] + jnp.sum(p, axis=1, keepdims=True)
                acc_ref[idx] = alpha * acc_ref[idx] + jnp.dot(p.astype(BF16), vh, preferred_element_type=F32)
                m_ref[idx] = m_new

    @pl.when(j == 0)
    def _init():
        m_ref[...] = jnp.full(m_ref.shape, -jnp.inf, F32)
        l_ref[...] = jnp.zeros(l_ref.shape, F32)
        acc_ref[...] = jnp.zeros(acc_ref.shape, F32)
        if has_ctx:
            update(kc_ref[0], vc_ref[0])

    update(k_ref[0], v_ref[0])

    @pl.when(j == n_kv - 1)
    def _finish():
        lf = lam_ref[...]
        lam = (jnp.exp(jnp.sum(lf[0:1] * lf[1:2], axis=1, keepdims=True))
               - jnp.exp(jnp.sum(lf[2:3] * lf[3:4], axis=1, keepdims=True)) + lam_init)
        for hd in range(DIFF_HEADS):
            o1 = acc_ref[2 * hd] / l_ref[2 * hd]
            o2 = acc_ref[2 * hd + 1] / l_ref[2 * hd + 1]
            o = _rms(o1 - lam * o2, g_ref[...]) * (1.0 - lam_init)
            o_ref[0, :, hd * DIFF_V_DIM:(hd + 1) * DIFF_V_DIM] = o.astype(BF16)


def _diff(q, k, v, ctx_kv, lam_vec, subln_g, lam_init):
    nb, n, _ = q.shape
    n_keys = k.shape[1]
    tq = min(DIFF_TQ, n)
    tk = min(DIFF_TK, n_keys)
    n_kv = n_keys // tk
    has_ctx = ctx_kv is not None
    in_specs = [pl.BlockSpec((1, tq, DIFF_QK_WIDTH), lambda b, i, j: (b, i, 0)),
                pl.BlockSpec((1, tk, DIFF_QK_WIDTH), lambda b, i, j: (b, j, 0)),
                pl.BlockSpec((1, tk, DIFF_WIDTH), lambda b, i, j: (b, j, 0))]
    args = [q, k, v]
    if has_ctx:
        n_ctx = ctx_kv[0].shape[1]
        in_specs += [pl.BlockSpec((1, n_ctx, DIFF_QK_WIDTH), lambda b, i, j: (b, 0, 0)),
                     pl.BlockSpec((1, n_ctx, DIFF_WIDTH), lambda b, i, j: (b, 0, 0))]
        args += list(ctx_kv)
    in_specs += [pl.BlockSpec((4, DIFF_QK_DIM), lambda b, i, j: (0, 0)),
                 pl.BlockSpec((1, DIFF_V_DIM), lambda b, i, j: (0, 0))]
    args += [lam_vec, subln_g.reshape(1, DIFF_V_DIM)]
    n_stream = 2 * DIFF_HEADS
    return pl.pallas_call(
        functools.partial(_diff_kernel, has_ctx=has_ctx, lam_init=lam_init, n_kv=n_kv),
        out_shape=jax.ShapeDtypeStruct((nb, n, DIFF_WIDTH), BF16),
        grid=(nb, n // tq, n_kv),
        in_specs=in_specs,
        out_specs=pl.BlockSpec((1, tq, DIFF_WIDTH), lambda b, i, j: (b, i, 0)),
        scratch_shapes=[pltpu.VMEM((n_stream, tq, 1), F32),
                        pltpu.VMEM((n_stream, tq, 1), F32),
                        pltpu.VMEM((n_stream, tq, DIFF_V_DIM), F32)],
        compiler_params=_params("parallel", "parallel", "arbitrary"),
        name="diff_attn",
    )(*args)


def _ctx_attn_kernel(q_ref, k_ref, v_ref, o_ref):
    q, k, v = q_ref[0], k_ref[0], v_ref[0]
    for hd in range(NA_HEADS):
        sl = slice(hd * NA_HEAD_DIM, (hd + 1) * NA_HEAD_DIM)
        s = lax.dot_general(q[:, sl], k[:, sl], _NT, preferred_element_type=F32)
        p = jnp.exp(s - jnp.max(s, axis=1, keepdims=True))
        o = jnp.dot(p.astype(BF16), v[:, sl], preferred_element_type=F32)
        o_ref[0, :, sl] = (o / jnp.sum(p, axis=1, keepdims=True)).astype(BF16)


def _ctx_attn(q, k, v):
    nb, n, w = q.shape
    spec = pl.BlockSpec((1, n, w), lambda b: (b, 0, 0))
    return pl.pallas_call(
        _ctx_attn_kernel,
        out_shape=jax.ShapeDtypeStruct(q.shape, BF16),
        grid=(nb,),
        in_specs=[spec, spec, spec],
        out_specs=spec,
        compiler_params=_params("parallel"),
        name="ctx_dense_attn",
    )(q, k, v)


def _outproj_kernel(x_ref, mod_ref, g_ref, a_ref, pprev_ref, pcur_ref, pnext_ref, c_ref,
                    pw_ref, ps_ref, w_ref, o_ref, *, tm, seq_len):
    i = pl.program_id(1)
    n = tm + 2 * POOL_HALO
    u = jnp.concatenate([pprev_ref[0], pcur_ref[0], pnext_ref[0]], axis=0)
    t_ext = i * tm - POOL_HALO + lax.broadcasted_iota(jnp.int32, (n, 1), 0)
    u = jnp.where((t_ext >= 0) & (t_ext < seq_len), u, 0.0)
    s2 = u + pltpu.roll(u, 1, 0)
    s4 = pltpu.roll(s2, 1, 0) + pltpu.roll(s2, n - 1, 0)
    s8 = pltpu.roll(s4, 2, 0) + pltpu.roll(s4, n - 2, 0)
    s16 = pltpu.roll(s8, 4, 0) + pltpu.roll(s8, n - 4, 0)
    cur = slice(POOL_HALO, POOL_HALO + tm)
    group = lax.broadcasted_iota(jnp.int32, (1, POOL_WIDTH), 1) // POOL_GROUP_DIM
    wsum = jnp.where(group == 0, s2[cur], jnp.where(group == 1, s4[cur], jnp.where(group == 2, s8[cur], s16[cur])))
    half = jnp.where(group == 0, POOL_WINDOWS[0] // 2,
                     jnp.where(group == 1, POOL_WINDOWS[1] // 2,
                               jnp.where(group == 2, POOL_WINDOWS[2] // 2, POOL_WINDOWS[3] // 2)))
    t = i * tm + lax.broadcasted_iota(jnp.int32, (tm, 1), 0)
    count = (jnp.clip(t + half, 0, seq_len) - jnp.clip(t - half, 0, seq_len)).astype(F32)
    pooled = (wsum / count - u[cur]).astype(BF16)
    b = jnp.dot(pooled, pw_ref[...], preferred_element_type=F32) * ps_ref[...]
    cat = jnp.concatenate([a_ref[0], b.astype(BF16), c_ref[0]], axis=1)
    y = jnp.dot(cat, w_ref[...], preferred_element_type=F32)
    o_ref[0] = x_ref[0] + mod_ref[0, 2:3, :] * _rms(y, g_ref[...])


def _outproj(x, mod, mod_row, g, a, pin, c, pool_bd, pool_scale, w):
    nb, n, _ = x.shape
    tm = min(TOKEN_TILE, n)
    halo_per_tile = tm // POOL_HALO
    n_halo = n // POOL_HALO
    tok = lambda width: pl.BlockSpec((1, tm, width), lambda b, i: (b, i, 0))
    return pl.pallas_call(
        functools.partial(_outproj_kernel, tm=tm, seq_len=n),
        out_shape=jax.ShapeDtypeStruct(x.shape, F32),
        grid=(nb, n // tm),
        in_specs=[tok(D_MODEL),
                  pl.BlockSpec((1, 3, D_MODEL), lambda b, i: (mod_row(b), 0, 0)),
                  pl.BlockSpec((1, D_MODEL), lambda b, i: (0, 0)),
                  tok(NA_WIDTH),
                  pl.BlockSpec((1, POOL_HALO, POOL_WIDTH),
                               lambda b, i: (b, jnp.maximum(i * halo_per_tile - 1, 0), 0)),
                  tok(POOL_WIDTH),
                  pl.BlockSpec((1, POOL_HALO, POOL_WIDTH),
                               lambda b, i: (b, jnp.minimum((i + 1) * halo_per_tile, n_halo - 1), 0)),
                  tok(DIFF_WIDTH),
                  pl.BlockSpec((POOL_WIDTH, POOL_WIDTH), lambda b, i: (0, 0)),
                  pl.BlockSpec((1, POOL_WIDTH), lambda b, i: (0, 0)),
                  _resident((MIX_WIDTH, D_MODEL), lambda b, i: (0, 0))],
        out_specs=tok(D_MODEL),
        compiler_params=_params("parallel", "parallel"),
        name="mixer_out_proj",
    )(x, mod, g, a, pin, pin, pin, c, pool_bd, pool_scale, w)


def _rope_tables(n_tokens):
    t = jnp.arange(n_tokens, dtype=jnp.int32)
    row = (t // GRID_W).astype(F32)
    col = (t % GRID_W).astype(F32)
    n_freq = DIFF_QK_DIM // 4
    inv_freq = jnp.power(ROPE_THETA, -jnp.arange(n_freq, dtype=F32) / n_freq)
    ang = jnp.concatenate([row[:, None] * inv_freq, col[:, None] * inv_freq], axis=-1)
    cos, sin = jnp.cos(ang), jnp.sin(ang)
    return jnp.tile(cos, (1, 4)), jnp.concatenate([-sin, sin, -sin, sin], axis=-1)


def _block_diag(w):
    n_g, cg, _ = w.shape
    eye = jnp.eye(n_g, dtype=w.dtype)
    return (eye[:, None, :, None] * w[:, :, None, :]).reshape(n_g * cg, n_g * cg)


def kernel(x, c, ctx, c_ctx, w_ada, b_ada, norm_g, ffn_w1, ffn_w2, w_in, w_out, na_rpb, pool_w, pool_scale,
           diff_lambda, diff_subln_g):
    nb, s, _ = x.shape
    n_ctx = ctx.shape[1]
    assert nb <= 7 and s % (GRID_W * NA_QROWS) == 0 and s % DIFF_TK == 0 and n_ctx % POOL_HALO == 0

    w1 = ffn_w1.astype(BF16)
    w2 = ffn_w2.astype(BF16)
    w_in_b = w_in.astype(BF16)
    w_out_b = w_out.astype(BF16)

    ctx_row = nb
    cvec = jnp.zeros((8, D_MODEL), F32).at[:nb].set(c).at[ctx_row].set(c_ctx)
    mods = _ada(cvec, w_ada, b_ada).reshape(DEPTH, 8, N_MOD, D_MODEL)
    lat_row = lambda b: b
    ctx_mod_row = lambda b: ctx_row

    cos, sin = _rope_tables(s)
    cos_ctx = jnp.ones((n_ctx, 2 * DIFF_QK_DIM), F32)
    sin_ctx = jnp.zeros((n_ctx, 2 * DIFF_QK_DIM), F32)

    x_lat, x_ctx = x, ctx
    for layer in range(DEPTH):
        last = layer == DEPTH - 1
        lam_init = 0.8 - 0.6 * math.exp(-0.3 * layer)
        m = mods[layer]
        g = norm_g[layer]
        bias = _na_bias_tables(na_rpb[layer], s // GRID_W)
        pool_bd = _block_diag(pool_w[layer]).astype(BF16)
        pool_sc = pool_scale[layer].reshape(1, POOL_WIDTH)

        x_lat = _ffn(x_lat, m[:, 0:3], lat_row, g[0:2], w1[layer, 0], w2[layer, 0], 0.5)
        x_ctx = _ffn(x_ctx, m[:, 0:3], ctx_mod_row, g[0:2], w1[layer, 0], w2[layer, 0], 0.5)

        aq, ak, av, pin, dq, dk, dv = _inproj(x_lat, m[:, 3:6], lat_row, g[2:3], w_in_b[layer], cos, sin)
        aqc, akc, avc, pinc, dqc, dkc, dvc = _inproj(x_ctx, m[:, 3:6], ctx_mod_row, g[2:3], w_in_b[layer],
                                                     cos_ctx, sin_ctx)
        a_lat = _na(aq, ak, av, akc, avc, bias)
        c_lat = _diff(dq, dk, dv, (dkc, dvc), diff_lambda[layer], diff_subln_g[layer], lam_init)
        x_lat_mixed = _outproj(x_lat, m[:, 3:6], lat_row, g[3:4], a_lat, pin, c_lat, pool_bd, pool_sc,
                               w_out_b[layer])
        x_lat = _ffn(x_lat_mixed, m[:, 6:9], lat_row, g[4:6], w1[layer, 1], w2[layer, 1], 0.5)

        if not last:
            a_ctx = _ctx_attn(aqc, akc, avc)
            c_ctx_out = _diff(dqc, dkc, dvc, None, diff_lambda[layer], diff_subln_g[layer], lam_init)
            x_ctx_mixed = _outproj(x_ctx, m[:, 3:6], ctx_mod_row, g[3:4], a_ctx, pinc, c_ctx_out, pool_bd, pool_sc,
                                   w_out_b[layer])
            x_ctx = _ffn(x_ctx_mixed, m[:, 6:9], ctx_mod_row, g[4:6], w1[layer, 1], w2[layer, 1], 0.5)
    return x_lat
```

```python
import functools
import math

import numpy as np
import jax
import jax.numpy as jnp
from jax import lax
from jax.experimental import pallas as pl
from jax.experimental.pallas import tpu as pltpu

D_MODEL = 1024
DEPTH = 2
GRID_W = 64
N_MOD = 9
D_FF = 2816
EPS = 1e-6
NEG_INF = -1e30
ROPE_THETA = 10000.0

NA_HEADS = 4
NA_HEAD_DIM = 64
NA_WIN_ROWS = 8
NA_WIN_COLS = 16
NA_WIDTH = NA_HEADS * NA_HEAD_DIM

POOL_WINDOWS = (2, 4, 8, 16)
POOL_GROUP_DIM = 64
POOL_WIDTH = len(POOL_WINDOWS) * POOL_GROUP_DIM
POOL_HALO = max(POOL_WINDOWS) // 2

DIFF_HEADS = 4
DIFF_QK_DIM = 64
DIFF_V_DIM = 2 * DIFF_QK_DIM
DIFF_QK_WIDTH = DIFF_HEADS * 2 * DIFF_QK_DIM
DIFF_WIDTH = DIFF_HEADS * DIFF_V_DIM
DIFF_Q_SCALE = DIFF_QK_DIM ** -0.5 * math.log2(math.e)

MIX_WIDTH = NA_WIDTH + POOL_WIDTH + DIFF_WIDTH
IN_WIDTH = 3 * NA_WIDTH + POOL_WIDTH + 2 * DIFF_QK_WIDTH + DIFF_WIDTH

BF16 = jnp.bfloat16
F32 = jnp.float32

V7X_VMEM_BYTES = 64 * 1024 * 1024
VMEM_LIMIT = V7X_VMEM_BYTES * 3 // 4

TOKEN_TILE = 512
FF_CHUNK = 256
ADA_COLS = 1152
NA_QROWS = 4
NA_KROWS = NA_QROWS + NA_WIN_ROWS
DIFF_TQ = 512
DIFF_TK = 1024

_NT = (((1,), (1,)), ((), ()))


def _params(*sem):
    return pltpu.CompilerParams(dimension_semantics=sem, vmem_limit_bytes=VMEM_LIMIT)


def _resident(shape, index_map):
    return pl.BlockSpec(shape, index_map, pipeline_mode=pl.Buffered(1))


def _rms(x, g):
    return x * lax.rsqrt(jnp.mean(x * x, axis=-1, keepdims=True) + EPS) * g


def _modulate(x, mod_ref, g):
    shift, scale = mod_ref[0, 0:1, :], mod_ref[0, 1:2, :]
    return _rms(x, g) * (1.0 + scale) + shift


def _ada_kernel(c_ref, w_ref, b_ref, o_ref):
    cv = c_ref[...]
    o_ref[0] = jnp.dot(cv * jax.nn.sigmoid(cv), w_ref[0], preferred_element_type=F32,
                       precision=lax.Precision.HIGHEST) + b_ref[0]


def _ada(cvec, w_ada, b_ada):
    n_out = N_MOD * D_MODEL
    return pl.pallas_call(
        _ada_kernel,
        out_shape=jax.ShapeDtypeStruct((DEPTH, 8, n_out), F32),
        grid=(DEPTH, n_out // ADA_COLS),
        in_specs=[pl.BlockSpec((8, D_MODEL), lambda l, j: (0, 0)),
                  pl.BlockSpec((1, D_MODEL, ADA_COLS), lambda l, j: (l, 0, j)),
                  pl.BlockSpec((1, 1, ADA_COLS), lambda l, j: (l, 0, j))],
        out_specs=pl.BlockSpec((1, 8, ADA_COLS), lambda l, j: (l, 0, j)),
        compiler_params=_params("parallel", "parallel"),
        name="ada_mod",
    )(cvec, w_ada, b_ada.reshape(DEPTH, 1, n_out))


def _ffn_kernel(x_ref, mod_ref, g_ref, w1_ref, w2_ref, o_ref, act_ref, *, res_w):
    x = x_ref[0]
    h = _modulate(x, mod_ref, g_ref[0:1, :]).astype(BF16)
    for c in range(D_FF // FF_CHUNK):
        lo = c * FF_CHUNK
        a = jnp.dot(h, w1_ref[:, lo:lo + FF_CHUNK], preferred_element_type=F32)
        b = jnp.dot(h, w1_ref[:, D_FF + lo:D_FF + lo + FF_CHUNK], preferred_element_type=F32)
        act_ref[:, lo:lo + FF_CHUNK] = (a * jax.nn.sigmoid(a) * b).astype(BF16)
    y = jnp.dot(act_ref[...], w2_ref[...], preferred_element_type=F32)
    o_ref[0] = x + res_w * mod_ref[0, 2:3, :] * _rms(y, g_ref[1:2, :])


def _ffn(x, mod, mod_row, g, w1, w2, res_w):
    nb, n, _ = x.shape
    tm = min(TOKEN_TILE, n)
    return pl.pallas_call(
        functools.partial(_ffn_kernel, res_w=res_w),
        out_shape=jax.ShapeDtypeStruct(x.shape, F32),
        grid=(nb, n // tm),
        in_specs=[pl.BlockSpec((1, tm, D_MODEL), lambda b, i: (b, i, 0)),
                  pl.BlockSpec((1, 3, D_MODEL), lambda b, i: (mod_row(b), 0, 0)),
                  pl.BlockSpec((2, D_MODEL), lambda b, i: (0, 0)),
                  _resident((D_MODEL, 2 * D_FF), lambda b, i: (0, 0)),
                  _resident((D_FF, D_MODEL), lambda b, i: (0, 0))],
        out_specs=pl.BlockSpec((1, tm, D_MODEL), lambda b, i: (b, i, 0)),
        scratch_shapes=[pltpu.VMEM((tm, D_FF), BF16)],
        compiler_params=_params("parallel", "parallel"),
        name="ffn",
    )(x, mod, g, w1, w2)


def _inproj_kernel(x_ref, mod_ref, g_ref, w_ref, wt_ref, cos_ref, sin_ref, cos_t_ref, sin_t_ref,
                   naq_ref, nak_ref, nav_ref, pin_ref, dqt_ref, dk_ref, dvt_ref):
    x = x_ref[0]
    h = _modulate(x, mod_ref, g_ref[0:1, :]).astype(BF16)

    def proj(lo, width):
        return jnp.dot(h, w_ref[:, lo:lo + width], preferred_element_type=F32)

    naq_ref[0] = (proj(0, NA_WIDTH) * (NA_HEAD_DIM ** -0.5)).astype(BF16)
    nak_ref[0] = proj(NA_WIDTH, NA_WIDTH).astype(BF16)
    nav_ref[0] = proj(2 * NA_WIDTH, NA_WIDTH).astype(BF16)
    pin_ref[0] = proj(3 * NA_WIDTH, POOL_WIDTH)

    cos, sin = cos_ref[...], sin_ref[...]
    lane = lax.broadcasted_iota(jnp.int32, cos.shape, 1)
    half = DIFF_QK_DIM // 2
    first_half = (lane % DIFF_QK_DIM) < half
    head_w = 2 * DIFF_QK_DIM
    k = proj(3 * NA_WIDTH + POOL_WIDTH, DIFF_QK_WIDTH)
    for hd in range(DIFF_HEADS):
        t = k[:, hd * head_w:(hd + 1) * head_w]
        partner = jnp.where(first_half, pltpu.roll(t, head_w - half, 1), pltpu.roll(t, half, 1))
        dk_ref[0, :, hd * head_w:(hd + 1) * head_w] = (t * cos + partner * sin).astype(BF16)

    qv = lax.dot_general(wt_ref[...], h, _NT, preferred_element_type=F32)
    cos_t, sin_t = cos_t_ref[...], sin_t_ref[...]
    for st in range(2 * DIFF_HEADS):
        base = st * DIFF_QK_DIM
        x1 = qv[base:base + half] * DIFF_Q_SCALE
        x2 = qv[base + half:base + DIFF_QK_DIM] * DIFF_Q_SCALE
        dqt_ref[0, base:base + half, :] = (x1 * cos_t - x2 * sin_t).astype(BF16)
        dqt_ref[0, base + half:base + DIFF_QK_DIM, :] = (x1 * sin_t + x2 * cos_t).astype(BF16)
    dvt_ref[0] = qv[DIFF_QK_WIDTH:].astype(BF16)


def _inproj(x, mod, mod_row, g, w, wt, rope):
    nb, n, _ = x.shape
    tm = min(TOKEN_TILE, n)
    cos, sin, cos_t, sin_t = rope
    n_tok_major = w.shape[1]
    tok = lambda width: pl.BlockSpec((1, tm, width), lambda b, i: (b, i, 0))
    feat = lambda width: pl.BlockSpec((1, width, tm), lambda b, i: (b, 0, i))
    out = lambda width, dt: jax.ShapeDtypeStruct((nb, n, width), dt)
    out_t = lambda width, dt: jax.ShapeDtypeStruct((nb, width, n), dt)
    return pl.pallas_call(
        _inproj_kernel,
        out_shape=(out(NA_WIDTH, BF16), out(NA_WIDTH, BF16), out(NA_WIDTH, BF16), out(POOL_WIDTH, F32),
                   out_t(DIFF_QK_WIDTH, BF16), out(DIFF_QK_WIDTH, BF16), out_t(DIFF_WIDTH, BF16)),
        grid=(nb, n // tm),
        in_specs=[tok(D_MODEL),
                  pl.BlockSpec((1, 3, D_MODEL), lambda b, i: (mod_row(b), 0, 0)),
                  pl.BlockSpec((1, D_MODEL), lambda b, i: (0, 0)),
                  _resident((D_MODEL, n_tok_major), lambda b, i: (0, 0)),
                  _resident((DIFF_QK_WIDTH + DIFF_WIDTH, D_MODEL), lambda b, i: (0, 0)),
                  pl.BlockSpec((tm, 2 * DIFF_QK_DIM), lambda b, i: (i, 0)),
                  pl.BlockSpec((tm, 2 * DIFF_QK_DIM), lambda b, i: (i, 0)),
                  pl.BlockSpec((DIFF_QK_DIM // 2, tm), lambda b, i: (0, i)),
                  pl.BlockSpec((DIFF_QK_DIM // 2, tm), lambda b, i: (0, i))],
        out_specs=(tok(NA_WIDTH), tok(NA_WIDTH), tok(NA_WIDTH), tok(POOL_WIDTH),
                   feat(DIFF_QK_WIDTH), tok(DIFF_QK_WIDTH), feat(DIFF_WIDTH)),
        compiler_params=_params("parallel", "parallel"),
        name="mixer_in_proj",
    )(x, mod, g, w, wt, cos, sin, cos_t, sin_t)


def _na_kernel(q_ref, k_ref, v_ref, kc_ref, vc_ref, bias_ref, o_ref, *, rows):
    i = pl.program_id(1)
    start = jnp.clip(i * NA_QROWS - NA_WIN_ROWS // 2, 0, rows - NA_KROWS) * GRID_W
    start = pl.multiple_of(start, GRID_W)
    n_keys = NA_KROWS * GRID_W
    q = q_ref[0]
    kb = k_ref[0, pl.ds(start, n_keys), :]
    vb = v_ref[0, pl.ds(start, n_keys), :]
    kc, vc = kc_ref[0], vc_ref[0]
    for hd in range(NA_HEADS):
        sl = slice(hd * NA_HEAD_DIM, (hd + 1) * NA_HEAD_DIM)
        qh = q[:, sl]
        s_loc = lax.dot_general(qh, kb[:, sl], _NT, preferred_element_type=F32) + bias_ref[0, hd]
        s_ctx = lax.dot_general(qh, kc[:, sl], _NT, preferred_element_type=F32)
        m = jnp.maximum(jnp.max(s_loc, axis=1, keepdims=True), jnp.max(s_ctx, axis=1, keepdims=True))
        p_loc = jnp.exp(s_loc - m)
        p_ctx = jnp.exp(s_ctx - m)
        denom = jnp.sum(p_loc, axis=1, keepdims=True) + jnp.sum(p_ctx, axis=1, keepdims=True)
        o = (jnp.dot(p_loc.astype(BF16), vb[:, sl], preferred_element_type=F32)
             + jnp.dot(p_ctx.astype(BF16), vc[:, sl], preferred_element_type=F32))
        o_ref[0, :, sl] = (o / denom).astype(BF16)


def _na(q, k, v, kc, vc, bias):
    nb, s, _ = q.shape
    rows = s // GRID_W
    n_steps = rows // NA_QROWS
    tq = NA_QROWS * GRID_W
    n_ctx = kc.shape[1]

    def pattern(b, i):
        return (jnp.where(i == 0, 0, jnp.where(i == n_steps - 1, 2, 1)), 0, 0, 0)

    return pl.pallas_call(
        functools.partial(_na_kernel, rows=rows),
        out_shape=jax.ShapeDtypeStruct((nb, s, NA_WIDTH), BF16),
        grid=(nb, n_steps),
        in_specs=[pl.BlockSpec((1, tq, NA_WIDTH), lambda b, i: (b, i, 0)),
                  _resident((1, s, NA_WIDTH), lambda b, i: (b, 0, 0)),
                  _resident((1, s, NA_WIDTH), lambda b, i: (b, 0, 0)),
                  pl.BlockSpec((1, n_ctx, NA_WIDTH), lambda b, i: (b, 0, 0)),
                  pl.BlockSpec((1, n_ctx, NA_WIDTH), lambda b, i: (b, 0, 0)),
                  pl.BlockSpec((1, NA_HEADS, tq, NA_KROWS * GRID_W), pattern)],
        out_specs=pl.BlockSpec((1, tq, NA_WIDTH), lambda b, i: (b, i, 0)),
        compiler_params=_params("parallel", "arbitrary"),
        name="neighbourhood_attn",
    )(q, k, v, kc, vc, bias)


def _na_bias_tables(rpb, rows):
    n_h = rpb.shape[0]
    n_dr = 2 * NA_WIN_ROWS - 1
    pad = GRID_W - NA_WIN_COLS
    ext = jnp.pad(rpb.astype(F32), ((0, 0), (0, 0), (pad, pad)), mode="edge")
    toe = jnp.stack([ext[:, :, GRID_W - 1 - qc: 2 * GRID_W - 1 - qc] for qc in range(GRID_W)], axis=2)
    qc = np.arange(GRID_W)[:, None]
    kc = np.arange(GRID_W)[None, :]
    win_c0 = np.clip(qc - NA_WIN_COLS // 2, 0, GRID_W - NA_WIN_COLS)
    in_cols = (kc >= win_c0) & (kc < win_c0 + NA_WIN_COLS)
    toe = jnp.where(jnp.asarray(in_cols)[None, None], toe, NEG_INF)
    masked = jnp.full((n_h, GRID_W, GRID_W), NEG_INF, F32)
    tables = []
    for r0 in (0, NA_QROWS, rows - NA_QROWS):
        start = int(np.clip(r0 - NA_WIN_ROWS // 2, 0, rows - NA_KROWS))
        q_rows = []
        for qr in range(NA_QROWS):
            r = r0 + qr
            ws = int(np.clip(r - NA_WIN_ROWS // 2, 0, rows - NA_WIN_ROWS))
            tiles = []
            for kl in range(NA_KROWS):
                kr = start + kl
                tiles.append(toe[:, kr - r + NA_WIN_ROWS - 1] if ws <= kr < ws + NA_WIN_ROWS else masked)
            q_rows.append(jnp.concatenate(tiles, axis=2))
        tables.append(jnp.concatenate(q_rows, axis=1))
    return jnp.stack(tables, axis=0)


def _diff_kernel(*refs, has_ctx, lam_init, n_kv):
    if has_ctx:
        qt_ref, k_ref, vt_ref, kc_ref, vct_ref, lam_ref, g_ref, o_ref, qpad_ref, m_ref, l_ref, acc_ref = refs
    else:
        qt_ref, k_ref, vt_ref, lam_ref, g_ref, o_ref, qpad_ref, m_ref, l_ref, acc_ref = refs
    j = pl.program_id(2)
    head_w = 2 * DIFF_QK_DIM

    def update(kblk, vtblk):
        for hd in range(DIFF_HEADS):
            kh = kblk[:, hd * head_w:(hd + 1) * head_w]
            vth = vtblk[hd * DIFF_V_DIM:(hd + 1) * DIFF_V_DIM, :]
            for st in range(2):
                idx = 2 * hd + st
                s = jnp.dot(kh, qpad_ref[idx], preferred_element_type=F32)
                m_old = m_ref[idx:idx + 1, :]
                m_new = jnp.maximum(m_old, jnp.max(s, axis=0, keepdims=True))
                alpha = jnp.exp2(m_old - m_new)
                p = jnp.exp2(s - m_new)
                l_ref[idx:idx + 1, :] = alpha * l_ref[idx:idx + 1, :] + jnp.sum(p, axis=0, keepdims=True)
                acc_ref[idx] = alpha * acc_ref[idx] + jnp.dot(vth, p.astype(BF16), preferred_element_type=F32)
                m_ref[idx:idx + 1, :] = m_new

    @pl.when(j == 0)
    def _init():
        row = lax.broadcasted_iota(jnp.int32, (head_w, qt_ref.shape[2]), 0)
        for hd in range(DIFF_HEADS):
            qh = qt_ref[0, hd * head_w:(hd + 1) * head_w, :]
            qpad_ref[2 * hd] = jnp.where(row < DIFF_QK_DIM, qh, jnp.zeros_like(qh))
            qpad_ref[2 * hd + 1] = jnp.where(row >= DIFF_QK_DIM, qh, jnp.zeros_like(qh))
        m_ref[...] = jnp.full(m_ref.shape, -jnp.inf, F32)
        l_ref[...] = jnp.zeros(l_ref.shape, F32)
        acc_ref[...] = jnp.zeros(acc_ref.shape, F32)
        if has_ctx:
            update(kc_ref[0], vct_ref[0])

    update(k_ref[0], vt_ref[0])

    @pl.when(j == n_kv - 1)
    def _finish():
        lf = lam_ref[...]
        lam = (jnp.exp(jnp.sum(lf[0:1] * lf[1:2], axis=1, keepdims=True))
               - jnp.exp(jnp.sum(lf[2:3] * lf[3:4], axis=1, keepdims=True)) + lam_init)
        for hd in range(DIFF_HEADS):
            o1 = acc_ref[2 * hd] / l_ref[2 * hd:2 * hd + 1, :]
            o2 = acc_ref[2 * hd + 1] / l_ref[2 * hd + 1:2 * hd + 2, :]
            o = o1 - lam * o2
            o = o * lax.rsqrt(jnp.mean(o * o, axis=0, keepdims=True) + EPS) * g_ref[...] * (1.0 - lam_init)
            o_ref[0, :, hd * DIFF_V_DIM:(hd + 1) * DIFF_V_DIM] = o.T.astype(BF16)


def _diff(qt, k, vt, ctx_kv, lam_vec, subln_g, lam_init):
    nb, _, n = qt.shape
    n_keys = k.shape[1]
    tq = min(DIFF_TQ, n)
    tk = min(DIFF_TK, n_keys)
    n_kv = n_keys // tk
    has_ctx = ctx_kv is not None
    in_specs = [pl.BlockSpec((1, DIFF_QK_WIDTH, tq), lambda b, i, j: (b, 0, i)),
                pl.BlockSpec((1, tk, DIFF_QK_WIDTH), lambda b, i, j: (b, j, 0)),
                pl.BlockSpec((1, DIFF_WIDTH, tk), lambda b, i, j: (b, 0, j))]
    args = [qt, k, vt]
    if has_ctx:
        n_ctx = ctx_kv[0].shape[1]
        in_specs += [pl.BlockSpec((1, n_ctx, DIFF_QK_WIDTH), lambda b, i, j: (b, 0, 0)),
                     pl.BlockSpec((1, DIFF_WIDTH, n_ctx), lambda b, i, j: (b, 0, 0))]
        args += list(ctx_kv)
    in_specs += [pl.BlockSpec((4, DIFF_QK_DIM), lambda b, i, j: (0, 0)),
                 pl.BlockSpec((DIFF_V_DIM, 1), lambda b, i, j: (0, 0))]
    args += [lam_vec, subln_g.reshape(DIFF_V_DIM, 1)]
    n_stream = 2 * DIFF_HEADS
    return pl.pallas_call(
        functools.partial(_diff_kernel, has_ctx=has_ctx, lam_init=lam_init, n_kv=n_kv),
        out_shape=jax.ShapeDtypeStruct((nb, n, DIFF_WIDTH), BF16),
        grid=(nb, n // tq, n_kv),
        in_specs=in_specs,
        out_specs=pl.BlockSpec((1, tq, DIFF_WIDTH), lambda b, i, j: (b, i, 0)),
        scratch_shapes=[pltpu.VMEM((n_stream, 2 * DIFF_QK_DIM, tq), BF16),
                        pltpu.VMEM((n_stream, tq), F32),
                        pltpu.VMEM((n_stream, tq), F32),
                        pltpu.VMEM((n_stream, DIFF_V_DIM, tq), F32)],
        compiler_params=_params("parallel", "parallel", "arbitrary"),
        name="diff_attn",
    )(*args)


def _ctx_attn_kernel(q_ref, k_ref, v_ref, o_ref):
    q, k, v = q_ref[0], k_ref[0], v_ref[0]
    for hd in range(NA_HEADS):
        sl = slice(hd * NA_HEAD_DIM, (hd + 1) * NA_HEAD_DIM)
        s = lax.dot_general(q[:, sl], k[:, sl], _NT, preferred_element_type=F32)
        p = jnp.exp(s - jnp.max(s, axis=1, keepdims=True))
        o = jnp.dot(p.astype(BF16), v[:, sl], preferred_element_type=F32)
        o_ref[0, :, sl] = (o / jnp.sum(p, axis=1, keepdims=True)).astype(BF16)


def _ctx_attn(q, k, v):
    nb, n, w = q.shape
    spec = pl.BlockSpec((1, n, w), lambda b: (b, 0, 0))
    return pl.pallas_call(
        _ctx_attn_kernel,
        out_shape=jax.ShapeDtypeStruct(q.shape, BF16),
        grid=(nb,),
        in_specs=[spec, spec, spec],
        out_specs=spec,
        compiler_params=_params("parallel"),
        name="ctx_dense_attn",
    )(q, k, v)


def _outproj_kernel(x_ref, mod_ref, g_ref, a_ref, pprev_ref, pcur_ref, pnext_ref, c_ref,
                    pw_ref, ps_ref, w_ref, o_ref, *, tm, seq_len):
    i = pl.program_id(1)
    n = tm + 2 * POOL_HALO
    u = jnp.concatenate([pprev_ref[0], pcur_ref[0], pnext_ref[0]], axis=0)
    t_ext = i * tm - POOL_HALO + lax.broadcasted_iota(jnp.int32, (n, 1), 0)
    u = jnp.where((t_ext >= 0) & (t_ext < seq_len), u, 0.0)
    s2 = u + pltpu.roll(u, 1, 0)
    s4 = pltpu.roll(s2, 1, 0) + pltpu.roll(s2, n - 1, 0)
    s8 = pltpu.roll(s4, 2, 0) + pltpu.roll(s4, n - 2, 0)
    s16 = pltpu.roll(s8, 4, 0) + pltpu.roll(s8, n - 4, 0)
    cur = slice(POOL_HALO, POOL_HALO + tm)
    group = lax.broadcasted_iota(jnp.int32, (1, POOL_WIDTH), 1) // POOL_GROUP_DIM
    wsum = jnp.where(group == 0, s2[cur], jnp.where(group == 1, s4[cur], jnp.where(group == 2, s8[cur], s16[cur])))
    half = jnp.where(group == 0, POOL_WINDOWS[0] // 2,
                     jnp.where(group == 1, POOL_WINDOWS[1] // 2,
                               jnp.where(group == 2, POOL_WINDOWS[2] // 2, POOL_WINDOWS[3] // 2)))
    t = i * tm + lax.broadcasted_iota(jnp.int32, (tm, 1), 0)
    count = (jnp.clip(t + half, 0, seq_len) - jnp.clip(t - half, 0, seq_len)).astype(F32)
    pooled = (wsum / count - u[cur]).astype(BF16)
    b = jnp.dot(pooled, pw_ref[...], preferred_element_type=F32) * ps_ref[...]
    cat = jnp.concatenate([a_ref[0], b.astype(BF16), c_ref[0]], axis=1)
    y = jnp.dot(cat, w_ref[...], preferred_element_type=F32)
    o_ref[0] = x_ref[0] + mod_ref[0, 2:3, :] * _rms(y, g_ref[...])


def _outproj(x, mod, mod_row, g, a, pin, c, pool_bd, pool_scale, w):
    nb, n, _ = x.shape
    tm = min(TOKEN_TILE, n)
    halo_per_tile = tm // POOL_HALO
    n_halo = n // POOL_HALO
    tok = lambda width: pl.BlockSpec((1, tm, width), lambda b, i: (b, i, 0))
    return pl.pallas_call(
        functools.partial(_outproj_kernel, tm=tm, seq_len=n),
        out_shape=jax.ShapeDtypeStruct(x.shape, F32),
        grid=(nb, n // tm),
        in_specs=[tok(D_MODEL),
                  pl.BlockSpec((1, 3, D_MODEL), lambda b, i: (mod_row(b), 0, 0)),
                  pl.BlockSpec((1, D_MODEL), lambda b, i: (0, 0)),
                  tok(NA_WIDTH),
                  pl.BlockSpec((1, POOL_HALO, POOL_WIDTH),
                               lambda b, i: (b, jnp.maximum(i * halo_per_tile - 1, 0), 0)),
                  tok(POOL_WIDTH),
                  pl.BlockSpec((1, POOL_HALO, POOL_WIDTH),
                               lambda b, i: (b, jnp.minimum((i + 1) * halo_per_tile, n_halo - 1), 0)),
                  tok(DIFF_WIDTH),
                  pl.BlockSpec((POOL_WIDTH, POOL_WIDTH), lambda b, i: (0, 0)),
                  pl.BlockSpec((1, POOL_WIDTH), lambda b, i: (0, 0)),
                  _resident((MIX_WIDTH, D_MODEL), lambda b, i: (0, 0))],
        out_specs=tok(D_MODEL),
        compiler_params=_params("parallel", "parallel"),
        name="mixer_out_proj",
    )(x, mod, g, a, pin, pin, pin, c, pool_bd, pool_scale, w)


def _rope_tables(n_tokens):
    t = jnp.arange(n_tokens, dtype=jnp.int32)
    row = (t // GRID_W).astype(F32)
    col = (t % GRID_W).astype(F32)
    n_freq = DIFF_QK_DIM // 4
    inv_freq = jnp.power(ROPE_THETA, -jnp.arange(n_freq, dtype=F32) / n_freq)
    ang = jnp.concatenate([row[:, None] * inv_freq, col[:, None] * inv_freq], axis=-1)
    cos, sin = jnp.cos(ang), jnp.sin(ang)
    return jnp.tile(cos, (1, 4)), jnp.concatenate([-sin, sin, -sin, sin], axis=-1), cos.T, sin.T


def _identity_rope_tables(n_tokens):
    half = DIFF_QK_DIM // 2
    return (jnp.ones((n_tokens, 2 * DIFF_QK_DIM), F32), jnp.zeros((n_tokens, 2 * DIFF_QK_DIM), F32),
            jnp.ones((half, n_tokens), F32), jnp.zeros((half, n_tokens), F32))


def _block_diag(w):
    n_g, cg, _ = w.shape
    eye = jnp.eye(n_g, dtype=w.dtype)
    return (eye[:, None, :, None] * w[:, :, None, :]).reshape(n_g * cg, n_g * cg)


def kernel(x, c, ctx, c_ctx, w_ada, b_ada, norm_g, ffn_w1, ffn_w2, w_in, w_out, na_rpb, pool_w, pool_scale,
           diff_lambda, diff_subln_g):
    nb, s, _ = x.shape
    n_ctx = ctx.shape[1]
    assert nb <= 7 and s % (GRID_W * NA_QROWS) == 0 and s % DIFF_TK == 0 and n_ctx % POOL_HALO == 0

    w1 = ffn_w1.astype(BF16)
    w2 = ffn_w2.astype(BF16)
    q_lo = 3 * NA_WIDTH + POOL_WIDTH
    k_lo = q_lo + DIFF_QK_WIDTH
    v_lo = k_lo + DIFF_QK_WIDTH
    w_in_b = w_in.astype(BF16)
    w_tok = jnp.concatenate([w_in_b[:, :, :q_lo], w_in_b[:, :, k_lo:v_lo]], axis=2)
    w_feat = jnp.swapaxes(jnp.concatenate([w_in_b[:, :, q_lo:k_lo], w_in_b[:, :, v_lo:]], axis=2), 1, 2)
    w_out_b = w_out.astype(BF16)

    ctx_row = nb
    cvec = jnp.zeros((8, D_MODEL), F32).at[:nb].set(c).at[ctx_row].set(c_ctx)
    mods = _ada(cvec, w_ada, b_ada).reshape(DEPTH, 8, N_MOD, D_MODEL)
    lat_row = lambda b: b
    ctx_mod_row = lambda b: ctx_row

    rope_lat = _rope_tables(s)
    rope_ctx = _identity_rope_tables(n_ctx)

    x_lat, x_ctx = x, ctx
    for layer in range(DEPTH):
        last = layer == DEPTH - 1
        lam_init = 0.8 - 0.6 * math.exp(-0.3 * layer)
        m = mods[layer]
        g = norm_g[layer]
        bias = _na_bias_tables(na_rpb[layer], s // GRID_W)
        pool_bd = _block_diag(pool_w[layer]).astype(BF16)
        pool_sc = pool_scale[layer].reshape(1, POOL_WIDTH)

        x_lat = _ffn(x_lat, m[:, 0:3], lat_row, g[0:2], w1[layer, 0], w2[layer, 0], 0.5)
        x_ctx = _ffn(x_ctx, m[:, 0:3], ctx_mod_row, g[0:2], w1[layer, 0], w2[layer, 0], 0.5)

        aq, ak, av, pin, dq, dk, dv = _inproj(x_lat, m[:, 3:6], lat_row, g[2:3], w_tok[layer], w_feat[layer],
                                              rope_lat)
        aqc, akc, avc, pinc, dqc, dkc, dvc = _inproj(x_ctx, m[:, 3:6], ctx_mod_row, g[2:3], w_tok[layer],
                                                     w_feat[layer], rope_ctx)
        a_lat = _na(aq, ak, av, akc, avc, bias)
        c_lat = _diff(dq, dk, dv, (dkc, dvc), diff_lambda[layer], diff_subln_g[layer], lam_init)
        x_lat_mixed = _outproj(x_lat, m[:, 3:6], lat_row, g[3:4], a_lat, pin, c_lat, pool_bd, pool_sc,
                               w_out_b[layer])
        x_lat = _ffn(x_lat_mixed, m[:, 6:9], lat_row, g[4:6], w1[layer, 1], w2[layer, 1], 0.5)

        if not last:
            a_ctx = _ctx_attn(aqc, akc, avc)
            c_ctx_out = _diff(dqc, dkc, dvc, None, diff_lambda[layer], diff_subln_g[layer], lam_init)
            x_ctx_mixed = _outproj(x_ctx, m[:, 3:6], ctx_mod_row, g[3:4], a_ctx, pinc, c_ctx_out, pool_bd, pool_sc,
                                   w_out_b[layer])
            x_ctx = _ffn(x_ctx_mixed, m[:, 6:9], ctx_mod_row, g[4:6], w1[layer, 1], w2[layer, 1], 0.5)
    return x_lat
```

```python
import functools
import math

import numpy as np
import jax
import jax.numpy as jnp
from jax import lax
from jax.experimental import pallas as pl
from jax.experimental.pallas import tpu as pltpu

D_MODEL = 1024
DEPTH = 2
GRID_W = 64
N_MOD = 9
D_FF = 2816
EPS = 1e-6
NEG_INF = -1e30
ROPE_THETA = 10000.0

NA_HEADS = 4
NA_HEAD_DIM = 64
NA_WIN_ROWS = 8
NA_WIN_COLS = 16
NA_WIDTH = NA_HEADS * NA_HEAD_DIM

POOL_WINDOWS = (2, 4, 8, 16)
POOL_GROUP_DIM = 64
POOL_WIDTH = len(POOL_WINDOWS) * POOL_GROUP_DIM
POOL_HALO = max(POOL_WINDOWS) // 2

DIFF_HEADS = 4
DIFF_QK_DIM = 64
DIFF_V_DIM = 2 * DIFF_QK_DIM
DIFF_QK_WIDTH = DIFF_HEADS * 2 * DIFF_QK_DIM
DIFF_WIDTH = DIFF_HEADS * DIFF_V_DIM
DIFF_Q_SCALE = DIFF_QK_DIM ** -0.5 * math.log2(math.e)

MIX_WIDTH = NA_WIDTH + POOL_WIDTH + DIFF_WIDTH
IN_WIDTH = 3 * NA_WIDTH + POOL_WIDTH + 2 * DIFF_QK_WIDTH + DIFF_WIDTH

BF16 = jnp.bfloat16
F32 = jnp.float32

V7X_VMEM_BYTES = 64 * 1024 * 1024
VMEM_LIMIT = V7X_VMEM_BYTES * 3 // 4

TOKEN_TILE = 512
FF_CHUNK = 256
ADA_COLS = 1152
NA_QROWS = 4
NA_KROWS = NA_QROWS + NA_WIN_ROWS
DIFF_TQ = 512
DIFF_TK = 1024

_NT = (((1,), (1,)), ((), ()))


def _params(*sem):
    return pltpu.CompilerParams(dimension_semantics=sem, vmem_limit_bytes=VMEM_LIMIT)


def _resident(shape, index_map):
    return pl.BlockSpec(shape, index_map, pipeline_mode=pl.Buffered(1))


def _rms(x, g):
    return x * lax.rsqrt(jnp.mean(x * x, axis=-1, keepdims=True) + EPS) * g


def _modulate(x, mod_ref, g):
    shift, scale = mod_ref[0, 0:1, :], mod_ref[0, 1:2, :]
    return _rms(x, g) * (1.0 + scale) + shift


def _ada_kernel(c_ref, w_ref, b_ref, o_ref):
    cv = c_ref[...]
    o_ref[0] = jnp.dot(cv * jax.nn.sigmoid(cv), w_ref[0], preferred_element_type=F32,
                       precision=lax.Precision.HIGHEST) + b_ref[0]


def _ada(cvec, w_ada, b_ada):
    n_out = N_MOD * D_MODEL
    return pl.pallas_call(
        _ada_kernel,
        out_shape=jax.ShapeDtypeStruct((DEPTH, 8, n_out), F32),
        grid=(DEPTH, n_out // ADA_COLS),
        in_specs=[pl.BlockSpec((8, D_MODEL), lambda l, j: (0, 0)),
                  pl.BlockSpec((1, D_MODEL, ADA_COLS), lambda l, j: (l, 0, j)),
                  pl.BlockSpec((1, 1, ADA_COLS), lambda l, j: (l, 0, j))],
        out_specs=pl.BlockSpec((1, 8, ADA_COLS), lambda l, j: (l, 0, j)),
        compiler_params=_params("parallel", "parallel"),
        name="ada_mod",
    )(cvec, w_ada, b_ada.reshape(DEPTH, 1, n_out))


def _ffn_kernel(x_ref, mod_ref, g_ref, w1_ref, w2_ref, o_ref, act_ref, *, res_w):
    x = x_ref[0]
    h = _modulate(x, mod_ref, g_ref[0:1, :]).astype(BF16)
    for c in range(D_FF // FF_CHUNK):
        lo = c * FF_CHUNK
        a = jnp.dot(h, w1_ref[:, lo:lo + FF_CHUNK], preferred_element_type=F32)
        b = jnp.dot(h, w1_ref[:, D_FF + lo:D_FF + lo + FF_CHUNK], preferred_element_type=F32)
        act_ref[:, lo:lo + FF_CHUNK] = (a * jax.nn.sigmoid(a) * b).astype(BF16)
    y = jnp.dot(act_ref[...], w2_ref[...], preferred_element_type=F32)
    o_ref[0] = x + res_w * mod_ref[0, 2:3, :] * _rms(y, g_ref[1:2, :])


def _ffn(x, mod, mod_row, g, w1, w2, res_w):
    nb, n, _ = x.shape
    tm = min(TOKEN_TILE, n)
    return pl.pallas_call(
        functools.partial(_ffn_kernel, res_w=res_w),
        out_shape=jax.ShapeDtypeStruct(x.shape, F32),
        grid=(nb, n // tm),
        in_specs=[pl.BlockSpec((1, tm, D_MODEL), lambda b, i: (b, i, 0)),
                  pl.BlockSpec((1, 3, D_MODEL), lambda b, i: (mod_row(b), 0, 0)),
                  pl.BlockSpec((2, D_MODEL), lambda b, i: (0, 0)),
                  _resident((D_MODEL, 2 * D_FF), lambda b, i: (0, 0)),
                  _resident((D_FF, D_MODEL), lambda b, i: (0, 0))],
        out_specs=pl.BlockSpec((1, tm, D_MODEL), lambda b, i: (b, i, 0)),
        scratch_shapes=[pltpu.VMEM((tm, D_FF), BF16)],
        compiler_params=_params("parallel", "parallel"),
        name="ffn",
    )(x, mod, g, w1, w2)


def _inproj_kernel(x_ref, mod_ref, g_ref, w_ref, wt_ref, cos_ref, sin_ref, cos_t_ref, sin_t_ref,
                   naq_ref, nak_ref, nav_ref, pin_ref, dqt_ref, dk_ref, dvt_ref):
    x = x_ref[0]
    h = _modulate(x, mod_ref, g_ref[0:1, :]).astype(BF16)

    def proj(lo, width):
        return jnp.dot(h, w_ref[:, lo:lo + width], preferred_element_type=F32)

    naq_ref[0] = (proj(0, NA_WIDTH) * (NA_HEAD_DIM ** -0.5)).astype(BF16)
    nak_ref[0] = proj(NA_WIDTH, NA_WIDTH).astype(BF16)
    nav_ref[0] = proj(2 * NA_WIDTH, NA_WIDTH).astype(BF16)
    pin_ref[0] = proj(3 * NA_WIDTH, POOL_WIDTH)

    cos, sin = cos_ref[...], sin_ref[...]
    lane = lax.broadcasted_iota(jnp.int32, cos.shape, 1)
    half = DIFF_QK_DIM // 2
    first_half = (lane % DIFF_QK_DIM) < half
    head_w = 2 * DIFF_QK_DIM
    k = proj(3 * NA_WIDTH + POOL_WIDTH, DIFF_QK_WIDTH)
    for hd in range(DIFF_HEADS):
        t = k[:, hd * head_w:(hd + 1) * head_w]
        partner = jnp.where(first_half, pltpu.roll(t, head_w - half, 1), pltpu.roll(t, half, 1))
        dk_ref[0, :, hd * head_w:(hd + 1) * head_w] = (t * cos + partner * sin).astype(BF16)

    qv = lax.dot_general(wt_ref[...], h, _NT, preferred_element_type=F32)
    cos_t, sin_t = cos_t_ref[...], sin_t_ref[...]
    for st in range(2 * DIFF_HEADS):
        base = st * DIFF_QK_DIM
        x1 = qv[base:base + half] * DIFF_Q_SCALE
        x2 = qv[base + half:base + DIFF_QK_DIM] * DIFF_Q_SCALE
        dqt_ref[0, base:base + half, :] = (x1 * cos_t - x2 * sin_t).astype(BF16)
        dqt_ref[0, base + half:base + DIFF_QK_DIM, :] = (x1 * sin_t + x2 * cos_t).astype(BF16)
    dvt_ref[0] = qv[DIFF_QK_WIDTH:].astype(BF16)


def _inproj(x, mod, mod_row, g, w, wt, rope):
    nb, n, _ = x.shape
    tm = min(TOKEN_TILE, n)
    cos, sin, cos_t, sin_t = rope
    n_tok_major = w.shape[1]
    tok = lambda width: pl.BlockSpec((1, tm, width), lambda b, i: (b, i, 0))
    feat = lambda width: pl.BlockSpec((1, width, tm), lambda b, i: (b, 0, i))
    out = lambda width, dt: jax.ShapeDtypeStruct((nb, n, width), dt)
    out_t = lambda width, dt: jax.ShapeDtypeStruct((nb, width, n), dt)
    return pl.pallas_call(
        _inproj_kernel,
        out_shape=(out(NA_WIDTH, BF16), out(NA_WIDTH, BF16), out(NA_WIDTH, BF16), out(POOL_WIDTH, F32),
                   out_t(DIFF_QK_WIDTH, BF16), out(DIFF_QK_WIDTH, BF16), out_t(DIFF_WIDTH, BF16)),
        grid=(nb, n // tm),
        in_specs=[tok(D_MODEL),
                  pl.BlockSpec((1, 3, D_MODEL), lambda b, i: (mod_row(b), 0, 0)),
                  pl.BlockSpec((1, D_MODEL), lambda b, i: (0, 0)),
                  _resident((D_MODEL, n_tok_major), lambda b, i: (0, 0)),
                  _resident((DIFF_QK_WIDTH + DIFF_WIDTH, D_MODEL), lambda b, i: (0, 0)),
                  pl.BlockSpec((tm, 2 * DIFF_QK_DIM), lambda b, i: (i, 0)),
                  pl.BlockSpec((tm, 2 * DIFF_QK_DIM), lambda b, i: (i, 0)),
                  pl.BlockSpec((DIFF_QK_DIM // 2, tm), lambda b, i: (0, i)),
                  pl.BlockSpec((DIFF_QK_DIM // 2, tm), lambda b, i: (0, i))],
        out_specs=(tok(NA_WIDTH), tok(NA_WIDTH), tok(NA_WIDTH), tok(POOL_WIDTH),
                   feat(DIFF_QK_WIDTH), tok(DIFF_QK_WIDTH), feat(DIFF_WIDTH)),
        compiler_params=_params("parallel", "parallel"),
        name="mixer_in_proj",
    )(x, mod, g, w, wt, cos, sin, cos_t, sin_t)


def _na_kernel(q_ref, k_ref, v_ref, kc_ref, vc_ref, bias_ref, o_ref, *, rows):
    i = pl.program_id(1)
    start = jnp.clip(i * NA_QROWS - NA_WIN_ROWS // 2, 0, rows - NA_KROWS) * GRID_W
    start = pl.multiple_of(start, GRID_W)
    n_keys = NA_KROWS * GRID_W
    q = q_ref[0]
    kb = k_ref[0, pl.ds(start, n_keys), :]
    vb = v_ref[0, pl.ds(start, n_keys), :]
    kc, vc = kc_ref[0], vc_ref[0]
    for hd in range(NA_HEADS):
        sl = slice(hd * NA_HEAD_DIM, (hd + 1) * NA_HEAD_DIM)
        qh = q[:, sl]
        s_loc = lax.dot_general(qh, kb[:, sl], _NT, preferred_element_type=F32) + bias_ref[0, hd]
        s_ctx = lax.dot_general(qh, kc[:, sl], _NT, preferred_element_type=F32)
        m = jnp.maximum(jnp.max(s_loc, axis=1, keepdims=True), jnp.max(s_ctx, axis=1, keepdims=True))
        p_loc = jnp.exp(s_loc - m)
        p_ctx = jnp.exp(s_ctx - m)
        denom = jnp.sum(p_loc, axis=1, keepdims=True) + jnp.sum(p_ctx, axis=1, keepdims=True)
        o = (jnp.dot(p_loc.astype(BF16), vb[:, sl], preferred_element_type=F32)
             + jnp.dot(p_ctx.astype(BF16), vc[:, sl], preferred_element_type=F32))
        o_ref[0, :, sl] = (o / denom).astype(BF16)


def _na(q, k, v, kc, vc, bias):
    nb, s, _ = q.shape
    rows = s // GRID_W
    n_steps = rows // NA_QROWS
    tq = NA_QROWS * GRID_W
    n_ctx = kc.shape[1]

    def pattern(b, i):
        return (jnp.where(i == 0, 0, jnp.where(i == n_steps - 1, 2, 1)), 0, 0, 0)

    return pl.pallas_call(
        functools.partial(_na_kernel, rows=rows),
        out_shape=jax.ShapeDtypeStruct((nb, s, NA_WIDTH), BF16),
        grid=(nb, n_steps),
        in_specs=[pl.BlockSpec((1, tq, NA_WIDTH), lambda b, i: (b, i, 0)),
                  _resident((1, s, NA_WIDTH), lambda b, i: (b, 0, 0)),
                  _resident((1, s, NA_WIDTH), lambda b, i: (b, 0, 0)),
                  pl.BlockSpec((1, n_ctx, NA_WIDTH), lambda b, i: (b, 0, 0)),
                  pl.BlockSpec((1, n_ctx, NA_WIDTH), lambda b, i: (b, 0, 0)),
                  pl.BlockSpec((1, NA_HEADS, tq, NA_KROWS * GRID_W), pattern)],
        out_specs=pl.BlockSpec((1, tq, NA_WIDTH), lambda b, i: (b, i, 0)),
        compiler_params=_params("parallel", "arbitrary"),
        name="neighbourhood_attn",
    )(q, k, v, kc, vc, bias)


def _na_bias_tables(rpb, rows):
    n_h = rpb.shape[0]
    n_dr = 2 * NA_WIN_ROWS - 1
    pad = GRID_W - NA_WIN_COLS
    ext = jnp.pad(rpb.astype(F32), ((0, 0), (0, 0), (pad, pad)), mode="edge")
    toe = jnp.stack([ext[:, :, GRID_W - 1 - qc: 2 * GRID_W - 1 - qc] for qc in range(GRID_W)], axis=2)
    qc = np.arange(GRID_W)[:, None]
    kc = np.arange(GRID_W)[None, :]
    win_c0 = np.clip(qc - NA_WIN_COLS // 2, 0, GRID_W - NA_WIN_COLS)
    in_cols = (kc >= win_c0) & (kc < win_c0 + NA_WIN_COLS)
    toe = jnp.where(jnp.asarray(in_cols)[None, None], toe, NEG_INF)
    masked = jnp.full((n_h, GRID_W, GRID_W), NEG_INF, F32)
    tables = []
    for r0 in (0, NA_QROWS, rows - NA_QROWS):
        start = int(np.clip(r0 - NA_WIN_ROWS // 2, 0, rows - NA_KROWS))
        q_rows = []
        for qr in range(NA_QROWS):
            r = r0 + qr
            ws = int(np.clip(r - NA_WIN_ROWS // 2, 0, rows - NA_WIN_ROWS))
            tiles = []
            for kl in range(NA_KROWS):
                kr = start + kl
                tiles.append(toe[:, kr - r + NA_WIN_ROWS - 1] if ws <= kr < ws + NA_WIN_ROWS else masked)
            q_rows.append(jnp.concatenate(tiles, axis=2))
        tables.append(jnp.concatenate(q_rows, axis=1))
    return jnp.stack(tables, axis=0)


def _diff_kernel(*refs, has_ctx, lam_init, n_kv):
    if has_ctx:
        qt_ref, k_ref, vt_ref, kc_ref, vct_ref, lam_ref, g_ref, o_ref, qpad_ref, m_ref, l_ref, acc_ref = refs
    else:
        qt_ref, k_ref, vt_ref, lam_ref, g_ref, o_ref, qpad_ref, m_ref, l_ref, acc_ref = refs
    j = pl.program_id(2)
    head_w = 2 * DIFF_QK_DIM

    n_stream = 2 * DIFF_HEADS

    def update(kblk, vtblk):
        def scores(idx):
            kh = kblk[:, (idx // 2) * head_w:(idx // 2 + 1) * head_w]
            return jnp.dot(kh, qpad_ref[idx], preferred_element_type=F32)

        def accumulate(idx, alpha, p):
            vth = vtblk[(idx // 2) * DIFF_V_DIM:(idx // 2 + 1) * DIFF_V_DIM, :]
            acc_ref[idx] = alpha * acc_ref[idx] + jnp.dot(vth, p, preferred_element_type=F32)

        s_next = scores(0)
        pending = None
        for idx in range(n_stream):
            s = s_next
            if idx + 1 < n_stream:
                s_next = scores(idx + 1)
            if pending is not None:
                accumulate(*pending)
            m_old = m_ref[idx:idx + 1, :]
            m_new = jnp.maximum(m_old, jnp.max(s, axis=0, keepdims=True))
            alpha = jnp.exp2(m_old - m_new)
            p = jnp.exp2(s - m_new)
            l_ref[idx:idx + 1, :] = alpha * l_ref[idx:idx + 1, :] + jnp.sum(p, axis=0, keepdims=True)
            m_ref[idx:idx + 1, :] = m_new
            pending = (idx, alpha, p.astype(BF16))
        accumulate(*pending)

    @pl.when(j == 0)
    def _init():
        row = lax.broadcasted_iota(jnp.int32, (head_w, qt_ref.shape[2]), 0)
        for hd in range(DIFF_HEADS):
            qh = qt_ref[0, hd * head_w:(hd + 1) * head_w, :]
            qpad_ref[2 * hd] = jnp.where(row < DIFF_QK_DIM, qh, jnp.zeros_like(qh))
            qpad_ref[2 * hd + 1] = jnp.where(row >= DIFF_QK_DIM, qh, jnp.zeros_like(qh))
        m_ref[...] = jnp.full(m_ref.shape, -jnp.inf, F32)
        l_ref[...] = jnp.zeros(l_ref.shape, F32)
        acc_ref[...] = jnp.zeros(acc_ref.shape, F32)
        if has_ctx:
            update(kc_ref[0], vct_ref[0])

    update(k_ref[0], vt_ref[0])

    @pl.when(j == n_kv - 1)
    def _finish():
        lf = lam_ref[...]
        lam = (jnp.exp(jnp.sum(lf[0:1] * lf[1:2], axis=1, keepdims=True))
               - jnp.exp(jnp.sum(lf[2:3] * lf[3:4], axis=1, keepdims=True)) + lam_init)
        for hd in range(DIFF_HEADS):
            o1 = acc_ref[2 * hd] / l_ref[2 * hd:2 * hd + 1, :]
            o2 = acc_ref[2 * hd + 1] / l_ref[2 * hd + 1:2 * hd + 2, :]
            o = o1 - lam * o2
            o = o * lax.rsqrt(jnp.mean(o * o, axis=0, keepdims=True) + EPS) * g_ref[...] * (1.0 - lam_init)
            o_ref[0, :, hd * DIFF_V_DIM:(hd + 1) * DIFF_V_DIM] = o.T.astype(BF16)


def _diff(qt, k, vt, ctx_kv, lam_vec, subln_g, lam_init):
    nb, _, n = qt.shape
    n_keys = k.shape[1]
    tq = min(DIFF_TQ, n)
    tk = min(DIFF_TK, n_keys)
    n_kv = n_keys // tk
    has_ctx = ctx_kv is not None
    in_specs = [pl.BlockSpec((1, DIFF_QK_WIDTH, tq), lambda b, i, j: (b, 0, i)),
                pl.BlockSpec((1, tk, DIFF_QK_WIDTH), lambda b, i, j: (b, j, 0)),
                pl.BlockSpec((1, DIFF_WIDTH, tk), lambda b, i, j: (b, 0, j))]
    args = [qt, k, vt]
    if has_ctx:
        n_ctx = ctx_kv[0].shape[1]
        in_specs += [pl.BlockSpec((1, n_ctx, DIFF_QK_WIDTH), lambda b, i, j: (b, 0, 0)),
                     pl.BlockSpec((1, DIFF_WIDTH, n_ctx), lambda b, i, j: (b, 0, 0))]
        args += list(ctx_kv)
    in_specs += [pl.BlockSpec((4, DIFF_QK_DIM), lambda b, i, j: (0, 0)),
                 pl.BlockSpec((DIFF_V_DIM, 1), lambda b, i, j: (0, 0))]
    args += [lam_vec, subln_g.reshape(DIFF_V_DIM, 1)]
    n_stream = 2 * DIFF_HEADS
    return pl.pallas_call(
        functools.partial(_diff_kernel, has_ctx=has_ctx, lam_init=lam_init, n_kv=n_kv),
        out_shape=jax.ShapeDtypeStruct((nb, n, DIFF_WIDTH), BF16),
        grid=(nb, n // tq, n_kv),
        in_specs=in_specs,
        out_specs=pl.BlockSpec((1, tq, DIFF_WIDTH), lambda b, i, j: (b, i, 0)),
        scratch_shapes=[pltpu.VMEM((n_stream, 2 * DIFF_QK_DIM, tq), BF16),
                        pltpu.VMEM((n_stream, tq), F32),
                        pltpu.VMEM((n_stream, tq), F32),
                        pltpu.VMEM((n_stream, DIFF_V_DIM, tq), F32)],
        compiler_params=_params("parallel", "parallel", "arbitrary"),
        name="diff_attn",
    )(*args)


def _ctx_attn_kernel(q_ref, k_ref, v_ref, o_ref):
    q, k, v = q_ref[0], k_ref[0], v_ref[0]
    for hd in range(NA_HEADS):
        sl = slice(hd * NA_HEAD_DIM, (hd + 1) * NA_HEAD_DIM)
        s = lax.dot_general(q[:, sl], k[:, sl], _NT, preferred_element_type=F32)
        p = jnp.exp(s - jnp.max(s, axis=1, keepdims=True))
        o = jnp.dot(p.astype(BF16), v[:, sl], preferred_element_type=F32)
        o_ref[0, :, sl] = (o / jnp.sum(p, axis=1, keepdims=True)).astype(BF16)


def _ctx_attn(q, k, v):
    nb, n, w = q.shape
    spec = pl.BlockSpec((1, n, w), lambda b: (b, 0, 0))
    return pl.pallas_call(
        _ctx_attn_kernel,
        out_shape=jax.ShapeDtypeStruct(q.shape, BF16),
        grid=(nb,),
        in_specs=[spec, spec, spec],
        out_specs=spec,
        compiler_params=_params("parallel"),
        name="ctx_dense_attn",
    )(q, k, v)


def _outproj_kernel(x_ref, mod_ref, g_ref, a_ref, pprev_ref, pcur_ref, pnext_ref, c_ref,
                    pw_ref, ps_ref, w_ref, o_ref, *, tm, seq_len):
    i = pl.program_id(1)
    n = tm + 2 * POOL_HALO
    u = jnp.concatenate([pprev_ref[0], pcur_ref[0], pnext_ref[0]], axis=0)
    t_ext = i * tm - POOL_HALO + lax.broadcasted_iota(jnp.int32, (n, 1), 0)
    u = jnp.where((t_ext >= 0) & (t_ext < seq_len), u, 0.0)
    s2 = u + pltpu.roll(u, 1, 0)
    s4 = pltpu.roll(s2, 1, 0) + pltpu.roll(s2, n - 1, 0)
    s8 = pltpu.roll(s4, 2, 0) + pltpu.roll(s4, n - 2, 0)
    s16 = pltpu.roll(s8, 4, 0) + pltpu.roll(s8, n - 4, 0)
    cur = slice(POOL_HALO, POOL_HALO + tm)
    group = lax.broadcasted_iota(jnp.int32, (1, POOL_WIDTH), 1) // POOL_GROUP_DIM
    wsum = jnp.where(group == 0, s2[cur], jnp.where(group == 1, s4[cur], jnp.where(group == 2, s8[cur], s16[cur])))
    half = jnp.where(group == 0, POOL_WINDOWS[0] // 2,
                     jnp.where(group == 1, POOL_WINDOWS[1] // 2,
                               jnp.where(group == 2, POOL_WINDOWS[2] // 2, POOL_WINDOWS[3] // 2)))
    t = i * tm + lax.broadcasted_iota(jnp.int32, (tm, 1), 0)
    count = (jnp.clip(t + half, 0, seq_len) - jnp.clip(t - half, 0, seq_len)).astype(F32)
    pooled = (wsum / count - u[cur]).astype(BF16)
    b = jnp.dot(pooled, pw_ref[...], preferred_element_type=F32) * ps_ref[...]
    cat = jnp.concatenate([a_ref[0], b.astype(BF16), c_ref[0]], axis=1)
    y = jnp.dot(cat, w_ref[...], preferred_element_type=F32)
    o_ref[0] = x_ref[0] + mod_ref[0, 2:3, :] * _rms(y, g_ref[...])


def _outproj(x, mod, mod_row, g, a, pin, c, pool_bd, pool_scale, w):
    nb, n, _ = x.shape
    tm = min(TOKEN_TILE, n)
    halo_per_tile = tm // POOL_HALO
    n_halo = n // POOL_HALO
    tok = lambda width: pl.BlockSpec((1, tm, width), lambda b, i: (b, i, 0))
    return pl.pallas_call(
        functools.partial(_outproj_kernel, tm=tm, seq_len=n),
        out_shape=jax.ShapeDtypeStruct(x.shape, F32),
        grid=(nb, n // tm),
        in_specs=[tok(D_MODEL),
                  pl.BlockSpec((1, 3, D_MODEL), lambda b, i: (mod_row(b), 0, 0)),
                  pl.BlockSpec((1, D_MODEL), lambda b, i: (0, 0)),
                  tok(NA_WIDTH),
                  pl.BlockSpec((1, POOL_HALO, POOL_WIDTH),
                               lambda b, i: (b, jnp.maximum(i * halo_per_tile - 1, 0), 0)),
                  tok(POOL_WIDTH),
                  pl.BlockSpec((1, POOL_HALO, POOL_WIDTH),
                               lambda b, i: (b, jnp.minimum((i + 1) * halo_per_tile, n_halo - 1), 0)),
                  tok(DIFF_WIDTH),
                  pl.BlockSpec((POOL_WIDTH, POOL_WIDTH), lambda b, i: (0, 0)),
                  pl.BlockSpec((1, POOL_WIDTH), lambda b, i: (0, 0)),
                  _resident((MIX_WIDTH, D_MODEL), lambda b, i: (0, 0))],
        out_specs=tok(D_MODEL),
        compiler_params=_params("parallel", "parallel"),
        name="mixer_out_proj",
    )(x, mod, g, a, pin, pin, pin, c, pool_bd, pool_scale, w)


def _rope_tables(n_tokens):
    t = jnp.arange(n_tokens, dtype=jnp.int32)
    row = (t // GRID_W).astype(F32)
    col = (t % GRID_W).astype(F32)
    n_freq = DIFF_QK_DIM // 4
    inv_freq = jnp.power(ROPE_THETA, -jnp.arange(n_freq, dtype=F32) / n_freq)
    ang = jnp.concatenate([row[:, None] * inv_freq, col[:, None] * inv_freq], axis=-1)
    cos, sin = jnp.cos(ang), jnp.sin(ang)
    return jnp.tile(cos, (1, 4)), jnp.concatenate([-sin, sin, -sin, sin], axis=-1), cos.T, sin.T


def _identity_rope_tables(n_tokens):
    half = DIFF_QK_DIM // 2
    return (jnp.ones((n_tokens, 2 * DIFF_QK_DIM), F32), jnp.zeros((n_tokens, 2 * DIFF_QK_DIM), F32),
            jnp.ones((half, n_tokens), F32), jnp.zeros((half, n_tokens), F32))


def _block_diag(w):
    n_g, cg, _ = w.shape
    eye = jnp.eye(n_g, dtype=w.dtype)
    return (eye[:, None, :, None] * w[:, :, None, :]).reshape(n_g * cg, n_g * cg)


def kernel(x, c, ctx, c_ctx, w_ada, b_ada, norm_g, ffn_w1, ffn_w2, w_in, w_out, na_rpb, pool_w, pool_scale,
           diff_lambda, diff_subln_g):
    nb, s, _ = x.shape
    n_ctx = ctx.shape[1]
    assert nb <= 7 and s % (GRID_W * NA_QROWS) == 0 and s % DIFF_TK == 0 and n_ctx % POOL_HALO == 0

    w1 = ffn_w1.astype(BF16)
    w2 = ffn_w2.astype(BF16)
    q_lo = 3 * NA_WIDTH + POOL_WIDTH
    k_lo = q_lo + DIFF_QK_WIDTH
    v_lo = k_lo + DIFF_QK_WIDTH
    w_in_b = w_in.astype(BF16)
    w_tok = jnp.concatenate([w_in_b[:, :, :q_lo], w_in_b[:, :, k_lo:v_lo]], axis=2)
    w_feat = jnp.swapaxes(jnp.concatenate([w_in_b[:, :, q_lo:k_lo], w_in_b[:, :, v_lo:]], axis=2), 1, 2)
    w_out_b = w_out.astype(BF16)

    ctx_row = nb
    cvec = jnp.zeros((8, D_MODEL), F32).at[:nb].set(c).at[ctx_row].set(c_ctx)
    mods = _ada(cvec, w_ada, b_ada).reshape(DEPTH, 8, N_MOD, D_MODEL)
    lat_row = lambda b: b
    ctx_mod_row = lambda b: ctx_row

    rope_lat = _rope_tables(s)
    rope_ctx = _identity_rope_tables(n_ctx)

    x_lat, x_ctx = x, ctx
    for layer in range(DEPTH):
        last = layer == DEPTH - 1
        lam_init = 0.8 - 0.6 * math.exp(-0.3 * layer)
        m = mods[layer]
        g = norm_g[layer]
        bias = _na_bias_tables(na_rpb[layer], s // GRID_W)
        pool_bd = _block_diag(pool_w[layer]).astype(BF16)
        pool_sc = pool_scale[layer].reshape(1, POOL_WIDTH)

        x_lat = _ffn(x_lat, m[:, 0:3], lat_row, g[0:2], w1[layer, 0], w2[layer, 0], 0.5)
        x_ctx = _ffn(x_ctx, m[:, 0:3], ctx_mod_row, g[0:2], w1[layer, 0], w2[layer, 0], 0.5)

        aq, ak, av, pin, dq, dk, dv = _inproj(x_lat, m[:, 3:6], lat_row, g[2:3], w_tok[layer], w_feat[layer],
                                              rope_lat)
        aqc, akc, avc, pinc, dqc, dkc, dvc = _inproj(x_ctx, m[:, 3:6], ctx_mod_row, g[2:3], w_tok[layer],
                                                     w_feat[layer], rope_ctx)
        a_lat = _na(aq, ak, av, akc, avc, bias)
        c_lat = _diff(dq, dk, dv, (dkc, dvc), diff_lambda[layer], diff_subln_g[layer], lam_init)
        x_lat_mixed = _outproj(x_lat, m[:, 3:6], lat_row, g[3:4], a_lat, pin, c_lat, pool_bd, pool_sc,
                               w_out_b[layer])
        x_lat = _ffn(x_lat_mixed, m[:, 6:9], lat_row, g[4:6], w1[layer, 1], w2[layer, 1], 0.5)

        if not last:
            a_ctx = _ctx_attn(aqc, akc, avc)
            c_ctx_out = _diff(dqc, dkc, dvc, None, diff_lambda[layer], diff_subln_g[layer], lam_init)
            x_ctx_mixed = _outproj(x_ctx, m[:, 3:6], ctx_mod_row, g[3:4], a_ctx, pinc, c_ctx_out, pool_bd, pool_sc,
                                   w_out_b[layer])
            x_ctx = _ffn(x_ctx_mixed, m[:, 6:9], ctx_mod_row, g[4:6], w1[layer, 1], w2[layer, 1], 0.5)
    return x_lat
```

```python
import functools
import math

import numpy as np
import jax
import jax.numpy as jnp
from jax import lax
from jax.experimental import pallas as pl
from jax.experimental.pallas import tpu as pltpu

D_MODEL = 1024
DEPTH = 2
GRID_W = 64
N_MOD = 9
D_FF = 2816
EPS = 1e-6
NEG_INF = -1e30
ROPE_THETA = 10000.0

NA_HEADS = 4
NA_HEAD_DIM = 64
NA_WIN_ROWS = 8
NA_WIN_COLS = 16
NA_WIDTH = NA_HEADS * NA_HEAD_DIM

POOL_WINDOWS = (2, 4, 8, 16)
POOL_GROUP_DIM = 64
POOL_WIDTH = len(POOL_WINDOWS) * POOL_GROUP_DIM
POOL_HALO = max(POOL_WINDOWS) // 2

DIFF_HEADS = 4
DIFF_QK_DIM = 64
DIFF_V_DIM = 2 * DIFF_QK_DIM
DIFF_QK_WIDTH = DIFF_HEADS * 2 * DIFF_QK_DIM
DIFF_WIDTH = DIFF_HEADS * DIFF_V_DIM
DIFF_Q_SCALE = DIFF_QK_DIM ** -0.5 * math.log2(math.e)

MIX_WIDTH = NA_WIDTH + POOL_WIDTH + DIFF_WIDTH
IN_WIDTH = 3 * NA_WIDTH + POOL_WIDTH + 2 * DIFF_QK_WIDTH + DIFF_WIDTH

BF16 = jnp.bfloat16
F32 = jnp.float32

V7X_VMEM_BYTES = 64 * 1024 * 1024
VMEM_LIMIT = V7X_VMEM_BYTES * 3 // 4

TOKEN_TILE = 512
FF_CHUNK = 256
ADA_COLS = 1152
NA_QROWS = 4
NA_KROWS = NA_QROWS + NA_WIN_ROWS
DIFF_TQ = 512
DIFF_TK = 1024

_NT = (((1,), (1,)), ((), ()))


def _params(*sem):
    return pltpu.CompilerParams(dimension_semantics=sem, vmem_limit_bytes=VMEM_LIMIT)


def _resident(shape, index_map):
    return pl.BlockSpec(shape, index_map, pipeline_mode=pl.Buffered(1))


def _rms(x, g):
    return x * lax.rsqrt(jnp.mean(x * x, axis=-1, keepdims=True) + EPS) * g


def _modulate(x, mod_ref, g):
    shift, scale = mod_ref[0, 0:1, :], mod_ref[0, 1:2, :]
    return _rms(x, g) * (1.0 + scale) + shift


def _ada_kernel(c_ref, w_ref, b_ref, o_ref):
    cv = c_ref[...]
    o_ref[0] = jnp.dot(cv * jax.nn.sigmoid(cv), w_ref[0], preferred_element_type=F32,
                       precision=lax.Precision.HIGHEST) + b_ref[0]


def _ada(cvec, w_ada, b_ada):
    n_out = N_MOD * D_MODEL
    return pl.pallas_call(
        _ada_kernel,
        out_shape=jax.ShapeDtypeStruct((DEPTH, 8, n_out), F32),
        grid=(DEPTH, n_out // ADA_COLS),
        in_specs=[pl.BlockSpec((8, D_MODEL), lambda l, j: (0, 0)),
                  pl.BlockSpec((1, D_MODEL, ADA_COLS), lambda l, j: (l, 0, j)),
                  pl.BlockSpec((1, 1, ADA_COLS), lambda l, j: (l, 0, j))],
        out_specs=pl.BlockSpec((1, 8, ADA_COLS), lambda l, j: (l, 0, j)),
        compiler_params=_params("parallel", "parallel"),
        name="ada_mod",
    )(cvec, w_ada, b_ada.reshape(DEPTH, 1, n_out))


def _ffn_kernel(x_ref, mod_ref, g_ref, w1_ref, w2_ref, o_ref, act_ref, *, res_w):
    x = x_ref[0]
    h = _modulate(x, mod_ref, g_ref[0:1, :]).astype(BF16)
    for c in range(D_FF // FF_CHUNK):
        lo = c * FF_CHUNK
        a = jnp.dot(h, w1_ref[:, lo:lo + FF_CHUNK], preferred_element_type=F32)
        b = jnp.dot(h, w1_ref[:, D_FF + lo:D_FF + lo + FF_CHUNK], preferred_element_type=F32)
        act_ref[:, lo:lo + FF_CHUNK] = (a * jax.nn.sigmoid(a) * b).astype(BF16)
    y = jnp.dot(act_ref[...], w2_ref[...], preferred_element_type=F32)
    o_ref[0] = x + res_w * mod_ref[0, 2:3, :] * _rms(y, g_ref[1:2, :])


def _mod_spec(layer, mod_row, sub):
    return pl.BlockSpec((None, 1, None, 3, D_MODEL), lambda b, i: (layer, mod_row(b), sub, 0, 0))


def _norm_spec(layer, first, count):
    return pl.BlockSpec((None, None, count, D_MODEL), lambda b, i: (layer, first // count, 0, 0))


def _ffn(x, mods, mod_row, norm_g, w1, w2, layer, which, res_w):
    nb, n, _ = x.shape
    tm = min(TOKEN_TILE, n)
    return pl.pallas_call(
        functools.partial(_ffn_kernel, res_w=res_w),
        out_shape=jax.ShapeDtypeStruct(x.shape, F32),
        grid=(nb, n // tm),
        in_specs=[pl.BlockSpec((1, tm, D_MODEL), lambda b, i: (b, i, 0)),
                  _mod_spec(layer, mod_row, 2 * which),
                  _norm_spec(layer, 4 * which, 2),
                  _resident((None, None, D_MODEL, 2 * D_FF), lambda b, i: (layer, which, 0, 0)),
                  _resident((None, None, D_FF, D_MODEL), lambda b, i: (layer, which, 0, 0))],
        out_specs=pl.BlockSpec((1, tm, D_MODEL), lambda b, i: (b, i, 0)),
        scratch_shapes=[pltpu.VMEM((tm, D_FF), BF16)],
        compiler_params=_params("parallel", "parallel"),
        name="ffn",
    )(x, mods, norm_g.reshape(DEPTH, 3, 2, D_MODEL), w1, w2)


def _inproj_kernel(x_ref, mod_ref, g_ref, w_ref, wt_ref, cos_ref, sin_ref, cos_t_ref, sin_t_ref,
                   naq_ref, nak_ref, nav_ref, pin_ref, dqt_ref, dk_ref, dvt_ref):
    x = x_ref[0]
    h = _modulate(x, mod_ref, g_ref[0:1, :]).astype(BF16)

    def proj(lo, width):
        return jnp.dot(h, w_ref[:, lo:lo + width], preferred_element_type=F32)

    naq_ref[0] = (proj(0, NA_WIDTH) * (NA_HEAD_DIM ** -0.5)).astype(BF16)
    nak_ref[0] = proj(NA_WIDTH, NA_WIDTH).astype(BF16)
    nav_ref[0] = proj(2 * NA_WIDTH, NA_WIDTH).astype(BF16)
    pin_ref[0] = proj(3 * NA_WIDTH, POOL_WIDTH)

    cos, sin = cos_ref[...], sin_ref[...]
    lane = lax.broadcasted_iota(jnp.int32, cos.shape, 1)
    half = DIFF_QK_DIM // 2
    first_half = (lane % DIFF_QK_DIM) < half
    head_w = 2 * DIFF_QK_DIM
    k = proj(3 * NA_WIDTH + POOL_WIDTH, DIFF_QK_WIDTH)
    for hd in range(DIFF_HEADS):
        t = k[:, hd * head_w:(hd + 1) * head_w]
        partner = jnp.where(first_half, pltpu.roll(t, head_w - half, 1), pltpu.roll(t, half, 1))
        dk_ref[0, :, hd * head_w:(hd + 1) * head_w] = (t * cos + partner * sin).astype(BF16)

    qv = lax.dot_general(wt_ref[...], h, _NT, preferred_element_type=F32)
    cos_t, sin_t = cos_t_ref[...], sin_t_ref[...]
    for st in range(2 * DIFF_HEADS):
        base = st * DIFF_QK_DIM
        x1 = qv[base:base + half] * DIFF_Q_SCALE
        x2 = qv[base + half:base + DIFF_QK_DIM] * DIFF_Q_SCALE
        dqt_ref[0, base:base + half, :] = (x1 * cos_t - x2 * sin_t).astype(BF16)
        dqt_ref[0, base + half:base + DIFF_QK_DIM, :] = (x1 * sin_t + x2 * cos_t).astype(BF16)
    dvt_ref[0] = qv[DIFF_QK_WIDTH:].astype(BF16)


def _inproj(x, mods, mod_row, norm_g, w, wt, layer, rope):
    nb, n, _ = x.shape
    tm = min(TOKEN_TILE, n)
    cos, sin, cos_t, sin_t = rope
    n_tok_major = w.shape[2]
    tok = lambda width: pl.BlockSpec((1, tm, width), lambda b, i: (b, i, 0))
    feat = lambda width: pl.BlockSpec((1, width, tm), lambda b, i: (b, 0, i))
    out = lambda width, dt: jax.ShapeDtypeStruct((nb, n, width), dt)
    out_t = lambda width, dt: jax.ShapeDtypeStruct((nb, width, n), dt)
    return pl.pallas_call(
        _inproj_kernel,
        out_shape=(out(NA_WIDTH, BF16), out(NA_WIDTH, BF16), out(NA_WIDTH, BF16), out(POOL_WIDTH, F32),
                   out_t(DIFF_QK_WIDTH, BF16), out(DIFF_QK_WIDTH, BF16), out_t(DIFF_WIDTH, BF16)),
        grid=(nb, n // tm),
        in_specs=[tok(D_MODEL),
                  _mod_spec(layer, mod_row, 1),
                  _norm_spec(layer, 2, 1),
                  _resident((None, D_MODEL, n_tok_major), lambda b, i: (layer, 0, 0)),
                  _resident((None, DIFF_QK_WIDTH + DIFF_WIDTH, D_MODEL), lambda b, i: (layer, 0, 0)),
                  pl.BlockSpec((tm, 2 * DIFF_QK_DIM), lambda b, i: (i, 0)),
                  pl.BlockSpec((tm, 2 * DIFF_QK_DIM), lambda b, i: (i, 0)),
                  pl.BlockSpec((DIFF_QK_DIM // 2, tm), lambda b, i: (0, i)),
                  pl.BlockSpec((DIFF_QK_DIM // 2, tm), lambda b, i: (0, i))],
        out_specs=(tok(NA_WIDTH), tok(NA_WIDTH), tok(NA_WIDTH), tok(POOL_WIDTH),
                   feat(DIFF_QK_WIDTH), tok(DIFF_QK_WIDTH), feat(DIFF_WIDTH)),
        compiler_params=_params("parallel", "parallel"),
        name="mixer_in_proj",
    )(x, mods, norm_g.reshape(DEPTH, 6, 1, D_MODEL), w, wt, cos, sin, cos_t, sin_t)


def _na_kernel(q_ref, k_ref, v_ref, kc_ref, vc_ref, bias_ref, o_ref, *, rows):
    i = pl.program_id(1)
    start = jnp.clip(i * NA_QROWS - NA_WIN_ROWS // 2, 0, rows - NA_KROWS) * GRID_W
    start = pl.multiple_of(start, GRID_W)
    n_keys = NA_KROWS * GRID_W
    q = q_ref[0]
    kb = k_ref[0, pl.ds(start, n_keys), :]
    vb = v_ref[0, pl.ds(start, n_keys), :]
    kc, vc = kc_ref[0], vc_ref[0]
    for hd in range(NA_HEADS):
        sl = slice(hd * NA_HEAD_DIM, (hd + 1) * NA_HEAD_DIM)
        qh = q[:, sl]
        s_loc = lax.dot_general(qh, kb[:, sl], _NT, preferred_element_type=F32) + bias_ref[0, hd]
        s_ctx = lax.dot_general(qh, kc[:, sl], _NT, preferred_element_type=F32)
        m = jnp.maximum(jnp.max(s_loc, axis=1, keepdims=True), jnp.max(s_ctx, axis=1, keepdims=True))
        p_loc = jnp.exp(s_loc - m)
        p_ctx = jnp.exp(s_ctx - m)
        denom = jnp.sum(p_loc, axis=1, keepdims=True) + jnp.sum(p_ctx, axis=1, keepdims=True)
        o = (jnp.dot(p_loc.astype(BF16), vb[:, sl], preferred_element_type=F32)
             + jnp.dot(p_ctx.astype(BF16), vc[:, sl], preferred_element_type=F32))
        o_ref[0, :, sl] = (o / denom).astype(BF16)


def _na(q, k, v, kc, vc, bias, layer):
    nb, s, _ = q.shape
    rows = s // GRID_W
    n_steps = rows // NA_QROWS
    tq = NA_QROWS * GRID_W
    n_ctx = kc.shape[1]

    def pattern(b, i):
        return (layer, jnp.where(i == 0, 0, jnp.where(i == n_steps - 1, 2, 1)), 0, 0, 0)

    return pl.pallas_call(
        functools.partial(_na_kernel, rows=rows),
        out_shape=jax.ShapeDtypeStruct((nb, s, NA_WIDTH), BF16),
        grid=(nb, n_steps),
        in_specs=[pl.BlockSpec((1, tq, NA_WIDTH), lambda b, i: (b, i, 0)),
                  _resident((1, s, NA_WIDTH), lambda b, i: (b, 0, 0)),
                  _resident((1, s, NA_WIDTH), lambda b, i: (b, 0, 0)),
                  pl.BlockSpec((1, n_ctx, NA_WIDTH), lambda b, i: (b, 0, 0)),
                  pl.BlockSpec((1, n_ctx, NA_WIDTH), lambda b, i: (b, 0, 0)),
                  pl.BlockSpec((None, 1, NA_HEADS, tq, NA_KROWS * GRID_W), pattern)],
        out_specs=pl.BlockSpec((1, tq, NA_WIDTH), lambda b, i: (b, i, 0)),
        compiler_params=_params("parallel", "arbitrary"),
        name="neighbourhood_attn",
    )(q, k, v, kc, vc, bias)


def _na_bias_tables(rpb, rows):
    n_l, n_h = rpb.shape[:2]
    pad = GRID_W - NA_WIN_COLS
    ext = jnp.pad(rpb.astype(F32), ((0, 0), (0, 0), (0, 0), (pad, pad)), mode="edge")
    toe = jnp.stack([ext[..., GRID_W - 1 - qc: 2 * GRID_W - 1 - qc] for qc in range(GRID_W)], axis=3)
    qc = np.arange(GRID_W)[:, None]
    kc = np.arange(GRID_W)[None, :]
    win_c0 = np.clip(qc - NA_WIN_COLS // 2, 0, GRID_W - NA_WIN_COLS)
    in_cols = (kc >= win_c0) & (kc < win_c0 + NA_WIN_COLS)
    toe = jnp.where(jnp.asarray(in_cols), toe, NEG_INF)
    toe = jnp.pad(toe, ((0, 0), (0, 0), (NA_KROWS, NA_KROWS), (0, 0), (0, 0)), constant_values=NEG_INF)
    valid = np.zeros((3, NA_QROWS, NA_KROWS), bool)
    tables = []
    for pat, r0 in enumerate((0, NA_QROWS, rows - NA_QROWS)):
        start = int(np.clip(r0 - NA_WIN_ROWS // 2, 0, rows - NA_KROWS))
        per_q = []
        for qr in range(NA_QROWS):
            r = r0 + qr
            ws = int(np.clip(r - NA_WIN_ROWS // 2, 0, rows - NA_WIN_ROWS))
            d0 = start - r + NA_WIN_ROWS - 1 + NA_KROWS
            per_q.append(toe[:, :, d0:d0 + NA_KROWS])
            valid[pat, qr] = [ws <= start + kl < ws + NA_WIN_ROWS for kl in range(NA_KROWS)]
        tables.append(jnp.stack(per_q, axis=2))
    t = jnp.stack(tables, axis=1)
    t = jnp.where(jnp.asarray(valid)[None, :, None, :, :, None, None], t, NEG_INF)
    t = jnp.transpose(t, (0, 1, 2, 3, 5, 4, 6))
    return t.reshape(n_l, 3, n_h, NA_QROWS * GRID_W, NA_KROWS * GRID_W)


def _diff_kernel(*refs, has_ctx, lam_init, n_kv, tk):
    if has_ctx:
        qt_ref, k_ref, vt_ref, kc_ref, vct_ref, lam_ref, g_ref, o_ref = refs[:8]
    else:
        qt_ref, k_ref, vt_ref, lam_ref, g_ref, o_ref = refs[:6]
    qpad_ref, m_ref, l_ref, acc_ref, s_head_ref, p_tail_ref, alpha_tail_ref = refs[-7:]
    head_w = 2 * DIFF_QK_DIM
    n_stream = 2 * DIFF_HEADS
    last = n_stream - 1


    def scores(idx, kh):
        return jnp.dot(kh, qpad_ref[idx], preferred_element_type=F32)

    def softmax_step(idx, s):
        m_old = m_ref[idx:idx + 1, :]
        m_new = jnp.maximum(m_old, jnp.max(s, axis=0, keepdims=True))
        alpha = jnp.exp2(m_old - m_new)
        p = jnp.exp2(s - m_new)
        l_ref[idx:idx + 1, :] = alpha * l_ref[idx:idx + 1, :] + jnp.sum(p, axis=0, keepdims=True)
        m_ref[idx:idx + 1, :] = m_new
        return alpha, p.astype(BF16)

    def accumulate(idx, alpha, p, vth):
        acc_ref[idx] = alpha * acc_ref[idx] + jnp.dot(vth, p, preferred_element_type=F32)

    def single_block(kblk, vtblk):
        k_head = lambda idx: kblk[:, (idx // 2) * head_w:(idx // 2 + 1) * head_w]
        v_head = lambda idx: vtblk[(idx // 2) * DIFF_V_DIM:(idx // 2 + 1) * DIFF_V_DIM, :]
        s_next = scores(0, k_head(0))
        pending = None
        for idx in range(n_stream):
            s = s_next
            if idx < last:
                s_next = scores(idx + 1, k_head(idx + 1))
            if pending is not None:
                accumulate(*pending)
            alpha, p = softmax_step(idx, s)
            pending = (idx, alpha, p, v_head(idx))
        accumulate(*pending)

    def k_head_at(j, idx):
        return k_ref[0, pl.ds(pl.multiple_of(j * tk, tk), tk), (idx // 2) * head_w:(idx // 2 + 1) * head_w]

    def v_head_at(j, idx):
        return vt_ref[0, (idx // 2) * DIFF_V_DIM:(idx // 2 + 1) * DIFF_V_DIM, pl.ds(pl.multiple_of(j * tk, tk), tk)]

    def pipelined_block(j, carry):
        s = s_head_ref[...]
        pending = (last, alpha_tail_ref[...], p_tail_ref[...], v_head_at(jnp.maximum(j - 1, 0), last))
        for idx in range(n_stream):
            if idx < last:
                s_next = scores(idx + 1, k_head_at(j, idx + 1))
            else:
                s_next = scores(0, k_head_at(jnp.minimum(j + 1, n_kv - 1), 0))
            accumulate(*pending)
            alpha, p = softmax_step(idx, s)
            pending = (idx, alpha, p, v_head_at(j, idx))
            s = s_next
        s_head_ref[...] = s
        alpha_tail_ref[...] = pending[1]
        p_tail_ref[...] = pending[2]
        return carry

    row = lax.broadcasted_iota(jnp.int32, (head_w, qt_ref.shape[2]), 0)
    for hd in range(DIFF_HEADS):
        qh = qt_ref[0, hd * head_w:(hd + 1) * head_w, :]
        qpad_ref[2 * hd] = jnp.where(row < DIFF_QK_DIM, qh, jnp.zeros_like(qh))
        qpad_ref[2 * hd + 1] = jnp.where(row >= DIFF_QK_DIM, qh, jnp.zeros_like(qh))
    m_ref[...] = jnp.full(m_ref.shape, -jnp.inf, F32)
    l_ref[...] = jnp.zeros(l_ref.shape, F32)
    acc_ref[...] = jnp.zeros(acc_ref.shape, F32)
    if has_ctx:
        single_block(kc_ref[0], vct_ref[0])

    if n_kv == 1:
        single_block(k_ref[0], vt_ref[0])
    else:
        s_head_ref[...] = scores(0, k_head_at(0, 0))
        alpha_tail_ref[...] = jnp.ones(alpha_tail_ref.shape, F32)
        p_tail_ref[...] = jnp.zeros(p_tail_ref.shape, BF16)
        lax.fori_loop(0, n_kv, pipelined_block, 0)
        accumulate(last, alpha_tail_ref[...], p_tail_ref[...], v_head_at(n_kv - 1, last))

    lf = lam_ref[...]
    lam = (jnp.exp(jnp.sum(lf[0:1] * lf[1:2], axis=1, keepdims=True))
           - jnp.exp(jnp.sum(lf[2:3] * lf[3:4], axis=1, keepdims=True)) + lam_init)
    for hd in range(DIFF_HEADS):
        o1 = acc_ref[2 * hd] / l_ref[2 * hd:2 * hd + 1, :]
        o2 = acc_ref[2 * hd + 1] / l_ref[2 * hd + 1:2 * hd + 2, :]
        o = o1 - lam * o2
        o = o * lax.rsqrt(jnp.mean(o * o, axis=0, keepdims=True) + EPS) * g_ref[...] * (1.0 - lam_init)
        o_ref[0, :, hd * DIFF_V_DIM:(hd + 1) * DIFF_V_DIM] = o.T.astype(BF16)


def _diff(qt, k, vt, ctx_kv, lam_vec, subln_g, layer, lam_init):
    nb, _, n = qt.shape
    n_keys = k.shape[1]
    tq = min(DIFF_TQ, n)
    tk = min(DIFF_TK, n_keys)
    n_kv = n_keys // tk
    has_ctx = ctx_kv is not None
    in_specs = [pl.BlockSpec((1, DIFF_QK_WIDTH, tq), lambda b, i: (b, 0, i)),
                _resident((1, n_keys, DIFF_QK_WIDTH), lambda b, i: (b, 0, 0)),
                _resident((1, DIFF_WIDTH, n_keys), lambda b, i: (b, 0, 0))]
    args = [qt, k, vt]
    if has_ctx:
        n_ctx = ctx_kv[0].shape[1]
        in_specs += [pl.BlockSpec((1, n_ctx, DIFF_QK_WIDTH), lambda b, i: (b, 0, 0)),
                     pl.BlockSpec((1, DIFF_WIDTH, n_ctx), lambda b, i: (b, 0, 0))]
        args += list(ctx_kv)
    in_specs += [pl.BlockSpec((None, 4, DIFF_QK_DIM), lambda b, i: (layer, 0, 0)),
                 pl.BlockSpec((None, DIFF_V_DIM, 1), lambda b, i: (layer, 0, 0))]
    args += [lam_vec, subln_g.reshape(DEPTH, DIFF_V_DIM, 1)]
    n_stream = 2 * DIFF_HEADS
    return pl.pallas_call(
        functools.partial(_diff_kernel, has_ctx=has_ctx, lam_init=lam_init, n_kv=n_kv, tk=tk),
        out_shape=jax.ShapeDtypeStruct((nb, n, DIFF_WIDTH), BF16),
        grid=(nb, n // tq),
        in_specs=in_specs,
        out_specs=pl.BlockSpec((1, tq, DIFF_WIDTH), lambda b, i: (b, i, 0)),
        scratch_shapes=[pltpu.VMEM((n_stream, 2 * DIFF_QK_DIM, tq), BF16),
                        pltpu.VMEM((n_stream, tq), F32),
                        pltpu.VMEM((n_stream, tq), F32),
                        pltpu.VMEM((n_stream, DIFF_V_DIM, tq), F32),
                        pltpu.VMEM((tk, tq), F32),
                        pltpu.VMEM((tk, tq), BF16),
                        pltpu.VMEM((1, tq), F32)],
        compiler_params=_params("parallel", "arbitrary"),
        name="diff_attn",
    )(*args)


def _ctx_attn_kernel(q_ref, k_ref, v_ref, o_ref):
    q, k, v = q_ref[0], k_ref[0], v_ref[0]
    for hd in range(NA_HEADS):
        sl = slice(hd * NA_HEAD_DIM, (hd + 1) * NA_HEAD_DIM)
        s = lax.dot_general(q[:, sl], k[:, sl], _NT, preferred_element_type=F32)
        p = jnp.exp(s - jnp.max(s, axis=1, keepdims=True))
        o = jnp.dot(p.astype(BF16), v[:, sl], preferred_element_type=F32)
        o_ref[0, :, sl] = (o / jnp.sum(p, axis=1, keepdims=True)).astype(BF16)


def _ctx_attn(q, k, v):
    nb, n, w = q.shape
    spec = pl.BlockSpec((1, n, w), lambda b: (b, 0, 0))
    return pl.pallas_call(
        _ctx_attn_kernel,
        out_shape=jax.ShapeDtypeStruct(q.shape, BF16),
        grid=(nb,),
        in_specs=[spec, spec, spec],
        out_specs=spec,
        compiler_params=_params("parallel"),
        name="ctx_dense_attn",
    )(q, k, v)


def _outproj_kernel(x_ref, mod_ref, g_ref, a_ref, pprev_ref, pcur_ref, pnext_ref, c_ref,
                    pw_ref, ps_ref, w_ref, o_ref, *, tm, seq_len):
    i = pl.program_id(1)
    n = tm + 2 * POOL_HALO
    u = jnp.concatenate([pprev_ref[0], pcur_ref[0], pnext_ref[0]], axis=0)
    t_ext = i * tm - POOL_HALO + lax.broadcasted_iota(jnp.int32, (n, 1), 0)
    u = jnp.where((t_ext >= 0) & (t_ext < seq_len), u, 0.0)
    s2 = u + pltpu.roll(u, 1, 0)
    s4 = pltpu.roll(s2, 1, 0) + pltpu.roll(s2, n - 1, 0)
    s8 = pltpu.roll(s4, 2, 0) + pltpu.roll(s4, n - 2, 0)
    s16 = pltpu.roll(s8, 4, 0) + pltpu.roll(s8, n - 4, 0)
    cur = slice(POOL_HALO, POOL_HALO + tm)
    group = lax.broadcasted_iota(jnp.int32, (1, POOL_WIDTH), 1) // POOL_GROUP_DIM
    wsum = jnp.where(group == 0, s2[cur], jnp.where(group == 1, s4[cur], jnp.where(group == 2, s8[cur], s16[cur])))
    half = jnp.where(group == 0, POOL_WINDOWS[0] // 2,
                     jnp.where(group == 1, POOL_WINDOWS[1] // 2,
                               jnp.where(group == 2, POOL_WINDOWS[2] // 2, POOL_WINDOWS[3] // 2)))
    t = i * tm + lax.broadcasted_iota(jnp.int32, (tm, 1), 0)
    count = (jnp.clip(t + half, 0, seq_len) - jnp.clip(t - half, 0, seq_len)).astype(F32)
    pooled = (wsum / count - u[cur]).astype(BF16)
    b = jnp.dot(pooled, pw_ref[...], preferred_element_type=F32) * ps_ref[...]
    cat = jnp.concatenate([a_ref[0], b.astype(BF16), c_ref[0]], axis=1)
    y = jnp.dot(cat, w_ref[...], preferred_element_type=F32)
    o_ref[0] = x_ref[0] + mod_ref[0, 2:3, :] * _rms(y, g_ref[...])


def _outproj(x, mods, mod_row, norm_g, a, pin, c, pool_bd, pool_scale, w, layer):
    nb, n, _ = x.shape
    tm = min(TOKEN_TILE, n)
    halo_per_tile = tm // POOL_HALO
    n_halo = n // POOL_HALO
    tok = lambda width: pl.BlockSpec((1, tm, width), lambda b, i: (b, i, 0))
    return pl.pallas_call(
        functools.partial(_outproj_kernel, tm=tm, seq_len=n),
        out_shape=jax.ShapeDtypeStruct(x.shape, F32),
        grid=(nb, n // tm),
        in_specs=[tok(D_MODEL),
                  _mod_spec(layer, mod_row, 1),
                  _norm_spec(layer, 3, 1),
                  tok(NA_WIDTH),
                  pl.BlockSpec((1, POOL_HALO, POOL_WIDTH),
                               lambda b, i: (b, jnp.maximum(i * halo_per_tile - 1, 0), 0)),
                  tok(POOL_WIDTH),
                  pl.BlockSpec((1, POOL_HALO, POOL_WIDTH),
                               lambda b, i: (b, jnp.minimum((i + 1) * halo_per_tile, n_halo - 1), 0)),
                  tok(DIFF_WIDTH),
                  pl.BlockSpec((None, POOL_WIDTH, POOL_WIDTH), lambda b, i: (layer, 0, 0)),
                  pl.BlockSpec((None, 1, POOL_WIDTH), lambda b, i: (layer, 0, 0)),
                  _resident((None, MIX_WIDTH, D_MODEL), lambda b, i: (layer, 0, 0))],
        out_specs=tok(D_MODEL),
        compiler_params=_params("parallel", "parallel"),
        name="mixer_out_proj",
    )(x, mods, norm_g.reshape(DEPTH, 6, 1, D_MODEL), a, pin, pin, pin, c, pool_bd, pool_scale, w)


def _rope_tables(n_tokens):
    t = jnp.arange(n_tokens, dtype=jnp.int32)
    row = (t // GRID_W).astype(F32)
    col = (t % GRID_W).astype(F32)
    n_freq = DIFF_QK_DIM // 4
    inv_freq = jnp.power(ROPE_THETA, -jnp.arange(n_freq, dtype=F32) / n_freq)
    ang = jnp.concatenate([row[:, None] * inv_freq, col[:, None] * inv_freq], axis=-1)
    cos, sin = jnp.cos(ang), jnp.sin(ang)
    return jnp.tile(cos, (1, 4)), jnp.concatenate([-sin, sin, -sin, sin], axis=-1), cos.T, sin.T


def _identity_rope_tables(n_tokens):
    half = DIFF_QK_DIM // 2
    return (jnp.ones((n_tokens, 2 * DIFF_QK_DIM), F32), jnp.zeros((n_tokens, 2 * DIFF_QK_DIM), F32),
            jnp.ones((half, n_tokens), F32), jnp.zeros((half, n_tokens), F32))


def _block_diag(w):
    n_l, n_g, cg, _ = w.shape
    eye = jnp.eye(n_g, dtype=w.dtype)
    return (eye[None, :, None, :, None] * w[:, :, :, None, :]).reshape(n_l, n_g * cg, n_g * cg)


def kernel(x, c, ctx, c_ctx, w_ada, b_ada, norm_g, ffn_w1, ffn_w2, w_in, w_out, na_rpb, pool_w, pool_scale,
           diff_lambda, diff_subln_g):
    nb, s, _ = x.shape
    n_ctx = ctx.shape[1]
    assert nb <= 7 and s % (GRID_W * NA_QROWS) == 0 and s % DIFF_TK == 0 and n_ctx % POOL_HALO == 0

    w1 = ffn_w1.astype(BF16)
    w2 = ffn_w2.astype(BF16)
    q_lo = 3 * NA_WIDTH + POOL_WIDTH
    k_lo = q_lo + DIFF_QK_WIDTH
    v_lo = k_lo + DIFF_QK_WIDTH
    w_in_b = w_in.astype(BF16)
    w_tok = jnp.concatenate([w_in_b[:, :, :q_lo], w_in_b[:, :, k_lo:v_lo]], axis=2)
    w_feat = jnp.swapaxes(jnp.concatenate([w_in_b[:, :, q_lo:k_lo], w_in_b[:, :, v_lo:]], axis=2), 1, 2)
    w_out_b = w_out.astype(BF16)

    ctx_row = nb
    cvec = jnp.zeros((8, D_MODEL), F32).at[:nb].set(c).at[ctx_row].set(c_ctx)
    mods = _ada(cvec, w_ada, b_ada).reshape(DEPTH, 8, N_MOD // 3, 3, D_MODEL)
    lat_row = lambda b: b
    ctx_mod_row = lambda b: ctx_row

    rope_lat = _rope_tables(s)
    rope_ctx = _identity_rope_tables(n_ctx)
    bias = _na_bias_tables(na_rpb, s // GRID_W)
    pool_bd = _block_diag(pool_w).astype(BF16)
    pool_sc = pool_scale.reshape(DEPTH, 1, POOL_WIDTH)

    x_lat, x_ctx = x, ctx
    for layer in range(DEPTH):
        last = layer == DEPTH - 1
        lam_init = 0.8 - 0.6 * math.exp(-0.3 * layer)

        x_lat = _ffn(x_lat, mods, lat_row, norm_g, w1, w2, layer, 0, 0.5)
        x_ctx = _ffn(x_ctx, mods, ctx_mod_row, norm_g, w1, w2, layer, 0, 0.5)

        aq, ak, av, pin, dq, dk, dv = _inproj(x_lat, mods, lat_row, norm_g, w_tok, w_feat, layer, rope_lat)
        aqc, akc, avc, pinc, dqc, dkc, dvc = _inproj(x_ctx, mods, ctx_mod_row, norm_g, w_tok, w_feat, layer, rope_ctx)
        a_lat = _na(aq, ak, av, akc, avc, bias, layer)
        c_lat = _diff(dq, dk, dv, (dkc, dvc), diff_lambda, diff_subln_g, layer, lam_init)
        x_lat = _outproj(x_lat, mods, lat_row, norm_g, a_lat, pin, c_lat, pool_bd, pool_sc, w_out_b, layer)
        x_lat = _ffn(x_lat, mods, lat_row, norm_g, w1, w2, layer, 1, 0.5)

        if not last:
            a_ctx = _ctx_attn(aqc, akc, avc)
            c_ctx_out = _diff(dqc, dkc, dvc, None, diff_lambda, diff_subln_g, layer, lam_init)
            x_ctx = _outproj(x_ctx, mods, ctx_mod_row, norm_g, a_ctx, pinc, c_ctx_out, pool_bd, pool_sc, w_out_b, layer)
            x_ctx = _ffn(x_ctx, mods, ctx_mod_row, norm_g, w1, w2, layer, 1, 0.5)
    return x_lat
```

```python
import functools
import math

import numpy as np
import jax
import jax.numpy as jnp
from jax import lax
from jax.experimental import pallas as pl
from jax.experimental.pallas import tpu as pltpu

D_MODEL = 1024
DEPTH = 2
GRID_W = 64
N_MOD = 9
D_FF = 2816
EPS = 1e-6
NEG_INF = -1e30
ROPE_THETA = 10000.0

NA_HEADS = 4
NA_HEAD_DIM = 64
NA_WIN_ROWS = 8
NA_WIN_COLS = 16
NA_WIDTH = NA_HEADS * NA_HEAD_DIM

POOL_WINDOWS = (2, 4, 8, 16)
POOL_GROUP_DIM = 64
POOL_WIDTH = len(POOL_WINDOWS) * POOL_GROUP_DIM
POOL_HALO = max(POOL_WINDOWS) // 2

DIFF_HEADS = 4
DIFF_QK_DIM = 64
DIFF_V_DIM = 2 * DIFF_QK_DIM
DIFF_QK_WIDTH = DIFF_HEADS * 2 * DIFF_QK_DIM
DIFF_WIDTH = DIFF_HEADS * DIFF_V_DIM
DIFF_Q_SCALE = DIFF_QK_DIM ** -0.5 * math.log2(math.e)

MIX_WIDTH = NA_WIDTH + POOL_WIDTH + DIFF_WIDTH
IN_WIDTH = 3 * NA_WIDTH + POOL_WIDTH + 2 * DIFF_QK_WIDTH + DIFF_WIDTH

BF16 = jnp.bfloat16
F32 = jnp.float32

V7X_VMEM_BYTES = 64 * 1024 * 1024
VMEM_LIMIT = V7X_VMEM_BYTES * 3 // 4

TOKEN_TILE = 512
FFN_TILE = 1024
FF_CHUNK = 256
ADA_COLS = 1152
NA_QROWS = 4
NA_KROWS = NA_QROWS + NA_WIN_ROWS
DIFF_TQ = 512
DIFF_TK = 1024
DIFF_ACC_ROWS = DIFF_V_DIM + 16

_NT = (((1,), (1,)), ((), ()))


def _params(*sem):
    return pltpu.CompilerParams(dimension_semantics=sem, vmem_limit_bytes=VMEM_LIMIT)


def _resident(shape, index_map):
    return pl.BlockSpec(shape, index_map, pipeline_mode=pl.Buffered(1))


def _rms(x, g):
    return x * lax.rsqrt(jnp.mean(x * x, axis=-1, keepdims=True) + EPS) * g


def _modulate(x, mod_ref, g):
    shift, scale = mod_ref[0, 0:1, :], mod_ref[0, 1:2, :]
    return _rms(x, g) * (1.0 + scale) + shift


def _ada_kernel(c_ref, w_ref, b_ref, o_ref):
    cv = c_ref[...]
    o_ref[0] = jnp.dot(cv * jax.nn.sigmoid(cv), w_ref[0], preferred_element_type=F32,
                       precision=lax.Precision.HIGHEST) + b_ref[0]


def _ada(cvec, w_ada, b_ada):
    n_out = N_MOD * D_MODEL
    return pl.pallas_call(
        _ada_kernel,
        out_shape=jax.ShapeDtypeStruct((DEPTH, 8, n_out), F32),
        grid=(DEPTH, n_out // ADA_COLS),
        in_specs=[pl.BlockSpec((8, D_MODEL), lambda l, j: (0, 0)),
                  pl.BlockSpec((1, D_MODEL, ADA_COLS), lambda l, j: (l, 0, j)),
                  pl.BlockSpec((1, 1, ADA_COLS), lambda l, j: (l, 0, j))],
        out_specs=pl.BlockSpec((1, 8, ADA_COLS), lambda l, j: (l, 0, j)),
        compiler_params=_params("parallel", "parallel"),
        name="ada_mod",
    )(cvec, w_ada, b_ada.reshape(DEPTH, 1, n_out))


def _ffn_kernel(x_ref, mod_ref, g_ref, w1_ref, w2_ref, o_ref, act_ref, *, res_w):
    x = x_ref[0]
    h = _modulate(x, mod_ref, g_ref[0:1, :]).astype(BF16)
    for c in range(D_FF // FF_CHUNK):
        lo = c * FF_CHUNK
        a = jnp.dot(h, w1_ref[:, lo:lo + FF_CHUNK], preferred_element_type=F32)
        b = jnp.dot(h, w1_ref[:, D_FF + lo:D_FF + lo + FF_CHUNK], preferred_element_type=F32)
        act_ref[:, lo:lo + FF_CHUNK] = (a * jax.nn.sigmoid(a) * b).astype(BF16)
    y = jnp.dot(act_ref[...], w2_ref[...], preferred_element_type=F32)
    o_ref[0] = x + res_w * mod_ref[0, 2:3, :] * _rms(y, g_ref[1:2, :])


def _mod_spec(layer, mod_row, sub):
    return pl.BlockSpec((None, 1, None, 3, D_MODEL), lambda b, i: (layer, mod_row(b), sub, 0, 0))


def _norm_spec(layer, first, count):
    return pl.BlockSpec((None, None, count, D_MODEL), lambda b, i: (layer, first // count, 0, 0))


def _ffn(x, mods, mod_row, norm_g, w1, w2, layer, which, res_w):
    nb, n, _ = x.shape
    tm = min(FFN_TILE, n)
    return pl.pallas_call(
        functools.partial(_ffn_kernel, res_w=res_w),
        out_shape=jax.ShapeDtypeStruct(x.shape, F32),
        grid=(nb, n // tm),
        in_specs=[pl.BlockSpec((1, tm, D_MODEL), lambda b, i: (b, i, 0)),
                  _mod_spec(layer, mod_row, 2 * which),
                  _norm_spec(layer, 4 * which, 2),
                  _resident((None, None, D_MODEL, 2 * D_FF), lambda b, i: (layer, which, 0, 0)),
                  _resident((None, None, D_FF, D_MODEL), lambda b, i: (layer, which, 0, 0))],
        out_specs=pl.BlockSpec((1, tm, D_MODEL), lambda b, i: (b, i, 0)),
        scratch_shapes=[pltpu.VMEM((tm, D_FF), BF16)],
        compiler_params=_params("parallel", "parallel"),
        name="ffn",
    )(x, mods, norm_g.reshape(DEPTH, 3, 2, D_MODEL), w1, w2)


def _inproj_kernel(x_ref, mod_ref, g_ref, w_ref, wt_ref, cos_ref, sin_ref, cos_t_ref, sin_t_ref,
                   naq_ref, nak_ref, nav_ref, pin_ref, dqt_ref, dk_ref, dvt_ref):
    x = x_ref[0]
    h = _modulate(x, mod_ref, g_ref[0:1, :]).astype(BF16)

    def proj(lo, width):
        return jnp.dot(h, w_ref[:, lo:lo + width], preferred_element_type=F32)

    naq_ref[0] = (proj(0, NA_WIDTH) * (NA_HEAD_DIM ** -0.5)).astype(BF16)
    nak_ref[0] = proj(NA_WIDTH, NA_WIDTH).astype(BF16)
    nav_ref[0] = proj(2 * NA_WIDTH, NA_WIDTH).astype(BF16)
    pin_ref[0] = proj(3 * NA_WIDTH, POOL_WIDTH)

    cos, sin = cos_ref[...], sin_ref[...]
    lane = lax.broadcasted_iota(jnp.int32, cos.shape, 1)
    half = DIFF_QK_DIM // 2
    first_half = (lane % DIFF_QK_DIM) < half
    head_w = 2 * DIFF_QK_DIM
    k = proj(3 * NA_WIDTH + POOL_WIDTH + DIFF_QK_WIDTH, DIFF_QK_WIDTH)
    for hd in range(DIFF_HEADS):
        t = k[:, hd * head_w:(hd + 1) * head_w]
        partner = jnp.where(first_half, pltpu.roll(t, head_w - half, 1), pltpu.roll(t, half, 1))
        dk_ref[0, :, hd * head_w:(hd + 1) * head_w] = (t * cos + partner * sin).astype(BF16)

    qv = lax.dot_general(wt_ref[...], h, _NT, preferred_element_type=F32)
    cos_t, sin_t = cos_t_ref[...], sin_t_ref[...]
    for st in range(2 * DIFF_HEADS):
        base = st * DIFF_QK_DIM
        x1 = qv[base:base + half] * DIFF_Q_SCALE
        x2 = qv[base + half:base + DIFF_QK_DIM] * DIFF_Q_SCALE
        dqt_ref[0, base:base + half, :] = (x1 * cos_t - x2 * sin_t).astype(BF16)
        dqt_ref[0, base + half:base + DIFF_QK_DIM, :] = (x1 * sin_t + x2 * cos_t).astype(BF16)
    dvt_ref[0] = qv[DIFF_QK_WIDTH:].astype(BF16)


def _inproj(x, mods, mod_row, norm_g, w, wt, layer, rope):
    nb, n, _ = x.shape
    tm = min(TOKEN_TILE, n)
    cos, sin, cos_t, sin_t = rope
    n_tok_major = w.shape[2]
    tok = lambda width: pl.BlockSpec((1, tm, width), lambda b, i: (b, i, 0))
    feat = lambda width: pl.BlockSpec((1, width, tm), lambda b, i: (b, 0, i))
    out = lambda width, dt: jax.ShapeDtypeStruct((nb, n, width), dt)
    out_t = lambda width, dt: jax.ShapeDtypeStruct((nb, width, n), dt)
    return pl.pallas_call(
        _inproj_kernel,
        out_shape=(out(NA_WIDTH, BF16), out(NA_WIDTH, BF16), out(NA_WIDTH, BF16), out(POOL_WIDTH, F32),
                   out_t(DIFF_QK_WIDTH, BF16), out(DIFF_QK_WIDTH, BF16), out_t(DIFF_WIDTH, BF16)),
        grid=(nb, n // tm),
        in_specs=[tok(D_MODEL),
                  _mod_spec(layer, mod_row, 1),
                  _norm_spec(layer, 2, 1),
                  _resident((None, D_MODEL, n_tok_major), lambda b, i: (layer, 0, 0)),
                  _resident((None, DIFF_QK_WIDTH + DIFF_WIDTH, D_MODEL), lambda b, i: (layer, 0, 0)),
                  pl.BlockSpec((tm, 2 * DIFF_QK_DIM), lambda b, i: (i, 0)),
                  pl.BlockSpec((tm, 2 * DIFF_QK_DIM), lambda b, i: (i, 0)),
                  pl.BlockSpec((DIFF_QK_DIM // 2, tm), lambda b, i: (0, i)),
                  pl.BlockSpec((DIFF_QK_DIM // 2, tm), lambda b, i: (0, i))],
        out_specs=(tok(NA_WIDTH), tok(NA_WIDTH), tok(NA_WIDTH), tok(POOL_WIDTH),
                   feat(DIFF_QK_WIDTH), tok(DIFF_QK_WIDTH), feat(DIFF_WIDTH)),
        compiler_params=_params("parallel", "parallel"),
        name="mixer_in_proj",
    )(x, mods, norm_g.reshape(DEPTH, 6, 1, D_MODEL), w, wt, cos, sin, cos_t, sin_t)


def _na_kernel(q_ref, k_ref, v_ref, kc_ref, vc_ref, bias_ref, o_ref, *, rows):
    i = pl.program_id(1)
    start = jnp.clip(i * NA_QROWS - NA_WIN_ROWS // 2, 0, rows - NA_KROWS) * GRID_W
    start = pl.multiple_of(start, GRID_W)
    n_keys = NA_KROWS * GRID_W
    q = q_ref[0]
    kb = k_ref[0, pl.ds(start, n_keys), :]
    vb = v_ref[0, pl.ds(start, n_keys), :]
    kc, vc = kc_ref[0], vc_ref[0]
    head = lambda hd: slice(hd * NA_HEAD_DIM, (hd + 1) * NA_HEAD_DIM)

    def scores(hd):
        qh = q[:, head(hd)]
        return (lax.dot_general(qh, kb[:, head(hd)], _NT, preferred_element_type=F32) + bias_ref[0, hd],
                lax.dot_general(qh, kc[:, head(hd)], _NT, preferred_element_type=F32))

    def finish(hd, p_loc, p_ctx, denom):
        o = (jnp.dot(p_loc, vb[:, head(hd)], preferred_element_type=F32)
             + jnp.dot(p_ctx, vc[:, head(hd)], preferred_element_type=F32))
        o_ref[0, :, head(hd)] = (o / denom).astype(BF16)

    s_next = scores(0)
    pending = None
    for hd in range(NA_HEADS):
        s_loc, s_ctx = s_next
        if hd + 1 < NA_HEADS:
            s_next = scores(hd + 1)
        if pending is not None:
            finish(*pending)
        m = jnp.maximum(jnp.max(s_loc, axis=1, keepdims=True), jnp.max(s_ctx, axis=1, keepdims=True))
        p_loc = jnp.exp(s_loc - m)
        p_ctx = jnp.exp(s_ctx - m)
        denom = jnp.sum(p_loc, axis=1, keepdims=True) + jnp.sum(p_ctx, axis=1, keepdims=True)
        pending = (hd, p_loc.astype(BF16), p_ctx.astype(BF16), denom)
    finish(*pending)


def _na(q, k, v, kc, vc, bias, layer):
    nb, s, _ = q.shape
    rows = s // GRID_W
    n_steps = rows // NA_QROWS
    tq = NA_QROWS * GRID_W
    n_ctx = kc.shape[1]

    def pattern(b, i):
        return (layer, jnp.where(i == 0, 0, jnp.where(i == n_steps - 1, 2, 1)), 0, 0, 0)

    return pl.pallas_call(
        functools.partial(_na_kernel, rows=rows),
        out_shape=jax.ShapeDtypeStruct((nb, s, NA_WIDTH), BF16),
        grid=(nb, n_steps),
        in_specs=[pl.BlockSpec((1, tq, NA_WIDTH), lambda b, i: (b, i, 0)),
                  _resident((1, s, NA_WIDTH), lambda b, i: (b, 0, 0)),
                  _resident((1, s, NA_WIDTH), lambda b, i: (b, 0, 0)),
                  pl.BlockSpec((1, n_ctx, NA_WIDTH), lambda b, i: (b, 0, 0)),
                  pl.BlockSpec((1, n_ctx, NA_WIDTH), lambda b, i: (b, 0, 0)),
                  pl.BlockSpec((None, 1, NA_HEADS, tq, NA_KROWS * GRID_W), pattern)],
        out_specs=pl.BlockSpec((1, tq, NA_WIDTH), lambda b, i: (b, i, 0)),
        compiler_params=_params("parallel", "arbitrary"),
        name="neighbourhood_attn",
    )(q, k, v, kc, vc, bias)


def _na_bias_tables(rpb, rows):
    n_l, n_h, n_dr, _ = rpb.shape
    pad = GRID_W - NA_WIN_COLS
    w = 2 * GRID_W
    ext = jnp.pad(rpb.astype(F32), ((0, 0), (0, 0), (0, 0), (pad, pad + 1)), mode="edge")
    skew = jnp.broadcast_to(ext[..., None, :], (n_l, n_h, n_dr, GRID_W, w)).reshape(n_l, n_h, n_dr, GRID_W * w)
    skew = skew[..., :GRID_W * (w - 1)].reshape(n_l, n_h, n_dr, GRID_W, w - 1)
    toe = skew[..., GRID_W - 1:]
    qc = np.arange(GRID_W)[:, None]
    kc = np.arange(GRID_W)[None, :]
    win_c0 = np.clip(qc - NA_WIN_COLS // 2, 0, GRID_W - NA_WIN_COLS)
    in_cols = (kc >= win_c0) & (kc < win_c0 + NA_WIN_COLS)
    toe = jnp.where(jnp.asarray(in_cols), toe, NEG_INF)
    toe = jnp.transpose(toe, (0, 1, 3, 2, 4))
    toe = jnp.pad(toe, ((0, 0), (0, 0), (0, 0), (NA_KROWS, NA_KROWS), (0, 0)), constant_values=NEG_INF)
    tables = []
    for r0 in (0, NA_QROWS, rows - NA_QROWS):
        start = int(np.clip(r0 - NA_WIN_ROWS // 2, 0, rows - NA_KROWS))
        per_q = []
        for qr in range(NA_QROWS):
            r = r0 + qr
            ws = int(np.clip(r - NA_WIN_ROWS // 2, 0, rows - NA_WIN_ROWS))
            d0 = start - r + NA_WIN_ROWS - 1 + NA_KROWS
            in_rows = np.array([ws <= start + kl < ws + NA_WIN_ROWS for kl in range(NA_KROWS)])[:, None]
            tile = jnp.where(jnp.asarray(in_rows), toe[:, :, :, d0:d0 + NA_KROWS], NEG_INF)
            per_q.append(tile.reshape(n_l, n_h, GRID_W, NA_KROWS * GRID_W))
        tables.append(jnp.concatenate(per_q, axis=2))
    return jnp.stack(tables, axis=1)


def _diff_kernel(*refs, has_ctx, lam_init, n_kv, tk):
    if has_ctx:
        qt_ref, k_ref, vt_ref, kc_ref, vct_ref, lam_ref, g_ref, o_ref = refs[:8]
    else:
        qt_ref, k_ref, vt_ref, lam_ref, g_ref, o_ref = refs[:6]
    qpad_ref, m_ref, acc_ref, s_head_ref, p_tail_ref, alpha_tail_ref = refs[-6:]
    head_w = 2 * DIFF_QK_DIM
    n_stream = 2 * DIFF_HEADS
    last = n_stream - 1


    def scores(idx, kh):
        return jnp.dot(kh, qpad_ref[idx], preferred_element_type=F32)

    def softmax_step(idx, s):
        m_old = m_ref[idx:idx + 1, :]
        m_new = jnp.maximum(m_old, jnp.max(s, axis=0, keepdims=True))
        alpha = jnp.exp2(m_old - m_new)
        m_ref[idx:idx + 1, :] = m_new
        return alpha, jnp.exp2(s - m_new).astype(BF16)

    def accumulate(idx, alpha, p, vth):
        v_ext = jnp.concatenate([vth, jnp.ones((DIFF_ACC_ROWS - DIFF_V_DIM, vth.shape[1]), BF16)], axis=0)
        acc_ref[idx] = alpha * acc_ref[idx] + jnp.dot(v_ext, p, preferred_element_type=F32)

    def single_block(kblk, vtblk):
        k_head = lambda idx: kblk[:, (idx // 2) * head_w:(idx // 2 + 1) * head_w]
        v_head = lambda idx: vtblk[(idx // 2) * DIFF_V_DIM:(idx // 2 + 1) * DIFF_V_DIM, :]
        s_next = scores(0, k_head(0))
        pending = None
        for idx in range(n_stream):
            s = s_next
            if idx < last:
                s_next = scores(idx + 1, k_head(idx + 1))
            if pending is not None:
                accumulate(*pending)
            alpha, p = softmax_step(idx, s)
            pending = (idx, alpha, p, v_head(idx))
        accumulate(*pending)

    def k_head_at(j, idx):
        return k_ref[0, pl.ds(pl.multiple_of(j * tk, tk), tk), (idx // 2) * head_w:(idx // 2 + 1) * head_w]

    def v_head_at(j, idx):
        return vt_ref[0, (idx // 2) * DIFF_V_DIM:(idx // 2 + 1) * DIFF_V_DIM, pl.ds(pl.multiple_of(j * tk, tk), tk)]

    def pipelined_block(j, carry):
        s = s_head_ref[...]
        pending = (last, alpha_tail_ref[...], p_tail_ref[...], v_head_at(jnp.maximum(j - 1, 0), last))
        for idx in range(n_stream):
            if idx < last:
                s_next = scores(idx + 1, k_head_at(j, idx + 1))
            else:
                s_next = scores(0, k_head_at(jnp.minimum(j + 1, n_kv - 1), 0))
            accumulate(*pending)
            alpha, p = softmax_step(idx, s)
            pending = (idx, alpha, p, v_head_at(j, idx))
            s = s_next
        s_head_ref[...] = s
        alpha_tail_ref[...] = pending[1]
        p_tail_ref[...] = pending[2]
        return carry

    row = lax.broadcasted_iota(jnp.int32, (head_w, qt_ref.shape[2]), 0)
    for hd in range(DIFF_HEADS):
        qh = qt_ref[0, hd * head_w:(hd + 1) * head_w, :]
        qpad_ref[2 * hd] = jnp.where(row < DIFF_QK_DIM, qh, jnp.zeros_like(qh))
        qpad_ref[2 * hd + 1] = jnp.where(row >= DIFF_QK_DIM, qh, jnp.zeros_like(qh))
    m_ref[...] = jnp.full(m_ref.shape, -jnp.inf, F32)
    acc_ref[...] = jnp.zeros(acc_ref.shape, F32)
    if has_ctx:
        single_block(kc_ref[0], vct_ref[0])

    if n_kv == 1:
        single_block(k_ref[0], vt_ref[0])
    else:
        s_head_ref[...] = scores(0, k_head_at(0, 0))
        alpha_tail_ref[...] = jnp.ones(alpha_tail_ref.shape, F32)
        p_tail_ref[...] = jnp.zeros(p_tail_ref.shape, BF16)
        lax.fori_loop(0, n_kv, pipelined_block, 0)
        accumulate(last, alpha_tail_ref[...], p_tail_ref[...], v_head_at(n_kv - 1, last))

    lf = lam_ref[...]
    lam = (jnp.exp(jnp.sum(lf[0:1] * lf[1:2], axis=1, keepdims=True))
           - jnp.exp(jnp.sum(lf[2:3] * lf[3:4], axis=1, keepdims=True)) + lam_init)
    for hd in range(DIFF_HEADS):
        o1 = acc_ref[2 * hd, 0:DIFF_V_DIM, :] / acc_ref[2 * hd, DIFF_V_DIM:DIFF_V_DIM + 1, :]
        o2 = acc_ref[2 * hd + 1, 0:DIFF_V_DIM, :] / acc_ref[2 * hd + 1, DIFF_V_DIM:DIFF_V_DIM + 1, :]
        o = o1 - lam * o2
        o = o * lax.rsqrt(jnp.mean(o * o, axis=0, keepdims=True) + EPS) * g_ref[...] * (1.0 - lam_init)
        o_ref[0, :, hd * DIFF_V_DIM:(hd + 1) * DIFF_V_DIM] = o.T.astype(BF16)


def _diff(qt, k, vt, ctx_kv, lam_vec, subln_g, layer, lam_init):
    nb, _, n = qt.shape
    n_keys = k.shape[1]
    tq = min(DIFF_TQ, n)
    tk = min(DIFF_TK, n_keys)
    n_kv = n_keys // tk
    has_ctx = ctx_kv is not None
    in_specs = [pl.BlockSpec((1, DIFF_QK_WIDTH, tq), lambda b, i: (b, 0, i)),
                _resident((1, n_keys, DIFF_QK_WIDTH), lambda b, i: (b, 0, 0)),
                _resident((1, DIFF_WIDTH, n_keys), lambda b, i: (b, 0, 0))]
    args = [qt, k, vt]
    if has_ctx:
        n_ctx = ctx_kv[0].shape[1]
        in_specs += [pl.BlockSpec((1, n_ctx, DIFF_QK_WIDTH), lambda b, i: (b, 0, 0)),
                     pl.BlockSpec((1, DIFF_WIDTH, n_ctx), lambda b, i: (b, 0, 0))]
        args += list(ctx_kv)
    in_specs += [pl.BlockSpec((None, 4, DIFF_QK_DIM), lambda b, i: (layer, 0, 0)),
                 pl.BlockSpec((None, DIFF_V_DIM, 1), lambda b, i: (layer, 0, 0))]
    args += [lam_vec, subln_g.reshape(DEPTH, DIFF_V_DIM, 1)]
    n_stream = 2 * DIFF_HEADS
    return pl.pallas_call(
        functools.partial(_diff_kernel, has_ctx=has_ctx, lam_init=lam_init, n_kv=n_kv, tk=tk),
        out_shape=jax.ShapeDtypeStruct((nb, n, DIFF_WIDTH), BF16),
        grid=(nb, n // tq),
        in_specs=in_specs,
        out_specs=pl.BlockSpec((1, tq, DIFF_WIDTH), lambda b, i: (b, i, 0)),
        scratch_shapes=[pltpu.VMEM((n_stream, 2 * DIFF_QK_DIM, tq), BF16),
                        pltpu.VMEM((n_stream, tq), F32),
                        pltpu.VMEM((n_stream, DIFF_ACC_ROWS, tq), F32),
                        pltpu.VMEM((tk, tq), F32),
                        pltpu.VMEM((tk, tq), BF16),
                        pltpu.VMEM((1, tq), F32)],
        compiler_params=_params("parallel", "arbitrary"),
        name="diff_attn",
    )(*args)


def _ctx_attn_kernel(q_ref, k_ref, v_ref, o_ref):
    q, k, v = q_ref[0], k_ref[0], v_ref[0]
    for hd in range(NA_HEADS):
        sl = slice(hd * NA_HEAD_DIM, (hd + 1) * NA_HEAD_DIM)
        s = lax.dot_general(q[:, sl], k[:, sl], _NT, preferred_element_type=F32)
        p = jnp.exp(s - jnp.max(s, axis=1, keepdims=True))
        o = jnp.dot(p.astype(BF16), v[:, sl], preferred_element_type=F32)
        o_ref[0, :, sl] = (o / jnp.sum(p, axis=1, keepdims=True)).astype(BF16)


def _ctx_attn(q, k, v):
    nb, n, w = q.shape
    spec = pl.BlockSpec((1, n, w), lambda b: (b, 0, 0))
    return pl.pallas_call(
        _ctx_attn_kernel,
        out_shape=jax.ShapeDtypeStruct(q.shape, BF16),
        grid=(nb,),
        in_specs=[spec, spec, spec],
        out_specs=spec,
        compiler_params=_params("parallel"),
        name="ctx_dense_attn",
    )(q, k, v)


def _outproj_kernel(x_ref, mod_ref, g_ref, a_ref, pprev_ref, pcur_ref, pnext_ref, c_ref,
                    pw_ref, ps_ref, w_ref, o_ref, *, tm, seq_len):
    i = pl.program_id(1)
    n = tm + 2 * POOL_HALO
    u = jnp.concatenate([pprev_ref[0], pcur_ref[0], pnext_ref[0]], axis=0)
    t_ext = i * tm - POOL_HALO + lax.broadcasted_iota(jnp.int32, (n, 1), 0)
    u = jnp.where((t_ext >= 0) & (t_ext < seq_len), u, 0.0)
    s2 = u + pltpu.roll(u, 1, 0)
    s4 = pltpu.roll(s2, 1, 0) + pltpu.roll(s2, n - 1, 0)
    s8 = pltpu.roll(s4, 2, 0) + pltpu.roll(s4, n - 2, 0)
    s16 = pltpu.roll(s8, 4, 0) + pltpu.roll(s8, n - 4, 0)
    cur = slice(POOL_HALO, POOL_HALO + tm)
    group = lax.broadcasted_iota(jnp.int32, (1, POOL_WIDTH), 1) // POOL_GROUP_DIM
    wsum = jnp.where(group == 0, s2[cur], jnp.where(group == 1, s4[cur], jnp.where(group == 2, s8[cur], s16[cur])))
    half = jnp.where(group == 0, POOL_WINDOWS[0] // 2,
                     jnp.where(group == 1, POOL_WINDOWS[1] // 2,
                               jnp.where(group == 2, POOL_WINDOWS[2] // 2, POOL_WINDOWS[3] // 2)))
    t = i * tm + lax.broadcasted_iota(jnp.int32, (tm, 1), 0)
    count = (jnp.clip(t + half, 0, seq_len) - jnp.clip(t - half, 0, seq_len)).astype(F32)
    pooled = (wsum / count - u[cur]).astype(BF16)
    b = jnp.dot(pooled, pw_ref[...], preferred_element_type=F32) * ps_ref[...]
    cat = jnp.concatenate([a_ref[0], b.astype(BF16), c_ref[0]], axis=1)
    y = jnp.dot(cat, w_ref[...], preferred_element_type=F32)
    o_ref[0] = x_ref[0] + mod_ref[0, 2:3, :] * _rms(y, g_ref[...])


def _outproj(x, mods, mod_row, norm_g, a, pin, c, pool_bd, pool_scale, w, layer):
    nb, n, _ = x.shape
    tm = min(TOKEN_TILE, n)
    halo_per_tile = tm // POOL_HALO
    n_halo = n // POOL_HALO
    tok = lambda width: pl.BlockSpec((1, tm, width), lambda b, i: (b, i, 0))
    return pl.pallas_call(
        functools.partial(_outproj_kernel, tm=tm, seq_len=n),
        out_shape=jax.ShapeDtypeStruct(x.shape, F32),
        grid=(nb, n // tm),
        in_specs=[tok(D_MODEL),
                  _mod_spec(layer, mod_row, 1),
                  _norm_spec(layer, 3, 1),
                  tok(NA_WIDTH),
                  pl.BlockSpec((1, POOL_HALO, POOL_WIDTH),
                               lambda b, i: (b, jnp.maximum(i * halo_per_tile - 1, 0), 0)),
                  tok(POOL_WIDTH),
                  pl.BlockSpec((1, POOL_HALO, POOL_WIDTH),
                               lambda b, i: (b, jnp.minimum((i + 1) * halo_per_tile, n_halo - 1), 0)),
                  tok(DIFF_WIDTH),
                  pl.BlockSpec((None, POOL_WIDTH, POOL_WIDTH), lambda b, i: (layer, 0, 0)),
                  pl.BlockSpec((None, 1, POOL_WIDTH), lambda b, i: (layer, 0, 0)),
                  _resident((None, MIX_WIDTH, D_MODEL), lambda b, i: (layer, 0, 0))],
        out_specs=tok(D_MODEL),
        compiler_params=_params("parallel", "parallel"),
        name="mixer_out_proj",
    )(x, mods, norm_g.reshape(DEPTH, 6, 1, D_MODEL), a, pin, pin, pin, c, pool_bd, pool_scale, w)


def _rope_tables(n_tokens):
    t = jnp.arange(n_tokens, dtype=jnp.int32)
    row = (t // GRID_W).astype(F32)
    col = (t % GRID_W).astype(F32)
    n_freq = DIFF_QK_DIM // 4
    inv_freq = jnp.power(ROPE_THETA, -jnp.arange(n_freq, dtype=F32) / n_freq)
    ang = jnp.concatenate([row[:, None] * inv_freq, col[:, None] * inv_freq], axis=-1)
    cos, sin = jnp.cos(ang), jnp.sin(ang)
    return jnp.tile(cos, (1, 4)), jnp.concatenate([-sin, sin, -sin, sin], axis=-1), cos.T, sin.T


def _identity_rope_tables(n_tokens):
    half = DIFF_QK_DIM // 2
    return (jnp.ones((n_tokens, 2 * DIFF_QK_DIM), F32), jnp.zeros((n_tokens, 2 * DIFF_QK_DIM), F32),
            jnp.ones((half, n_tokens), F32), jnp.zeros((half, n_tokens), F32))


def _block_diag(w):
    n_l, n_g, cg, _ = w.shape
    eye = jnp.eye(n_g, dtype=w.dtype)
    return (eye[None, :, None, :, None] * w[:, :, :, None, :]).reshape(n_l, n_g * cg, n_g * cg)


def kernel(x, c, ctx, c_ctx, w_ada, b_ada, norm_g, ffn_w1, ffn_w2, w_in, w_out, na_rpb, pool_w, pool_scale,
           diff_lambda, diff_subln_g):
    nb, s, _ = x.shape
    n_ctx = ctx.shape[1]
    assert nb <= 7 and s % (GRID_W * NA_QROWS) == 0 and s % DIFF_TK == 0 and n_ctx % POOL_HALO == 0

    w1 = ffn_w1.astype(BF16)
    w2 = ffn_w2.astype(BF16)
    q_lo = 3 * NA_WIDTH + POOL_WIDTH
    k_lo = q_lo + DIFF_QK_WIDTH
    v_lo = k_lo + DIFF_QK_WIDTH
    w_in_b = w_in.astype(BF16)
    w_tok = w_in_b
    w_feat = jnp.swapaxes(jnp.concatenate([w_in_b[:, :, q_lo:k_lo], w_in_b[:, :, v_lo:]], axis=2), 1, 2)
    w_out_b = w_out.astype(BF16)

    ctx_row = nb
    cvec = jnp.zeros((8, D_MODEL), F32).at[:nb].set(c).at[ctx_row].set(c_ctx)
    mods = _ada(cvec, w_ada, b_ada).reshape(DEPTH, 8, N_MOD // 3, 3, D_MODEL)
    lat_row = lambda b: b
    ctx_mod_row = lambda b: ctx_row

    rope_lat = _rope_tables(s)
    rope_ctx = _identity_rope_tables(n_ctx)
    bias = _na_bias_tables(na_rpb, s // GRID_W)
    pool_bd = _block_diag(pool_w).astype(BF16)
    pool_sc = pool_scale.reshape(DEPTH, 1, POOL_WIDTH)

    x_lat, x_ctx = x, ctx
    for layer in range(DEPTH):
        last = layer == DEPTH - 1
        lam_init = 0.8 - 0.6 * math.exp(-0.3 * layer)

        x_lat = _ffn(x_lat, mods, lat_row, norm_g, w1, w2, layer, 0, 0.5)
        x_ctx = _ffn(x_ctx, mods, ctx_mod_row, norm_g, w1, w2, layer, 0, 0.5)

        aq, ak, av, pin, dq, dk, dv = _inproj(x_lat, mods, lat_row, norm_g, w_tok, w_feat, layer, rope_lat)
        aqc, akc, avc, pinc, dqc, dkc, dvc = _inproj(x_ctx, mods, ctx_mod_row, norm_g, w_tok, w_feat, layer, rope_ctx)
        a_lat = _na(aq, ak, av, akc, avc, bias, layer)
        c_lat = _diff(dq, dk, dv, (dkc, dvc), diff_lambda, diff_subln_g, layer, lam_init)
        x_lat = _outproj(x_lat, mods, lat_row, norm_g, a_lat, pin, c_lat, pool_bd, pool_sc, w_out_b, layer)
        x_lat = _ffn(x_lat, mods, lat_row, norm_g, w1, w2, layer, 1, 0.5)

        if not last:
            a_ctx = _ctx_attn(aqc, akc, avc)
            c_ctx_out = _diff(dqc, dkc, dvc, None, diff_lambda, diff_subln_g, layer, lam_init)
            x_ctx = _outproj(x_ctx, mods, ctx_mod_row, norm_g, a_ctx, pinc, c_ctx_out, pool_bd, pool_sc, w_out_b, layer)
            x_ctx = _ffn(x_ctx, mods, ctx_mod_row, norm_g, w1, w2, layer, 1, 0.5)
    return x_lat
```

```python
import functools
import math

import numpy as np
import jax
import jax.numpy as jnp
from jax import lax
from jax.experimental import pallas as pl
from jax.experimental.pallas import tpu as pltpu

D_MODEL = 1024
DEPTH = 2
GRID_W = 64
N_MOD = 9
D_FF = 2816
EPS = 1e-6
NEG_INF = -1e30
ROPE_THETA = 10000.0

NA_HEADS = 4
NA_HEAD_DIM = 64
NA_WIN_ROWS = 8
NA_WIN_COLS = 16
NA_WIDTH = NA_HEADS * NA_HEAD_DIM

POOL_WINDOWS = (2, 4, 8, 16)
POOL_GROUP_DIM = 64
POOL_WIDTH = len(POOL_WINDOWS) * POOL_GROUP_DIM
POOL_HALO = max(POOL_WINDOWS) // 2

DIFF_HEADS = 4
DIFF_QK_DIM = 64
DIFF_V_DIM = 2 * DIFF_QK_DIM
DIFF_QK_WIDTH = DIFF_HEADS * 2 * DIFF_QK_DIM
DIFF_WIDTH = DIFF_HEADS * DIFF_V_DIM
DIFF_Q_SCALE = DIFF_QK_DIM ** -0.5 * math.log2(math.e)

MIX_WIDTH = NA_WIDTH + POOL_WIDTH + DIFF_WIDTH
IN_WIDTH = 3 * NA_WIDTH + POOL_WIDTH + 2 * DIFF_QK_WIDTH + DIFF_WIDTH

BF16 = jnp.bfloat16
F32 = jnp.float32

V7X_VMEM_BYTES = 64 * 1024 * 1024
VMEM_LIMIT = V7X_VMEM_BYTES * 3 // 4

TOKEN_TILE = 512
FFN_TILE = 1024
FF_CHUNK = 256
ADA_COLS = 1152
NA_QROWS = 4
NA_KROWS = NA_QROWS + NA_WIN_ROWS
DIFF_TQ = 512
DIFF_TK = 1024
DIFF_ACC_ROWS = DIFF_V_DIM + 16

_NT = (((1,), (1,)), ((), ()))


def _params(*sem):
    return pltpu.CompilerParams(dimension_semantics=sem, vmem_limit_bytes=VMEM_LIMIT)


def _resident(shape, index_map):
    return pl.BlockSpec(shape, index_map, pipeline_mode=pl.Buffered(1))


def _rms(x, g):
    return x * lax.rsqrt(jnp.mean(x * x, axis=-1, keepdims=True) + EPS) * g


def _modulate(x, mod_ref, g):
    shift, scale = mod_ref[0, 0:1, :], mod_ref[0, 1:2, :]
    return _rms(x, g) * (1.0 + scale) + shift


def _ada_kernel(c_ref, w_ref, b_ref, o_ref):
    cv = c_ref[...]
    o_ref[0] = jnp.dot(cv * jax.nn.sigmoid(cv), w_ref[0], preferred_element_type=F32,
                       precision=lax.Precision.HIGHEST) + b_ref[0]


def _ada(cvec, w_ada, b_ada):
    n_out = N_MOD * D_MODEL
    return pl.pallas_call(
        _ada_kernel,
        out_shape=jax.ShapeDtypeStruct((DEPTH, 8, n_out), F32),
        grid=(DEPTH, n_out // ADA_COLS),
        in_specs=[pl.BlockSpec((8, D_MODEL), lambda l, j: (0, 0)),
                  pl.BlockSpec((1, D_MODEL, ADA_COLS), lambda l, j: (l, 0, j)),
                  pl.BlockSpec((1, 1, ADA_COLS), lambda l, j: (l, 0, j))],
        out_specs=pl.BlockSpec((1, 8, ADA_COLS), lambda l, j: (l, 0, j)),
        compiler_params=_params("parallel", "parallel"),
        name="ada_mod",
    )(cvec, w_ada, b_ada.reshape(DEPTH, 1, n_out))


def _swiglu_half_step(x, mod_ref, g_ref, w1_ref, w2_ref, act_ref, res_w):
    h = _modulate(x, mod_ref, g_ref[0:1, :]).astype(BF16)
    for c in range(D_FF // FF_CHUNK):
        lo = c * FF_CHUNK
        a = jnp.dot(h, w1_ref[:, lo:lo + FF_CHUNK], preferred_element_type=F32)
        b = jnp.dot(h, w1_ref[:, D_FF + lo:D_FF + lo + FF_CHUNK], preferred_element_type=F32)
        act_ref[:, lo:lo + FF_CHUNK] = (a * jax.nn.sigmoid(a) * b).astype(BF16)
    y = jnp.dot(act_ref[...], w2_ref[...], preferred_element_type=F32)
    return x + res_w * mod_ref[0, 2:3, :] * _rms(y, g_ref[1:2, :])


def _ffn_kernel(x_ref, mod_ref, g_ref, w1_ref, w2_ref, o_ref, act_ref, *, res_w):
    o_ref[0] = _swiglu_half_step(x_ref[0], mod_ref, g_ref, w1_ref, w2_ref, act_ref, res_w)


def _mod_spec(layer, mod_row, sub):
    return pl.BlockSpec((None, 1, None, 3, D_MODEL), lambda b, i: (layer, mod_row(b), sub, 0, 0))


def _norm_spec(layer, first, count):
    return pl.BlockSpec((None, None, count, D_MODEL), lambda b, i: (layer, first // count, 0, 0))


def _ffn(x, mods, mod_row, norm_g, w1, w2, layer, which, res_w):
    nb, n, _ = x.shape
    tm = min(FFN_TILE, n)
    return pl.pallas_call(
        functools.partial(_ffn_kernel, res_w=res_w),
        out_shape=jax.ShapeDtypeStruct(x.shape, F32),
        grid=(nb, n // tm),
        in_specs=[pl.BlockSpec((1, tm, D_MODEL), lambda b, i: (b, i, 0)),
                  _mod_spec(layer, mod_row, 2 * which),
                  _norm_spec(layer, 4 * which, 2),
                  _resident((None, None, D_MODEL, 2 * D_FF), lambda b, i: (layer, which, 0, 0)),
                  _resident((None, None, D_FF, D_MODEL), lambda b, i: (layer, which, 0, 0))],
        out_specs=pl.BlockSpec((1, tm, D_MODEL), lambda b, i: (b, i, 0)),
        scratch_shapes=[pltpu.VMEM((tm, D_FF), BF16)],
        compiler_params=_params("parallel", "parallel"),
        name="ffn",
    )(x, mods, norm_g.reshape(DEPTH, 3, 2, D_MODEL), w1, w2)


def _inproj_kernel(x_ref, mod_ref, g_ref, w_ref, wt_ref, cos_ref, sin_ref, cos_t_ref, sin_t_ref,
                   naq_ref, nak_ref, nav_ref, pin_ref, dqt_ref, dk_ref, dvt_ref):
    x = x_ref[0]
    h = _modulate(x, mod_ref, g_ref[0:1, :]).astype(BF16)

    def proj(lo, width):
        return jnp.dot(h, w_ref[:, lo:lo + width], preferred_element_type=F32)

    naq_ref[0] = (proj(0, NA_WIDTH) * (NA_HEAD_DIM ** -0.5)).astype(BF16)
    nak_ref[0] = proj(NA_WIDTH, NA_WIDTH).astype(BF16)
    nav_ref[0] = proj(2 * NA_WIDTH, NA_WIDTH).astype(BF16)
    pin_ref[0] = proj(3 * NA_WIDTH, POOL_WIDTH)

    cos, sin = cos_ref[...], sin_ref[...]
    lane = lax.broadcasted_iota(jnp.int32, cos.shape, 1)
    half = DIFF_QK_DIM // 2
    first_half = (lane % DIFF_QK_DIM) < half
    head_w = 2 * DIFF_QK_DIM
    k = proj(3 * NA_WIDTH + POOL_WIDTH + DIFF_QK_WIDTH, DIFF_QK_WIDTH)
    for hd in range(DIFF_HEADS):
        t = k[:, hd * head_w:(hd + 1) * head_w]
        partner = jnp.where(first_half, pltpu.roll(t, head_w - half, 1), pltpu.roll(t, half, 1))
        dk_ref[0, :, hd * head_w:(hd + 1) * head_w] = (t * cos + partner * sin).astype(BF16)

    qv = lax.dot_general(wt_ref[...], h, _NT, preferred_element_type=F32)
    cos_t, sin_t = cos_t_ref[...], sin_t_ref[...]
    for st in range(2 * DIFF_HEADS):
        base = st * DIFF_QK_DIM
        x1 = qv[base:base + half] * DIFF_Q_SCALE
        x2 = qv[base + half:base + DIFF_QK_DIM] * DIFF_Q_SCALE
        dqt_ref[0, base:base + half, :] = (x1 * cos_t - x2 * sin_t).astype(BF16)
        dqt_ref[0, base + half:base + DIFF_QK_DIM, :] = (x1 * sin_t + x2 * cos_t).astype(BF16)
    dvt_ref[0] = qv[DIFF_QK_WIDTH:].astype(BF16)


def _inproj(x, mods, mod_row, norm_g, w, wt, layer, rope):
    nb, n, _ = x.shape
    tm = min(TOKEN_TILE, n)
    cos, sin, cos_t, sin_t = rope
    n_tok_major = w.shape[2]
    tok = lambda width: pl.BlockSpec((1, tm, width), lambda b, i: (b, i, 0))
    feat = lambda width: pl.BlockSpec((1, width, tm), lambda b, i: (b, 0, i))
    out = lambda width, dt: jax.ShapeDtypeStruct((nb, n, width), dt)
    out_t = lambda width, dt: jax.ShapeDtypeStruct((nb, width, n), dt)
    return pl.pallas_call(
        _inproj_kernel,
        out_shape=(out(NA_WIDTH, BF16), out(NA_WIDTH, BF16), out(NA_WIDTH, BF16), out(POOL_WIDTH, F32),
                   out_t(DIFF_QK_WIDTH, BF16), out(DIFF_QK_WIDTH, BF16), out_t(DIFF_WIDTH, BF16)),
        grid=(nb, n // tm),
        in_specs=[tok(D_MODEL),
                  _mod_spec(layer, mod_row, 1),
                  _norm_spec(layer, 2, 1),
                  _resident((None, D_MODEL, n_tok_major), lambda b, i: (layer, 0, 0)),
                  _resident((None, DIFF_QK_WIDTH + DIFF_WIDTH, D_MODEL), lambda b, i: (layer, 0, 0)),
                  pl.BlockSpec((tm, 2 * DIFF_QK_DIM), lambda b, i: (i, 0)),
                  pl.BlockSpec((tm, 2 * DIFF_QK_DIM), lambda b, i: (i, 0)),
                  pl.BlockSpec((DIFF_QK_DIM // 2, tm), lambda b, i: (0, i)),
                  pl.BlockSpec((DIFF_QK_DIM // 2, tm), lambda b, i: (0, i))],
        out_specs=(tok(NA_WIDTH), tok(NA_WIDTH), tok(NA_WIDTH), tok(POOL_WIDTH),
                   feat(DIFF_QK_WIDTH), tok(DIFF_QK_WIDTH), feat(DIFF_WIDTH)),
        compiler_params=_params("parallel", "parallel"),
        name="mixer_in_proj",
    )(x, mods, norm_g.reshape(DEPTH, 6, 1, D_MODEL), w, wt, cos, sin, cos_t, sin_t)


def _na_kernel(q_ref, k_ref, v_ref, kc_ref, vc_ref, bias_ref, o_ref, *, rows):
    i = pl.program_id(1)
    start = jnp.clip(i * NA_QROWS - NA_WIN_ROWS // 2, 0, rows - NA_KROWS) * GRID_W
    start = pl.multiple_of(start, GRID_W)
    n_keys = NA_KROWS * GRID_W
    q = q_ref[0]
    kb = k_ref[0, pl.ds(start, n_keys), :]
    vb = v_ref[0, pl.ds(start, n_keys), :]
    kc, vc = kc_ref[0], vc_ref[0]
    head = lambda hd: slice(hd * NA_HEAD_DIM, (hd + 1) * NA_HEAD_DIM)

    def scores(hd):
        qh = q[:, head(hd)]
        return (lax.dot_general(qh, kb[:, head(hd)], _NT, preferred_element_type=F32) + bias_ref[0, hd],
                lax.dot_general(qh, kc[:, head(hd)], _NT, preferred_element_type=F32))

    def finish(hd, p_loc, p_ctx, denom):
        o = (jnp.dot(p_loc, vb[:, head(hd)], preferred_element_type=F32)
             + jnp.dot(p_ctx, vc[:, head(hd)], preferred_element_type=F32))
        o_ref[0, :, head(hd)] = (o / denom).astype(BF16)

    s_next = scores(0)
    pending = None
    for hd in range(NA_HEADS):
        s_loc, s_ctx = s_next
        if hd + 1 < NA_HEADS:
            s_next = scores(hd + 1)
        if pending is not None:
            finish(*pending)
        m = jnp.maximum(jnp.max(s_loc, axis=1, keepdims=True), jnp.max(s_ctx, axis=1, keepdims=True))
        p_loc = jnp.exp(s_loc - m)
        p_ctx = jnp.exp(s_ctx - m)
        denom = jnp.sum(p_loc, axis=1, keepdims=True) + jnp.sum(p_ctx, axis=1, keepdims=True)
        pending = (hd, p_loc.astype(BF16), p_ctx.astype(BF16), denom)
    finish(*pending)


def _na(q, k, v, kc, vc, bias, layer):
    nb, s, _ = q.shape
    rows = s // GRID_W
    n_steps = rows // NA_QROWS
    tq = NA_QROWS * GRID_W
    n_ctx = kc.shape[1]

    def pattern(b, i):
        return (layer, jnp.where(i == 0, 0, jnp.where(i == n_steps - 1, 2, 1)), 0, 0, 0)

    return pl.pallas_call(
        functools.partial(_na_kernel, rows=rows),
        out_shape=jax.ShapeDtypeStruct((nb, s, NA_WIDTH), BF16),
        grid=(nb, n_steps),
        in_specs=[pl.BlockSpec((1, tq, NA_WIDTH), lambda b, i: (b, i, 0)),
                  _resident((1, s, NA_WIDTH), lambda b, i: (b, 0, 0)),
                  _resident((1, s, NA_WIDTH), lambda b, i: (b, 0, 0)),
                  pl.BlockSpec((1, n_ctx, NA_WIDTH), lambda b, i: (b, 0, 0)),
                  pl.BlockSpec((1, n_ctx, NA_WIDTH), lambda b, i: (b, 0, 0)),
                  pl.BlockSpec((None, 1, NA_HEADS, tq, NA_KROWS * GRID_W), pattern)],
        out_specs=pl.BlockSpec((1, tq, NA_WIDTH), lambda b, i: (b, i, 0)),
        compiler_params=_params("parallel", "arbitrary"),
        name="neighbourhood_attn",
    )(q, k, v, kc, vc, bias)


def _na_bias_tables(rpb, rows):
    n_l, n_h, n_dr, _ = rpb.shape
    pad = GRID_W - NA_WIN_COLS
    w = 2 * GRID_W
    ext = jnp.pad(rpb.astype(F32), ((0, 0), (0, 0), (0, 0), (pad, pad + 1)), mode="edge")
    skew = jnp.broadcast_to(ext[..., None, :], (n_l, n_h, n_dr, GRID_W, w)).reshape(n_l, n_h, n_dr, GRID_W * w)
    skew = skew[..., :GRID_W * (w - 1)].reshape(n_l, n_h, n_dr, GRID_W, w - 1)
    toe = skew[..., GRID_W - 1:]
    qc = np.arange(GRID_W)[:, None]
    kc = np.arange(GRID_W)[None, :]
    win_c0 = np.clip(qc - NA_WIN_COLS // 2, 0, GRID_W - NA_WIN_COLS)
    in_cols = (kc >= win_c0) & (kc < win_c0 + NA_WIN_COLS)
    toe = jnp.where(jnp.asarray(in_cols), toe, NEG_INF)
    toe = jnp.transpose(toe, (0, 1, 3, 2, 4))
    toe = jnp.pad(toe, ((0, 0), (0, 0), (0, 0), (NA_KROWS, NA_KROWS), (0, 0)), constant_values=NEG_INF)
    tables = []
    for r0 in (0, NA_QROWS, rows - NA_QROWS):
        start = int(np.clip(r0 - NA_WIN_ROWS // 2, 0, rows - NA_KROWS))
        per_q = []
        for qr in range(NA_QROWS):
            r = r0 + qr
            ws = int(np.clip(r - NA_WIN_ROWS // 2, 0, rows - NA_WIN_ROWS))
            d0 = start - r + NA_WIN_ROWS - 1 + NA_KROWS
            in_rows = np.array([ws <= start + kl < ws + NA_WIN_ROWS for kl in range(NA_KROWS)])[:, None]
            tile = jnp.where(jnp.asarray(in_rows), toe[:, :, :, d0:d0 + NA_KROWS], NEG_INF)
            per_q.append(tile.reshape(n_l, n_h, GRID_W, NA_KROWS * GRID_W))
        tables.append(jnp.concatenate(per_q, axis=2))
    return jnp.stack(tables, axis=1)


def _diff_kernel(*refs, has_ctx, lam_init, n_kv, tk):
    if has_ctx:
        qt_ref, k_ref, vt_ref, kc_ref, vct_ref, lam_ref, g_ref, o_ref = refs[:8]
    else:
        qt_ref, k_ref, vt_ref, lam_ref, g_ref, o_ref = refs[:6]
    qpad_ref, m_ref, acc_ref, s_head_ref, p_tail_ref, alpha_tail_ref = refs[-6:]
    head_w = 2 * DIFF_QK_DIM
    n_stream = 2 * DIFF_HEADS
    last = n_stream - 1


    def scores(idx, kh):
        return jnp.dot(kh, qpad_ref[idx], preferred_element_type=F32)

    def softmax_step(idx, s):
        m_old = m_ref[idx:idx + 1, :]
        m_new = jnp.maximum(m_old, jnp.max(s, axis=0, keepdims=True))
        alpha = jnp.exp2(m_old - m_new)
        m_ref[idx:idx + 1, :] = m_new
        return alpha, jnp.exp2(s - m_new).astype(BF16)

    def accumulate(idx, alpha, p, vth):
        v_ext = jnp.concatenate([vth, jnp.ones((DIFF_ACC_ROWS - DIFF_V_DIM, vth.shape[1]), BF16)], axis=0)
        acc_ref[idx] = alpha * acc_ref[idx] + jnp.dot(v_ext, p, preferred_element_type=F32)

    def single_block(kblk, vtblk):
        k_head = lambda idx: kblk[:, (idx // 2) * head_w:(idx // 2 + 1) * head_w]
        v_head = lambda idx: vtblk[(idx // 2) * DIFF_V_DIM:(idx // 2 + 1) * DIFF_V_DIM, :]
        s_next = scores(0, k_head(0))
        pending = None
        for idx in range(n_stream):
            s = s_next
            if idx < last:
                s_next = scores(idx + 1, k_head(idx + 1))
            if pending is not None:
                accumulate(*pending)
            alpha, p = softmax_step(idx, s)
            pending = (idx, alpha, p, v_head(idx))
        accumulate(*pending)

    def k_head_at(j, idx):
        return k_ref[0, pl.ds(pl.multiple_of(j * tk, tk), tk), (idx // 2) * head_w:(idx // 2 + 1) * head_w]

    def v_head_at(j, idx):
        return vt_ref[0, (idx // 2) * DIFF_V_DIM:(idx // 2 + 1) * DIFF_V_DIM, pl.ds(pl.multiple_of(j * tk, tk), tk)]

    def pipelined_block(j, carry):
        s = s_head_ref[...]
        pending = (last, alpha_tail_ref[...], p_tail_ref[...], v_head_at(jnp.maximum(j - 1, 0), last))
        for idx in range(n_stream):
            if idx < last:
                s_next = scores(idx + 1, k_head_at(j, idx + 1))
            else:
                s_next = scores(0, k_head_at(jnp.minimum(j + 1, n_kv - 1), 0))
            accumulate(*pending)
            alpha, p = softmax_step(idx, s)
            pending = (idx, alpha, p, v_head_at(j, idx))
            s = s_next
        s_head_ref[...] = s
        alpha_tail_ref[...] = pending[1]
        p_tail_ref[...] = pending[2]
        return carry

    row = lax.broadcasted_iota(jnp.int32, (head_w, qt_ref.shape[2]), 0)
    for hd in range(DIFF_HEADS):
        qh = qt_ref[0, hd * head_w:(hd + 1) * head_w, :]
        qpad_ref[2 * hd] = jnp.where(row < DIFF_QK_DIM, qh, jnp.zeros_like(qh))
        qpad_ref[2 * hd + 1] = jnp.where(row >= DIFF_QK_DIM, qh, jnp.zeros_like(qh))
    m_ref[...] = jnp.full(m_ref.shape, -jnp.inf, F32)
    acc_ref[...] = jnp.zeros(acc_ref.shape, F32)
    if has_ctx:
        single_block(kc_ref[0], vct_ref[0])

    if n_kv == 1:
        single_block(k_ref[0], vt_ref[0])
    else:
        s_head_ref[...] = scores(0, k_head_at(0, 0))
        alpha_tail_ref[...] = jnp.ones(alpha_tail_ref.shape, F32)
        p_tail_ref[...] = jnp.zeros(p_tail_ref.shape, BF16)
        lax.fori_loop(0, n_kv, pipelined_block, 0)
        accumulate(last, alpha_tail_ref[...], p_tail_ref[...], v_head_at(n_kv - 1, last))

    lf = lam_ref[...]
    lam = (jnp.exp(jnp.sum(lf[0:1] * lf[1:2], axis=1, keepdims=True))
           - jnp.exp(jnp.sum(lf[2:3] * lf[3:4], axis=1, keepdims=True)) + lam_init)
    for hd in range(DIFF_HEADS):
        o1 = acc_ref[2 * hd, 0:DIFF_V_DIM, :] / acc_ref[2 * hd, DIFF_V_DIM:DIFF_V_DIM + 1, :]
        o2 = acc_ref[2 * hd + 1, 0:DIFF_V_DIM, :] / acc_ref[2 * hd + 1, DIFF_V_DIM:DIFF_V_DIM + 1, :]
        o = o1 - lam * o2
        o = o * lax.rsqrt(jnp.mean(o * o, axis=0, keepdims=True) + EPS) * g_ref[...] * (1.0 - lam_init)
        o_ref[0, :, hd * DIFF_V_DIM:(hd + 1) * DIFF_V_DIM] = o.T.astype(BF16)


def _diff(qt, k, vt, ctx_kv, lam_vec, subln_g, layer, lam_init):
    nb, _, n = qt.shape
    n_keys = k.shape[1]
    tq = min(DIFF_TQ, n)
    tk = min(DIFF_TK, n_keys)
    n_kv = n_keys // tk
    has_ctx = ctx_kv is not None
    in_specs = [pl.BlockSpec((1, DIFF_QK_WIDTH, tq), lambda b, i: (b, 0, i)),
                _resident((1, n_keys, DIFF_QK_WIDTH), lambda b, i: (b, 0, 0)),
                _resident((1, DIFF_WIDTH, n_keys), lambda b, i: (b, 0, 0))]
    args = [qt, k, vt]
    if has_ctx:
        n_ctx = ctx_kv[0].shape[1]
        in_specs += [pl.BlockSpec((1, n_ctx, DIFF_QK_WIDTH), lambda b, i: (b, 0, 0)),
                     pl.BlockSpec((1, DIFF_WIDTH, n_ctx), lambda b, i: (b, 0, 0))]
        args += list(ctx_kv)
    in_specs += [pl.BlockSpec((None, 4, DIFF_QK_DIM), lambda b, i: (layer, 0, 0)),
                 pl.BlockSpec((None, DIFF_V_DIM, 1), lambda b, i: (layer, 0, 0))]
    args += [lam_vec, subln_g.reshape(DEPTH, DIFF_V_DIM, 1)]
    n_stream = 2 * DIFF_HEADS
    return pl.pallas_call(
        functools.partial(_diff_kernel, has_ctx=has_ctx, lam_init=lam_init, n_kv=n_kv, tk=tk),
        out_shape=jax.ShapeDtypeStruct((nb, n, DIFF_WIDTH), BF16),
        grid=(nb, n // tq),
        in_specs=in_specs,
        out_specs=pl.BlockSpec((1, tq, DIFF_WIDTH), lambda b, i: (b, i, 0)),
        scratch_shapes=[pltpu.VMEM((n_stream, 2 * DIFF_QK_DIM, tq), BF16),
                        pltpu.VMEM((n_stream, tq), F32),
                        pltpu.VMEM((n_stream, DIFF_ACC_ROWS, tq), F32),
                        pltpu.VMEM((tk, tq), F32),
                        pltpu.VMEM((tk, tq), BF16),
                        pltpu.VMEM((1, tq), F32)],
        compiler_params=_params("parallel", "arbitrary"),
        name="diff_attn",
    )(*args)


def _ctx_attn_kernel(q_ref, k_ref, v_ref, o_ref):
    q, k, v = q_ref[0], k_ref[0], v_ref[0]
    for hd in range(NA_HEADS):
        sl = slice(hd * NA_HEAD_DIM, (hd + 1) * NA_HEAD_DIM)
        s = lax.dot_general(q[:, sl], k[:, sl], _NT, preferred_element_type=F32)
        p = jnp.exp(s - jnp.max(s, axis=1, keepdims=True))
        o = jnp.dot(p.astype(BF16), v[:, sl], preferred_element_type=F32)
        o_ref[0, :, sl] = (o / jnp.sum(p, axis=1, keepdims=True)).astype(BF16)


def _ctx_attn(q, k, v):
    nb, n, w = q.shape
    spec = pl.BlockSpec((1, n, w), lambda b: (b, 0, 0))
    return pl.pallas_call(
        _ctx_attn_kernel,
        out_shape=jax.ShapeDtypeStruct(q.shape, BF16),
        grid=(nb,),
        in_specs=[spec, spec, spec],
        out_specs=spec,
        compiler_params=_params("parallel"),
        name="ctx_dense_attn",
    )(q, k, v)


def _outproj_ffn_kernel(x_ref, mod_ref, g_ref, a_ref, pprev_ref, pcur_ref, pnext_ref, c_ref,
                        pw_ref, ps_ref, w_ref, fmod_ref, fg_ref, w1_ref, w2_ref, o_ref, act_ref,
                        *, tm, seq_len, res_w):
    i = pl.program_id(1)
    n = tm + 2 * POOL_HALO
    u = jnp.concatenate([pprev_ref[0], pcur_ref[0], pnext_ref[0]], axis=0)
    t_ext = i * tm - POOL_HALO + lax.broadcasted_iota(jnp.int32, (n, 1), 0)
    u = jnp.where((t_ext >= 0) & (t_ext < seq_len), u, 0.0)
    s2 = u + pltpu.roll(u, 1, 0)
    s4 = pltpu.roll(s2, 1, 0) + pltpu.roll(s2, n - 1, 0)
    s8 = pltpu.roll(s4, 2, 0) + pltpu.roll(s4, n - 2, 0)
    s16 = pltpu.roll(s8, 4, 0) + pltpu.roll(s8, n - 4, 0)
    cur = slice(POOL_HALO, POOL_HALO + tm)
    group = lax.broadcasted_iota(jnp.int32, (1, POOL_WIDTH), 1) // POOL_GROUP_DIM
    wsum = jnp.where(group == 0, s2[cur], jnp.where(group == 1, s4[cur], jnp.where(group == 2, s8[cur], s16[cur])))
    half = jnp.where(group == 0, POOL_WINDOWS[0] // 2,
                     jnp.where(group == 1, POOL_WINDOWS[1] // 2,
                               jnp.where(group == 2, POOL_WINDOWS[2] // 2, POOL_WINDOWS[3] // 2)))
    t = i * tm + lax.broadcasted_iota(jnp.int32, (tm, 1), 0)
    count = (jnp.clip(t + half, 0, seq_len) - jnp.clip(t - half, 0, seq_len)).astype(F32)
    pooled = (wsum / count - u[cur]).astype(BF16)
    b = jnp.dot(pooled, pw_ref[...], preferred_element_type=F32) * ps_ref[...]
    cat = jnp.concatenate([a_ref[0], b.astype(BF16), c_ref[0]], axis=1)
    y = jnp.dot(cat, w_ref[...], preferred_element_type=F32)
    x_mixed = x_ref[0] + mod_ref[0, 2:3, :] * _rms(y, g_ref[...])
    o_ref[0] = _swiglu_half_step(x_mixed, fmod_ref, fg_ref, w1_ref, w2_ref, act_ref, res_w)


def _outproj_ffn(x, mods, mod_row, norm_g, a, pin, c, pool_bd, pool_scale, w, w1, w2, layer, res_w):
    nb, n, _ = x.shape
    tm = min(TOKEN_TILE, n)
    halo_per_tile = tm // POOL_HALO
    n_halo = n // POOL_HALO
    tok = lambda width: pl.BlockSpec((1, tm, width), lambda b, i: (b, i, 0))
    return pl.pallas_call(
        functools.partial(_outproj_ffn_kernel, tm=tm, seq_len=n, res_w=res_w),
        out_shape=jax.ShapeDtypeStruct(x.shape, F32),
        grid=(nb, n // tm),
        in_specs=[tok(D_MODEL),
                  _mod_spec(layer, mod_row, 1),
                  _norm_spec(layer, 3, 1),
                  tok(NA_WIDTH),
                  pl.BlockSpec((1, POOL_HALO, POOL_WIDTH),
                               lambda b, i: (b, jnp.maximum(i * halo_per_tile - 1, 0), 0)),
                  tok(POOL_WIDTH),
                  pl.BlockSpec((1, POOL_HALO, POOL_WIDTH),
                               lambda b, i: (b, jnp.minimum((i + 1) * halo_per_tile, n_halo - 1), 0)),
                  tok(DIFF_WIDTH),
                  pl.BlockSpec((None, POOL_WIDTH, POOL_WIDTH), lambda b, i: (layer, 0, 0)),
                  pl.BlockSpec((None, 1, POOL_WIDTH), lambda b, i: (layer, 0, 0)),
                  _resident((None, MIX_WIDTH, D_MODEL), lambda b, i: (layer, 0, 0)),
                  _mod_spec(layer, mod_row, 2),
                  _norm_spec(layer, 4, 2),
                  _resident((None, None, D_MODEL, 2 * D_FF), lambda b, i: (layer, 1, 0, 0)),
                  _resident((None, None, D_FF, D_MODEL), lambda b, i: (layer, 1, 0, 0))],
        out_specs=tok(D_MODEL),
        scratch_shapes=[pltpu.VMEM((tm, D_FF), BF16)],
        compiler_params=_params("parallel", "parallel"),
        name="mixer_out_proj_ffn",
    )(x, mods, norm_g.reshape(DEPTH, 6, 1, D_MODEL), a, pin, pin, pin, c, pool_bd, pool_scale, w,
      mods, norm_g.reshape(DEPTH, 3, 2, D_MODEL), w1, w2)


def _rope_tables(n_tokens):
    t = np.arange(n_tokens)
    row = (t // GRID_W).astype(np.float32)
    col = (t % GRID_W).astype(np.float32)
    n_freq = DIFF_QK_DIM // 4
    inv_freq = np.power(np.float32(ROPE_THETA), -np.arange(n_freq, dtype=np.float32) / np.float32(n_freq))
    ang = np.concatenate([row[:, None] * inv_freq, col[:, None] * inv_freq], axis=-1).astype(np.float32)
    cos, sin = np.cos(ang), np.sin(ang)
    tables = (np.tile(cos, (1, 4)), np.concatenate([-sin, sin, -sin, sin], axis=-1), cos.T, sin.T)
    return tuple(jnp.asarray(np.ascontiguousarray(a), F32) for a in tables)


def _identity_rope_tables(n_tokens):
    half = DIFF_QK_DIM // 2
    return (jnp.ones((n_tokens, 2 * DIFF_QK_DIM), F32), jnp.zeros((n_tokens, 2 * DIFF_QK_DIM), F32),
            jnp.ones((half, n_tokens), F32), jnp.zeros((half, n_tokens), F32))


def _block_diag(w):
    n_l, n_g, cg, _ = w.shape
    eye = jnp.eye(n_g, dtype=w.dtype)
    return (eye[None, :, None, :, None] * w[:, :, :, None, :]).reshape(n_l, n_g * cg, n_g * cg)


def kernel(x, c, ctx, c_ctx, w_ada, b_ada, norm_g, ffn_w1, ffn_w2, w_in, w_out, na_rpb, pool_w, pool_scale,
           diff_lambda, diff_subln_g):
    nb, s, _ = x.shape
    n_ctx = ctx.shape[1]
    assert nb <= 7 and s % (GRID_W * NA_QROWS) == 0 and s % DIFF_TK == 0 and n_ctx % POOL_HALO == 0

    w1 = ffn_w1.astype(BF16)
    w2 = ffn_w2.astype(BF16)
    q_lo = 3 * NA_WIDTH + POOL_WIDTH
    k_lo = q_lo + DIFF_QK_WIDTH
    v_lo = k_lo + DIFF_QK_WIDTH
    w_in_b = w_in.astype(BF16)
    w_tok = w_in_b
    w_feat = jnp.swapaxes(jnp.concatenate([w_in_b[:, :, q_lo:k_lo], w_in_b[:, :, v_lo:]], axis=2), 1, 2)
    w_out_b = w_out.astype(BF16)

    ctx_row = nb
    cvec = jnp.zeros((8, D_MODEL), F32).at[:nb].set(c).at[ctx_row].set(c_ctx)
    mods = _ada(cvec, w_ada, b_ada).reshape(DEPTH, 8, N_MOD // 3, 3, D_MODEL)
    lat_row = lambda b: b
    ctx_mod_row = lambda b: ctx_row

    rope_lat = _rope_tables(s)
    rope_ctx = _identity_rope_tables(n_ctx)
    bias = _na_bias_tables(na_rpb, s // GRID_W)
    pool_bd = _block_diag(pool_w).astype(BF16)
    pool_sc = pool_scale.reshape(DEPTH, 1, POOL_WIDTH)

    x_lat, x_ctx = x, ctx
    for layer in range(DEPTH):
        last = layer == DEPTH - 1
        lam_init = 0.8 - 0.6 * math.exp(-0.3 * layer)

        x_lat = _ffn(x_lat, mods, lat_row, norm_g, w1, w2, layer, 0, 0.5)
        x_ctx = _ffn(x_ctx, mods, ctx_mod_row, norm_g, w1, w2, layer, 0, 0.5)

        aq, ak, av, pin, dq, dk, dv = _inproj(x_lat, mods, lat_row, norm_g, w_tok, w_feat, layer, rope_lat)
        aqc, akc, avc, pinc, dqc, dkc, dvc = _inproj(x_ctx, mods, ctx_mod_row, norm_g, w_tok, w_feat, layer, rope_ctx)
        a_lat = _na(aq, ak, av, akc, avc, bias, layer)
        c_lat = _diff(dq, dk, dv, (dkc, dvc), diff_lambda, diff_subln_g, layer, lam_init)
        x_lat = _outproj_ffn(x_lat, mods, lat_row, norm_g, a_lat, pin, c_lat, pool_bd, pool_sc, w_out_b, w1, w2,
                             layer, 0.5)

        if not last:
            a_ctx = _ctx_attn(aqc, akc, avc)
            c_ctx_out = _diff(dqc, dkc, dvc, None, diff_lambda, diff_subln_g, layer, lam_init)
            x_ctx = _outproj_ffn(x_ctx, mods, ctx_mod_row, norm_g, a_ctx, pinc, c_ctx_out, pool_bd, pool_sc, w_out_b,
                                 w1, w2, layer, 0.5)
    return x_lat
```

```python
import functools
import math

import numpy as np
import jax
import jax.numpy as jnp
from jax import lax
from jax.experimental import pallas as pl
from jax.experimental.pallas import tpu as pltpu

D_MODEL = 1024
DEPTH = 2
GRID_W = 64
N_MOD = 9
D_FF = 2816
EPS = 1e-6
NEG_INF = -1e30
ROPE_THETA = 10000.0

NA_HEADS = 4
NA_HEAD_DIM = 64
NA_WIN_ROWS = 8
NA_WIN_COLS = 16
NA_WIDTH = NA_HEADS * NA_HEAD_DIM

POOL_WINDOWS = (2, 4, 8, 16)
POOL_GROUP_DIM = 64
POOL_WIDTH = len(POOL_WINDOWS) * POOL_GROUP_DIM
POOL_HALO = max(POOL_WINDOWS) // 2

DIFF_HEADS = 4
DIFF_QK_DIM = 64
DIFF_V_DIM = 2 * DIFF_QK_DIM
DIFF_QK_WIDTH = DIFF_HEADS * 2 * DIFF_QK_DIM
DIFF_WIDTH = DIFF_HEADS * DIFF_V_DIM
DIFF_Q_SCALE = DIFF_QK_DIM ** -0.5 * math.log2(math.e)

MIX_WIDTH = NA_WIDTH + POOL_WIDTH + DIFF_WIDTH
IN_WIDTH = 3 * NA_WIDTH + POOL_WIDTH + 2 * DIFF_QK_WIDTH + DIFF_WIDTH

BF16 = jnp.bfloat16
F32 = jnp.float32

V7X_VMEM_BYTES = 64 * 1024 * 1024
VMEM_LIMIT = V7X_VMEM_BYTES * 3 // 4

TOKEN_TILE = 512
FFN_TILE = 1024
FF_CHUNK = 256
ADA_COLS = 1152
NA_QROWS = 4
NA_KROWS = NA_QROWS + NA_WIN_ROWS
DIFF_TQ = 512
DIFF_TK = 1024
DIFF_ACC_ROWS = DIFF_V_DIM + 16

_NT = (((1,), (1,)), ((), ()))


def _params(*sem):
    return pltpu.CompilerParams(dimension_semantics=sem, vmem_limit_bytes=VMEM_LIMIT)


def _resident(shape, index_map):
    return pl.BlockSpec(shape, index_map, pipeline_mode=pl.Buffered(1))


def _rms(x, g):
    return x * lax.rsqrt(jnp.mean(x * x, axis=-1, keepdims=True) + EPS) * g


def _modulate(x, mod_ref, g):
    shift, scale = mod_ref[0, 0:1, :], mod_ref[0, 1:2, :]
    return _rms(x, g) * (1.0 + scale) + shift


def _ada_kernel(c_ref, w_ref, b_ref, o_ref):
    cv = c_ref[...]
    o_ref[0] = jnp.dot(cv * jax.nn.sigmoid(cv), w_ref[0], preferred_element_type=F32,
                       precision=lax.Precision.HIGHEST) + b_ref[0]


def _ada(cvec, w_ada, b_ada):
    n_out = N_MOD * D_MODEL
    return pl.pallas_call(
        _ada_kernel,
        out_shape=jax.ShapeDtypeStruct((DEPTH, 8, n_out), F32),
        grid=(DEPTH, n_out // ADA_COLS),
        in_specs=[pl.BlockSpec((8, D_MODEL), lambda l, j: (0, 0)),
                  pl.BlockSpec((1, D_MODEL, ADA_COLS), lambda l, j: (l, 0, j)),
                  pl.BlockSpec((1, 1, ADA_COLS), lambda l, j: (l, 0, j))],
        out_specs=pl.BlockSpec((1, 8, ADA_COLS), lambda l, j: (l, 0, j)),
        compiler_params=_params("parallel", "parallel"),
        name="ada_mod",
    )(cvec, w_ada, b_ada.reshape(DEPTH, 1, n_out))


def _swiglu_half_step(x, mod_ref, g_ref, w1_ref, w2_ref, act_ref, res_w):
    h = _modulate(x, mod_ref, g_ref[0:1, :]).astype(BF16)
    for c in range(D_FF // FF_CHUNK):
        lo = c * FF_CHUNK
        a = jnp.dot(h, w1_ref[:, lo:lo + FF_CHUNK], preferred_element_type=F32)
        b = jnp.dot(h, w1_ref[:, D_FF + lo:D_FF + lo + FF_CHUNK], preferred_element_type=F32)
        act_ref[:, lo:lo + FF_CHUNK] = (a * jax.nn.sigmoid(a) * b).astype(BF16)
    y = jnp.dot(act_ref[...], w2_ref[...], preferred_element_type=F32)
    return x + res_w * mod_ref[0, 2:3, :] * _rms(y, g_ref[1:2, :])


def _ffn_kernel(x_ref, mod_ref, g_ref, w1_ref, w2_ref, o_ref, act_ref, *, res_w):
    o_ref[0] = _swiglu_half_step(x_ref[0], mod_ref, g_ref, w1_ref, w2_ref, act_ref, res_w)


def _mod_spec(layer, mod_row, sub):
    return pl.BlockSpec((None, 1, None, 3, D_MODEL), lambda b, i: (layer, mod_row(b), sub, 0, 0))


def _norm_spec(layer, first, count):
    return pl.BlockSpec((None, None, count, D_MODEL), lambda b, i: (layer, first // count, 0, 0))


def _ffn(x, mods, mod_row, norm_g, w1, w2, layer, which, res_w):
    nb, n, _ = x.shape
    tm = min(FFN_TILE, n)
    return pl.pallas_call(
        functools.partial(_ffn_kernel, res_w=res_w),
        out_shape=jax.ShapeDtypeStruct(x.shape, F32),
        grid=(nb, n // tm),
        in_specs=[pl.BlockSpec((1, tm, D_MODEL), lambda b, i: (b, i, 0)),
                  _mod_spec(layer, mod_row, 2 * which),
                  _norm_spec(layer, 4 * which, 2),
                  _resident((None, None, D_MODEL, 2 * D_FF), lambda b, i: (layer, which, 0, 0)),
                  _resident((None, None, D_FF, D_MODEL), lambda b, i: (layer, which, 0, 0))],
        out_specs=pl.BlockSpec((1, tm, D_MODEL), lambda b, i: (b, i, 0)),
        scratch_shapes=[pltpu.VMEM((tm, D_FF), BF16)],
        compiler_params=_params("parallel", "parallel"),
        name="ffn",
    )(x, mods, norm_g.reshape(DEPTH, 3, 2, D_MODEL), w1, w2)


def _inproj_kernel(x_ref, mod_ref, g_ref, w_ref, wt_ref, cos_ref, sin_ref, cos_t_ref, sin_t_ref,
                   naq_ref, nak_ref, nav_ref, pin_ref, dqt_ref, dk_ref, dvt_ref):
    x = x_ref[0]
    h = _modulate(x, mod_ref, g_ref[0:1, :]).astype(BF16)

    def proj(lo, width):
        return jnp.dot(h, w_ref[:, lo:lo + width], preferred_element_type=F32)

    naq_ref[0] = (proj(0, NA_WIDTH) * (NA_HEAD_DIM ** -0.5)).astype(BF16)
    nak_ref[0] = proj(NA_WIDTH, NA_WIDTH).astype(BF16)
    nav_ref[0] = proj(2 * NA_WIDTH, NA_WIDTH).astype(BF16)
    pin_ref[0] = proj(3 * NA_WIDTH, POOL_WIDTH)

    cos, sin = cos_ref[...], sin_ref[...]
    lane = lax.broadcasted_iota(jnp.int32, cos.shape, 1)
    half = DIFF_QK_DIM // 2
    first_half = (lane % DIFF_QK_DIM) < half
    head_w = 2 * DIFF_QK_DIM
    k = proj(3 * NA_WIDTH + POOL_WIDTH + DIFF_QK_WIDTH, DIFF_QK_WIDTH)
    for hd in range(DIFF_HEADS):
        t = k[:, hd * head_w:(hd + 1) * head_w]
        partner = jnp.where(first_half, pltpu.roll(t, head_w - half, 1), pltpu.roll(t, half, 1))
        dk_ref[0, :, hd * head_w:(hd + 1) * head_w] = (t * cos + partner * sin).astype(BF16)

    qv = lax.dot_general(wt_ref[...], h, _NT, preferred_element_type=F32)
    cos_t, sin_t = cos_t_ref[...], sin_t_ref[...]
    for st in range(2 * DIFF_HEADS):
        base = st * DIFF_QK_DIM
        x1 = qv[base:base + half] * DIFF_Q_SCALE
        x2 = qv[base + half:base + DIFF_QK_DIM] * DIFF_Q_SCALE
        dqt_ref[0, base:base + half, :] = (x1 * cos_t - x2 * sin_t).astype(BF16)
        dqt_ref[0, base + half:base + DIFF_QK_DIM, :] = (x1 * sin_t + x2 * cos_t).astype(BF16)
    dvt_ref[0] = qv[DIFF_QK_WIDTH:].astype(BF16)


def _inproj(x, mods, mod_row, norm_g, w, wt, layer, rope):
    nb, n, _ = x.shape
    tm = min(TOKEN_TILE, n)
    cos, sin, cos_t, sin_t = rope
    n_tok_major = w.shape[2]
    tok = lambda width: pl.BlockSpec((1, tm, width), lambda b, i: (b, i, 0))
    feat = lambda width: pl.BlockSpec((1, width, tm), lambda b, i: (b, 0, i))
    out = lambda width, dt: jax.ShapeDtypeStruct((nb, n, width), dt)
    out_t = lambda width, dt: jax.ShapeDtypeStruct((nb, width, n), dt)
    return pl.pallas_call(
        _inproj_kernel,
        out_shape=(out(NA_WIDTH, BF16), out(NA_WIDTH, BF16), out(NA_WIDTH, BF16), out(POOL_WIDTH, F32),
                   out_t(DIFF_QK_WIDTH, BF16), out(DIFF_QK_WIDTH, BF16), out_t(DIFF_WIDTH, BF16)),
        grid=(nb, n // tm),
        in_specs=[tok(D_MODEL),
                  _mod_spec(layer, mod_row, 1),
                  _norm_spec(layer, 2, 1),
                  _resident((None, D_MODEL, n_tok_major), lambda b, i: (layer, 0, 0)),
                  _resident((None, DIFF_QK_WIDTH + DIFF_WIDTH, D_MODEL), lambda b, i: (layer, 0, 0)),
                  pl.BlockSpec((tm, 2 * DIFF_QK_DIM), lambda b, i: (i, 0)),
                  pl.BlockSpec((tm, 2 * DIFF_QK_DIM), lambda b, i: (i, 0)),
                  pl.BlockSpec((DIFF_QK_DIM // 2, tm), lambda b, i: (0, i)),
                  pl.BlockSpec((DIFF_QK_DIM // 2, tm), lambda b, i: (0, i))],
        out_specs=(tok(NA_WIDTH), tok(NA_WIDTH), tok(NA_WIDTH), tok(POOL_WIDTH),
                   feat(DIFF_QK_WIDTH), tok(DIFF_QK_WIDTH), feat(DIFF_WIDTH)),
        compiler_params=_params("parallel", "parallel"),
        name="mixer_in_proj",
    )(x, mods, norm_g.reshape(DEPTH, 6, 1, D_MODEL), w, wt, cos, sin, cos_t, sin_t)


def _na_kernel(q_ref, k_ref, v_ref, kc_ref, vc_ref, bias_ref, o_ref, *, rows):
    i = pl.program_id(1)
    start = jnp.clip(i * NA_QROWS - NA_WIN_ROWS // 2, 0, rows - NA_KROWS) * GRID_W
    start = pl.multiple_of(start, GRID_W)
    n_keys = NA_KROWS * GRID_W
    q = q_ref[0]
    kb = k_ref[0, pl.ds(start, n_keys), :]
    vb = v_ref[0, pl.ds(start, n_keys), :]
    kc, vc = kc_ref[0], vc_ref[0]
    head = lambda hd: slice(hd * NA_HEAD_DIM, (hd + 1) * NA_HEAD_DIM)

    def scores(hd):
        qh = q[:, head(hd)]
        return (lax.dot_general(qh, kb[:, head(hd)], _NT, preferred_element_type=F32) + bias_ref[0, hd],
                lax.dot_general(qh, kc[:, head(hd)], _NT, preferred_element_type=F32))

    def finish(hd, p_loc, p_ctx, denom):
        o = (jnp.dot(p_loc, vb[:, head(hd)], preferred_element_type=F32)
             + jnp.dot(p_ctx, vc[:, head(hd)], preferred_element_type=F32))
        o_ref[0, :, head(hd)] = (o / denom).astype(BF16)

    s_next = scores(0)
    pending = None
    for hd in range(NA_HEADS):
        s_loc, s_ctx = s_next
        if hd + 1 < NA_HEADS:
            s_next = scores(hd + 1)
        if pending is not None:
            finish(*pending)
        m = jnp.maximum(jnp.max(s_loc, axis=1, keepdims=True), jnp.max(s_ctx, axis=1, keepdims=True))
        p_loc = jnp.exp(s_loc - m)
        p_ctx = jnp.exp(s_ctx - m)
        denom = jnp.sum(p_loc, axis=1, keepdims=True) + jnp.sum(p_ctx, axis=1, keepdims=True)
        pending = (hd, p_loc.astype(BF16), p_ctx.astype(BF16), denom)
    finish(*pending)


def _na(q, k, v, kc, vc, bias, layer):
    nb, s, _ = q.shape
    rows = s // GRID_W
    n_steps = rows // NA_QROWS
    tq = NA_QROWS * GRID_W
    n_ctx = kc.shape[1]

    def pattern(b, i):
        return (layer, jnp.where(i == 0, 0, jnp.where(i == n_steps - 1, 2, 1)), 0, 0, 0)

    return pl.pallas_call(
        functools.partial(_na_kernel, rows=rows),
        out_shape=jax.ShapeDtypeStruct((nb, s, NA_WIDTH), BF16),
        grid=(nb, n_steps),
        in_specs=[pl.BlockSpec((1, tq, NA_WIDTH), lambda b, i: (b, i, 0)),
                  _resident((1, s, NA_WIDTH), lambda b, i: (b, 0, 0)),
                  _resident((1, s, NA_WIDTH), lambda b, i: (b, 0, 0)),
                  pl.BlockSpec((1, n_ctx, NA_WIDTH), lambda b, i: (b, 0, 0)),
                  pl.BlockSpec((1, n_ctx, NA_WIDTH), lambda b, i: (b, 0, 0)),
                  pl.BlockSpec((None, 1, NA_HEADS, tq, NA_KROWS * GRID_W), pattern)],
        out_specs=pl.BlockSpec((1, tq, NA_WIDTH), lambda b, i: (b, i, 0)),
        compiler_params=_params("parallel", "arbitrary"),
        name="neighbourhood_attn",
    )(q, k, v, kc, vc, bias)


def _na_bias_tables(rpb, rows):
    n_l, n_h, n_dr, _ = rpb.shape
    pad = GRID_W - NA_WIN_COLS
    w = 2 * GRID_W
    ext = jnp.pad(rpb.astype(F32), ((0, 0), (0, 0), (0, 0), (pad, pad + 1)), mode="edge")
    skew = jnp.broadcast_to(ext[..., None, :], (n_l, n_h, n_dr, GRID_W, w)).reshape(n_l, n_h, n_dr, GRID_W * w)
    skew = skew[..., :GRID_W * (w - 1)].reshape(n_l, n_h, n_dr, GRID_W, w - 1)
    toe = skew[..., GRID_W - 1:]
    qc = np.arange(GRID_W)[:, None]
    kc = np.arange(GRID_W)[None, :]
    win_c0 = np.clip(qc - NA_WIN_COLS // 2, 0, GRID_W - NA_WIN_COLS)
    in_cols = (kc >= win_c0) & (kc < win_c0 + NA_WIN_COLS)
    toe = jnp.where(jnp.asarray(in_cols), toe, NEG_INF)
    toe = jnp.transpose(toe, (0, 1, 3, 2, 4))
    toe = jnp.pad(toe, ((0, 0), (0, 0), (0, 0), (NA_KROWS, NA_KROWS), (0, 0)), constant_values=NEG_INF)
    tables = []
    for r0 in (0, NA_QROWS, rows - NA_QROWS):
        start = int(np.clip(r0 - NA_WIN_ROWS // 2, 0, rows - NA_KROWS))
        per_q = []
        for qr in range(NA_QROWS):
            r = r0 + qr
            ws = int(np.clip(r - NA_WIN_ROWS // 2, 0, rows - NA_WIN_ROWS))
            d0 = start - r + NA_WIN_ROWS - 1 + NA_KROWS
            in_rows = np.array([ws <= start + kl < ws + NA_WIN_ROWS for kl in range(NA_KROWS)])[:, None]
            tile = jnp.where(jnp.asarray(in_rows), toe[:, :, :, d0:d0 + NA_KROWS], NEG_INF)
            per_q.append(tile.reshape(n_l, n_h, GRID_W, NA_KROWS * GRID_W))
        tables.append(jnp.concatenate(per_q, axis=2))
    return jnp.stack(tables, axis=1)


def _diff_kernel(*refs, has_ctx, lam_init, n_kv, tk):
    if has_ctx:
        qt_ref, k_ref, vt_ref, kc_ref, vct_ref, lam_ref, g_ref, o_ref = refs[:8]
    else:
        qt_ref, k_ref, vt_ref, lam_ref, g_ref, o_ref = refs[:6]
    qpad_ref, m_ref, acc_ref, s_buf, mblk_ref, p_tail_ref, alpha_tail_ref = refs[-7:]
    head_w = 2 * DIFF_QK_DIM
    n_stream = 2 * DIFF_HEADS
    last = n_stream - 1


    def scores(idx, kh):
        return jnp.dot(kh, qpad_ref[idx], preferred_element_type=F32)

    def softmax_step(idx, s):
        m_old = m_ref[idx:idx + 1, :]
        m_new = jnp.maximum(m_old, jnp.max(s, axis=0, keepdims=True))
        alpha = jnp.exp2(m_old - m_new)
        m_ref[idx:idx + 1, :] = m_new
        return alpha, jnp.exp2(s - m_new).astype(BF16)

    def accumulate(idx, alpha, p, vth):
        v_ext = jnp.concatenate([vth, jnp.ones((DIFF_ACC_ROWS - DIFF_V_DIM, vth.shape[1]), BF16)], axis=0)
        acc_ref[idx] = alpha * acc_ref[idx] + jnp.dot(v_ext, p, preferred_element_type=F32)

    def single_block(kblk, vtblk):
        k_head = lambda idx: kblk[:, (idx // 2) * head_w:(idx // 2 + 1) * head_w]
        v_head = lambda idx: vtblk[(idx // 2) * DIFF_V_DIM:(idx // 2 + 1) * DIFF_V_DIM, :]
        s_next = scores(0, k_head(0))
        pending = None
        for idx in range(n_stream):
            s = s_next
            if idx < last:
                s_next = scores(idx + 1, k_head(idx + 1))
            if pending is not None:
                accumulate(*pending)
            alpha, p = softmax_step(idx, s)
            pending = (idx, alpha, p, v_head(idx))
        accumulate(*pending)

    def k_head_at(j, idx):
        return k_ref[0, pl.ds(pl.multiple_of(j * tk, tk), tk), (idx // 2) * head_w:(idx // 2 + 1) * head_w]

    def v_head_at(j, idx):
        return vt_ref[0, (idx // 2) * DIFF_V_DIM:(idx // 2 + 1) * DIFF_V_DIM, pl.ds(pl.multiple_of(j * tk, tk), tk)]

    def stage_scores(idx, kh):
        s = scores(idx, kh)
        s_buf[idx % 2] = s
        mblk_ref[idx % 2:idx % 2 + 1, :] = jnp.max(s, axis=0, keepdims=True)

    def softmax_staged(idx):
        slot = idx % 2
        m_old = m_ref[idx:idx + 1, :]
        m_new = jnp.maximum(m_old, mblk_ref[slot:slot + 1, :])
        alpha = jnp.exp2(m_old - m_new)
        m_ref[idx:idx + 1, :] = m_new
        return alpha, jnp.exp2(s_buf[slot] - m_new).astype(BF16)

    def pipelined_block(j, carry):
        pending = (last, alpha_tail_ref[...], p_tail_ref[...], v_head_at(jnp.maximum(j - 1, 0), last))
        for idx in range(n_stream):
            if idx < last:
                stage_scores(idx + 1, k_head_at(j, idx + 1))
            else:
                stage_scores(0, k_head_at(jnp.minimum(j + 1, n_kv - 1), 0))
            accumulate(*pending)
            alpha, p = softmax_staged(idx)
            pending = (idx, alpha, p, v_head_at(j, idx))
        alpha_tail_ref[...] = pending[1]
        p_tail_ref[...] = pending[2]
        return carry

    row = lax.broadcasted_iota(jnp.int32, (head_w, qt_ref.shape[2]), 0)
    for hd in range(DIFF_HEADS):
        qh = qt_ref[0, hd * head_w:(hd + 1) * head_w, :]
        qpad_ref[2 * hd] = jnp.where(row < DIFF_QK_DIM, qh, jnp.zeros_like(qh))
        qpad_ref[2 * hd + 1] = jnp.where(row >= DIFF_QK_DIM, qh, jnp.zeros_like(qh))
    m_ref[...] = jnp.full(m_ref.shape, -jnp.inf, F32)
    acc_ref[...] = jnp.zeros(acc_ref.shape, F32)
    if has_ctx:
        single_block(kc_ref[0], vct_ref[0])

    if n_kv == 1:
        single_block(k_ref[0], vt_ref[0])
    else:
        stage_scores(0, k_head_at(0, 0))
        alpha_tail_ref[...] = jnp.ones(alpha_tail_ref.shape, F32)
        p_tail_ref[...] = jnp.zeros(p_tail_ref.shape, BF16)
        lax.fori_loop(0, n_kv, pipelined_block, 0)
        accumulate(last, alpha_tail_ref[...], p_tail_ref[...], v_head_at(n_kv - 1, last))

    lf = lam_ref[...]
    lam = (jnp.exp(jnp.sum(lf[0:1] * lf[1:2], axis=1, keepdims=True))
           - jnp.exp(jnp.sum(lf[2:3] * lf[3:4], axis=1, keepdims=True)) + lam_init)
    for hd in range(DIFF_HEADS):
        o1 = acc_ref[2 * hd, 0:DIFF_V_DIM, :] / acc_ref[2 * hd, DIFF_V_DIM:DIFF_V_DIM + 1, :]
        o2 = acc_ref[2 * hd + 1, 0:DIFF_V_DIM, :] / acc_ref[2 * hd + 1, DIFF_V_DIM:DIFF_V_DIM + 1, :]
        o = o1 - lam * o2
        o = o * lax.rsqrt(jnp.mean(o * o, axis=0, keepdims=True) + EPS) * g_ref[...] * (1.0 - lam_init)
        o_ref[0, :, hd * DIFF_V_DIM:(hd + 1) * DIFF_V_DIM] = o.T.astype(BF16)


def _diff(qt, k, vt, ctx_kv, lam_vec, subln_g, layer, lam_init):
    nb, _, n = qt.shape
    n_keys = k.shape[1]
    tq = min(DIFF_TQ, n)
    tk = min(DIFF_TK, n_keys)
    n_kv = n_keys // tk
    has_ctx = ctx_kv is not None
    in_specs = [pl.BlockSpec((1, DIFF_QK_WIDTH, tq), lambda b, i: (b, 0, i)),
                _resident((1, n_keys, DIFF_QK_WIDTH), lambda b, i: (b, 0, 0)),
                _resident((1, DIFF_WIDTH, n_keys), lambda b, i: (b, 0, 0))]
    args = [qt, k, vt]
    if has_ctx:
        n_ctx = ctx_kv[0].shape[1]
        in_specs += [pl.BlockSpec((1, n_ctx, DIFF_QK_WIDTH), lambda b, i: (b, 0, 0)),
                     pl.BlockSpec((1, DIFF_WIDTH, n_ctx), lambda b, i: (b, 0, 0))]
        args += list(ctx_kv)
    in_specs += [pl.BlockSpec((None, 4, DIFF_QK_DIM), lambda b, i: (layer, 0, 0)),
                 pl.BlockSpec((None, DIFF_V_DIM, 1), lambda b, i: (layer, 0, 0))]
    args += [lam_vec, subln_g.reshape(DEPTH, DIFF_V_DIM, 1)]
    n_stream = 2 * DIFF_HEADS
    return pl.pallas_call(
        functools.partial(_diff_kernel, has_ctx=has_ctx, lam_init=lam_init, n_kv=n_kv, tk=tk),
        out_shape=jax.ShapeDtypeStruct((nb, n, DIFF_WIDTH), BF16),
        grid=(nb, n // tq),
        in_specs=in_specs,
        out_specs=pl.BlockSpec((1, tq, DIFF_WIDTH), lambda b, i: (b, i, 0)),
        scratch_shapes=[pltpu.VMEM((n_stream, 2 * DIFF_QK_DIM, tq), BF16),
                        pltpu.VMEM((n_stream, tq), F32),
                        pltpu.VMEM((n_stream, DIFF_ACC_ROWS, tq), F32),
                        pltpu.VMEM((2, tk, tq), F32),
                        pltpu.VMEM((2, tq), F32),
                        pltpu.VMEM((tk, tq), BF16),
                        pltpu.VMEM((1, tq), F32)],
        compiler_params=_params("parallel", "arbitrary"),
        name="diff_attn",
    )(*args)


def _ctx_attn_kernel(q_ref, k_ref, v_ref, o_ref):
    q, k, v = q_ref[0], k_ref[0], v_ref[0]
    for hd in range(NA_HEADS):
        sl = slice(hd * NA_HEAD_DIM, (hd + 1) * NA_HEAD_DIM)
        s = lax.dot_general(q[:, sl], k[:, sl], _NT, preferred_element_type=F32)
        p = jnp.exp(s - jnp.max(s, axis=1, keepdims=True))
        o = jnp.dot(p.astype(BF16), v[:, sl], preferred_element_type=F32)
        o_ref[0, :, sl] = (o / jnp.sum(p, axis=1, keepdims=True)).astype(BF16)


def _ctx_attn(q, k, v):
    nb, n, w = q.shape
    spec = pl.BlockSpec((1, n, w), lambda b: (b, 0, 0))
    return pl.pallas_call(
        _ctx_attn_kernel,
        out_shape=jax.ShapeDtypeStruct(q.shape, BF16),
        grid=(nb,),
        in_specs=[spec, spec, spec],
        out_specs=spec,
        compiler_params=_params("parallel"),
        name="ctx_dense_attn",
    )(q, k, v)


def _outproj_ffn_kernel(x_ref, mod_ref, g_ref, a_ref, pprev_ref, pcur_ref, pnext_ref, c_ref,
                        pw_ref, ps_ref, w_ref, fmod_ref, fg_ref, w1_ref, w2_ref, o_ref, act_ref,
                        *, tm, seq_len, res_w):
    i = pl.program_id(1)
    n = tm + 2 * POOL_HALO
    u = jnp.concatenate([pprev_ref[0], pcur_ref[0], pnext_ref[0]], axis=0)
    t_ext = i * tm - POOL_HALO + lax.broadcasted_iota(jnp.int32, (n, 1), 0)
    u = jnp.where((t_ext >= 0) & (t_ext < seq_len), u, 0.0)
    s2 = u + pltpu.roll(u, 1, 0)
    s4 = pltpu.roll(s2, 1, 0) + pltpu.roll(s2, n - 1, 0)
    s8 = pltpu.roll(s4, 2, 0) + pltpu.roll(s4, n - 2, 0)
    s16 = pltpu.roll(s8, 4, 0) + pltpu.roll(s8, n - 4, 0)
    cur = slice(POOL_HALO, POOL_HALO + tm)
    group = lax.broadcasted_iota(jnp.int32, (1, POOL_WIDTH), 1) // POOL_GROUP_DIM
    wsum = jnp.where(group == 0, s2[cur], jnp.where(group == 1, s4[cur], jnp.where(group == 2, s8[cur], s16[cur])))
    half = jnp.where(group == 0, POOL_WINDOWS[0] // 2,
                     jnp.where(group == 1, POOL_WINDOWS[1] // 2,
                               jnp.where(group == 2, POOL_WINDOWS[2] // 2, POOL_WINDOWS[3] // 2)))
    t = i * tm + lax.broadcasted_iota(jnp.int32, (tm, 1), 0)
    count = (jnp.clip(t + half, 0, seq_len) - jnp.clip(t - half, 0, seq_len)).astype(F32)
    pooled = (wsum / count - u[cur]).astype(BF16)
    b = jnp.dot(pooled, pw_ref[...], preferred_element_type=F32) * ps_ref[...]
    cat = jnp.concatenate([a_ref[0], b.astype(BF16), c_ref[0]], axis=1)
    y = jnp.dot(cat, w_ref[...], preferred_element_type=F32)
    x_mixed = x_ref[0] + mod_ref[0, 2:3, :] * _rms(y, g_ref[...])
    o_ref[0] = _swiglu_half_step(x_mixed, fmod_ref, fg_ref, w1_ref, w2_ref, act_ref, res_w)


def _outproj_ffn(x, mods, mod_row, norm_g, a, pin, c, pool_bd, pool_scale, w, w1, w2, layer, res_w):
    nb, n, _ = x.shape
    tm = min(TOKEN_TILE, n)
    halo_per_tile = tm // POOL_HALO
    n_halo = n // POOL_HALO
    tok = lambda width: pl.BlockSpec((1, tm, width), lambda b, i: (b, i, 0))
    return pl.pallas_call(
        functools.partial(_outproj_ffn_kernel, tm=tm, seq_len=n, res_w=res_w),
        out_shape=jax.ShapeDtypeStruct(x.shape, F32),
        grid=(nb, n // tm),
        in_specs=[tok(D_MODEL),
                  _mod_spec(layer, mod_row, 1),
                  _norm_spec(layer, 3, 1),
                  tok(NA_WIDTH),
                  pl.BlockSpec((1, POOL_HALO, POOL_WIDTH),
                               lambda b, i: (b, jnp.maximum(i * halo_per_tile - 1, 0), 0)),
                  tok(POOL_WIDTH),
                  pl.BlockSpec((1, POOL_HALO, POOL_WIDTH),
                               lambda b, i: (b, jnp.minimum((i + 1) * halo_per_tile, n_halo - 1), 0)),
                  tok(DIFF_WIDTH),
                  pl.BlockSpec((None, POOL_WIDTH, POOL_WIDTH), lambda b, i: (layer, 0, 0)),
                  pl.BlockSpec((None, 1, POOL_WIDTH), lambda b, i: (layer, 0, 0)),
                  _resident((None, MIX_WIDTH, D_MODEL), lambda b, i: (layer, 0, 0)),
                  _mod_spec(layer, mod_row, 2),
                  _norm_spec(layer, 4, 2),
                  _resident((None, None, D_MODEL, 2 * D_FF), lambda b, i: (layer, 1, 0, 0)),
                  _resident((None, None, D_FF, D_MODEL), lambda b, i: (layer, 1, 0, 0))],
        out_specs=tok(D_MODEL),
        scratch_shapes=[pltpu.VMEM((tm, D_FF), BF16)],
        compiler_params=_params("parallel", "parallel"),
        name="mixer_out_proj_ffn",
    )(x, mods, norm_g.reshape(DEPTH, 6, 1, D_MODEL), a, pin, pin, pin, c, pool_bd, pool_scale, w,
      mods, norm_g.reshape(DEPTH, 3, 2, D_MODEL), w1, w2)


def _rope_tables(n_tokens):
    t = np.arange(n_tokens)
    row = (t // GRID_W).astype(np.float32)
    col = (t % GRID_W).astype(np.float32)
    n_freq = DIFF_QK_DIM // 4
    inv_freq = np.power(np.float32(ROPE_THETA), -np.arange(n_freq, dtype=np.float32) / np.float32(n_freq))
    ang = np.concatenate([row[:, None] * inv_freq, col[:, None] * inv_freq], axis=-1).astype(np.float32)
    cos, sin = np.cos(ang), np.sin(ang)
    tables = (np.tile(cos, (1, 4)), np.concatenate([-sin, sin, -sin, sin], axis=-1), cos.T, sin.T)
    return tuple(jnp.asarray(np.ascontiguousarray(a), F32) for a in tables)


def _identity_rope_tables(n_tokens):
    half = DIFF_QK_DIM // 2
    return (jnp.ones((n_tokens, 2 * DIFF_QK_DIM), F32), jnp.zeros((n_tokens, 2 * DIFF_QK_DIM), F32),
            jnp.ones((half, n_tokens), F32), jnp.zeros((half, n_tokens), F32))


def _block_diag(w):
    n_l, n_g, cg, _ = w.shape
    eye = jnp.eye(n_g, dtype=w.dtype)
    return (eye[None, :, None, :, None] * w[:, :, :, None, :]).reshape(n_l, n_g * cg, n_g * cg)


def kernel(x, c, ctx, c_ctx, w_ada, b_ada, norm_g, ffn_w1, ffn_w2, w_in, w_out, na_rpb, pool_w, pool_scale,
           diff_lambda, diff_subln_g):
    nb, s, _ = x.shape
    n_ctx = ctx.shape[1]
    assert nb <= 7 and s % (GRID_W * NA_QROWS) == 0 and s % DIFF_TK == 0 and n_ctx % POOL_HALO == 0

    w1 = ffn_w1.astype(BF16)
    w2 = ffn_w2.astype(BF16)
    q_lo = 3 * NA_WIDTH + POOL_WIDTH
    k_lo = q_lo + DIFF_QK_WIDTH
    v_lo = k_lo + DIFF_QK_WIDTH
    w_in_b = w_in.astype(BF16)
    w_tok = w_in_b
    w_feat = jnp.swapaxes(jnp.concatenate([w_in_b[:, :, q_lo:k_lo], w_in_b[:, :, v_lo:]], axis=2), 1, 2)
    w_out_b = w_out.astype(BF16)

    ctx_row = nb
    cvec = jnp.zeros((8, D_MODEL), F32).at[:nb].set(c).at[ctx_row].set(c_ctx)
    mods = _ada(cvec, w_ada, b_ada).reshape(DEPTH, 8, N_MOD // 3, 3, D_MODEL)
    lat_row = lambda b: b
    ctx_mod_row = lambda b: ctx_row

    rope_lat = _rope_tables(s)
    rope_ctx = _identity_rope_tables(n_ctx)
    bias = _na_bias_tables(na_rpb, s // GRID_W)
    pool_bd = _block_diag(pool_w).astype(BF16)
    pool_sc = pool_scale.reshape(DEPTH, 1, POOL_WIDTH)

    x_lat, x_ctx = x, ctx
    for layer in range(DEPTH):
        last = layer == DEPTH - 1
        lam_init = 0.8 - 0.6 * math.exp(-0.3 * layer)

        x_lat = _ffn(x_lat, mods, lat_row, norm_g, w1, w2, layer, 0, 0.5)
        x_ctx = _ffn(x_ctx, mods, ctx_mod_row, norm_g, w1, w2, layer, 0, 0.5)

        aq, ak, av, pin, dq, dk, dv = _inproj(x_lat, mods, lat_row, norm_g, w_tok, w_feat, layer, rope_lat)
        aqc, akc, avc, pinc, dqc, dkc, dvc = _inproj(x_ctx, mods, ctx_mod_row, norm_g, w_tok, w_feat, layer, rope_ctx)
        a_lat = _na(aq, ak, av, akc, avc, bias, layer)
        c_lat = _diff(dq, dk, dv, (dkc, dvc), diff_lambda, diff_subln_g, layer, lam_init)
        x_lat = _outproj_ffn(x_lat, mods, lat_row, norm_g, a_lat, pin, c_lat, pool_bd, pool_sc, w_out_b, w1, w2,
                             layer, 0.5)

        if not last:
            a_ctx = _ctx_attn(aqc, akc, avc)
            c_ctx_out = _diff(dqc, dkc, dvc, None, diff_lambda, diff_subln_g, layer, lam_init)
            x_ctx = _outproj_ffn(x_ctx, mods, ctx_mod_row, norm_g, a_ctx, pinc, c_ctx_out, pool_bd, pool_sc, w_out_b,
                                 w1, w2, layer, 0.5)
    return x_lat
```

```python
import functools
import math

import numpy as np
import jax
import jax.numpy as jnp
from jax import lax
from jax.experimental import pallas as pl
from jax.experimental.pallas import tpu as pltpu

D_MODEL = 1024
DEPTH = 2
GRID_W = 64
N_MOD = 9
D_FF = 2816
EPS = 1e-6
NEG_INF = -1e30
ROPE_THETA = 10000.0

NA_HEADS = 4
NA_HEAD_DIM = 64
NA_WIN_ROWS = 8
NA_WIN_COLS = 16
NA_WIDTH = NA_HEADS * NA_HEAD_DIM

POOL_WINDOWS = (2, 4, 8, 16)
POOL_GROUP_DIM = 64
POOL_WIDTH = len(POOL_WINDOWS) * POOL_GROUP_DIM
POOL_HALO = max(POOL_WINDOWS) // 2

DIFF_HEADS = 4
DIFF_QK_DIM = 64
DIFF_V_DIM = 2 * DIFF_QK_DIM
DIFF_QK_WIDTH = DIFF_HEADS * 2 * DIFF_QK_DIM
DIFF_WIDTH = DIFF_HEADS * DIFF_V_DIM
DIFF_Q_SCALE = DIFF_QK_DIM ** -0.5 * math.log2(math.e)
NA_Q_SCALE = NA_HEAD_DIM ** -0.5 * math.log2(math.e)

MIX_WIDTH = NA_WIDTH + POOL_WIDTH + DIFF_WIDTH
FEAT_ROWS = DIFF_QK_WIDTH + DIFF_WIDTH + 2 * NA_WIDTH
IN_WIDTH = 3 * NA_WIDTH + POOL_WIDTH + 2 * DIFF_QK_WIDTH + DIFF_WIDTH

BF16 = jnp.bfloat16
F32 = jnp.float32

V7X_VMEM_BYTES = 64 * 1024 * 1024
VMEM_LIMIT = V7X_VMEM_BYTES * 3 // 4

TOKEN_TILE = 512
FFN_TILE = 1024
FF_CHUNK = 256
ADA_COLS = 1152
NA_QROWS = 4
NA_KROWS = NA_QROWS + NA_WIN_ROWS
NA_ACC_ROWS = NA_HEAD_DIM + 16
DIFF_TQ = 256
DIFF_TK = 1024
DIFF_ACC_ROWS = DIFF_V_DIM + 16

_NT = (((1,), (1,)), ((), ()))


def _params(*sem):
    return pltpu.CompilerParams(dimension_semantics=sem, vmem_limit_bytes=VMEM_LIMIT)


def _resident(shape, index_map):
    return pl.BlockSpec(shape, index_map, pipeline_mode=pl.Buffered(1))


def _rms(x, g):
    return x * lax.rsqrt(jnp.mean(x * x, axis=-1, keepdims=True) + EPS) * g


def _modulate(x, mod_ref, g):
    shift, scale = mod_ref[0, 0:1, :], mod_ref[0, 1:2, :]
    return _rms(x, g) * (1.0 + scale) + shift


def _ada_kernel(c_ref, w_ref, b_ref, o_ref):
    cv = c_ref[...]
    o_ref[0] = jnp.dot(cv * jax.nn.sigmoid(cv), w_ref[0], preferred_element_type=F32,
                       precision=lax.Precision.HIGHEST) + b_ref[0]


def _ada(cvec, w_ada, b_ada):
    n_out = N_MOD * D_MODEL
    return pl.pallas_call(
        _ada_kernel,
        out_shape=jax.ShapeDtypeStruct((DEPTH, 8, n_out), F32),
        grid=(DEPTH, n_out // ADA_COLS),
        in_specs=[pl.BlockSpec((8, D_MODEL), lambda l, j: (0, 0)),
                  pl.BlockSpec((1, D_MODEL, ADA_COLS), lambda l, j: (l, 0, j)),
                  pl.BlockSpec((1, 1, ADA_COLS), lambda l, j: (l, 0, j))],
        out_specs=pl.BlockSpec((1, 8, ADA_COLS), lambda l, j: (l, 0, j)),
        compiler_params=_params("parallel", "parallel"),
        name="ada_mod",
    )(cvec, w_ada, b_ada.reshape(DEPTH, 1, n_out))


def _swiglu_half_step(x, mod_ref, g_ref, w1_ref, w2_ref, act_ref, res_w):
    h = _modulate(x, mod_ref, g_ref[0:1, :]).astype(BF16)
    for c in range(D_FF // FF_CHUNK):
        lo = c * FF_CHUNK
        a = jnp.dot(h, w1_ref[:, lo:lo + FF_CHUNK], preferred_element_type=F32)
        b = jnp.dot(h, w1_ref[:, D_FF + lo:D_FF + lo + FF_CHUNK], preferred_element_type=F32)
        act_ref[:, lo:lo + FF_CHUNK] = (a * jax.nn.sigmoid(a) * b).astype(BF16)
    y = jnp.dot(act_ref[...], w2_ref[...], preferred_element_type=F32)
    return x + res_w * mod_ref[0, 2:3, :] * _rms(y, g_ref[1:2, :])


def _ffn_kernel(x_ref, mod_ref, g_ref, w1_ref, w2_ref, o_ref, act_ref, *, res_w):
    o_ref[0] = _swiglu_half_step(x_ref[0], mod_ref, g_ref, w1_ref, w2_ref, act_ref, res_w)


def _mod_spec(layer, mod_row, sub):
    return pl.BlockSpec((None, 1, None, 3, D_MODEL), lambda b, i: (layer, mod_row(b), sub, 0, 0))


def _norm_spec(layer, first, count):
    return pl.BlockSpec((None, None, count, D_MODEL), lambda b, i: (layer, first // count, 0, 0))


def _ffn(x, mods, mod_row, norm_g, w1, w2, layer, which, res_w):
    nb, n, _ = x.shape
    tm = min(FFN_TILE, n)
    return pl.pallas_call(
        functools.partial(_ffn_kernel, res_w=res_w),
        out_shape=jax.ShapeDtypeStruct(x.shape, F32),
        grid=(nb, n // tm),
        in_specs=[pl.BlockSpec((1, tm, D_MODEL), lambda b, i: (b, i, 0)),
                  _mod_spec(layer, mod_row, 2 * which),
                  _norm_spec(layer, 4 * which, 2),
                  _resident((None, None, D_MODEL, 2 * D_FF), lambda b, i: (layer, which, 0, 0)),
                  _resident((None, None, D_FF, D_MODEL), lambda b, i: (layer, which, 0, 0))],
        out_specs=pl.BlockSpec((1, tm, D_MODEL), lambda b, i: (b, i, 0)),
        scratch_shapes=[pltpu.VMEM((tm, D_FF), BF16)],
        compiler_params=_params("parallel", "parallel"),
        name="ffn",
    )(x, mods, norm_g.reshape(DEPTH, 3, 2, D_MODEL), w1, w2)


def _inproj_kernel(x_ref, mod_ref, g_ref, w_ref, wt_ref, cos_ref, sin_ref, cos_t_ref, sin_t_ref,
                   naqt_ref, nak_ref, navt_ref, pin_ref, dqt_ref, dk_ref, dvt_ref):
    x = x_ref[0]
    h = _modulate(x, mod_ref, g_ref[0:1, :]).astype(BF16)

    def proj(lo, width):
        return jnp.dot(h, w_ref[:, lo:lo + width], preferred_element_type=F32)

    nak_ref[0] = proj(NA_WIDTH, NA_WIDTH).astype(BF16)
    pin_ref[0] = proj(3 * NA_WIDTH, POOL_WIDTH)

    cos, sin = cos_ref[...], sin_ref[...]
    lane = lax.broadcasted_iota(jnp.int32, cos.shape, 1)
    half = DIFF_QK_DIM // 2
    first_half = (lane % DIFF_QK_DIM) < half
    head_w = 2 * DIFF_QK_DIM
    k = proj(3 * NA_WIDTH + POOL_WIDTH + DIFF_QK_WIDTH, DIFF_QK_WIDTH)
    for hd in range(DIFF_HEADS):
        t = k[:, hd * head_w:(hd + 1) * head_w]
        partner = jnp.where(first_half, pltpu.roll(t, head_w - half, 1), pltpu.roll(t, half, 1))
        dk_ref[0, :, hd * head_w:(hd + 1) * head_w] = (t * cos + partner * sin).astype(BF16)

    qv = lax.dot_general(wt_ref[...], h, _NT, preferred_element_type=F32)
    na_lo = DIFF_QK_WIDTH + DIFF_WIDTH
    naqt_ref[0] = (qv[na_lo:na_lo + NA_WIDTH] * NA_Q_SCALE).astype(BF16)
    navt_ref[0] = qv[na_lo + NA_WIDTH:].astype(BF16)
    cos_t, sin_t = cos_t_ref[...], sin_t_ref[...]
    for st in range(2 * DIFF_HEADS):
        base = st * DIFF_QK_DIM
        x1 = qv[base:base + half] * DIFF_Q_SCALE
        x2 = qv[base + half:base + DIFF_QK_DIM] * DIFF_Q_SCALE
        dqt_ref[0, base:base + half, :] = (x1 * cos_t - x2 * sin_t).astype(BF16)
        dqt_ref[0, base + half:base + DIFF_QK_DIM, :] = (x1 * sin_t + x2 * cos_t).astype(BF16)
    dvt_ref[0] = qv[DIFF_QK_WIDTH:na_lo].astype(BF16)


def _inproj(x, mods, mod_row, norm_g, w, wt, layer, rope):
    nb, n, _ = x.shape
    tm = min(TOKEN_TILE, n)
    cos, sin, cos_t, sin_t = rope
    n_tok_major = w.shape[2]
    tok = lambda width: pl.BlockSpec((1, tm, width), lambda b, i: (b, i, 0))
    feat = lambda width: pl.BlockSpec((1, width, tm), lambda b, i: (b, 0, i))
    out = lambda width, dt: jax.ShapeDtypeStruct((nb, n, width), dt)
    out_t = lambda width, dt: jax.ShapeDtypeStruct((nb, width, n), dt)
    return pl.pallas_call(
        _inproj_kernel,
        out_shape=(out_t(NA_WIDTH, BF16), out(NA_WIDTH, BF16), out_t(NA_WIDTH, BF16), out(POOL_WIDTH, F32),
                   out_t(DIFF_QK_WIDTH, BF16), out(DIFF_QK_WIDTH, BF16), out_t(DIFF_WIDTH, BF16)),
        grid=(nb, n // tm),
        in_specs=[tok(D_MODEL),
                  _mod_spec(layer, mod_row, 1),
                  _norm_spec(layer, 2, 1),
                  _resident((None, D_MODEL, n_tok_major), lambda b, i: (layer, 0, 0)),
                  _resident((None, FEAT_ROWS, D_MODEL), lambda b, i: (layer, 0, 0)),
                  pl.BlockSpec((tm, 2 * DIFF_QK_DIM), lambda b, i: (i, 0)),
                  pl.BlockSpec((tm, 2 * DIFF_QK_DIM), lambda b, i: (i, 0)),
                  pl.BlockSpec((DIFF_QK_DIM // 2, tm), lambda b, i: (0, i)),
                  pl.BlockSpec((DIFF_QK_DIM // 2, tm), lambda b, i: (0, i))],
        out_specs=(feat(NA_WIDTH), tok(NA_WIDTH), feat(NA_WIDTH), tok(POOL_WIDTH),
                   feat(DIFF_QK_WIDTH), tok(DIFF_QK_WIDTH), feat(DIFF_WIDTH)),
        compiler_params=_params("parallel", "parallel"),
        name="mixer_in_proj",
    )(x, mods, norm_g.reshape(DEPTH, 6, 1, D_MODEL), w, wt, cos, sin, cos_t, sin_t)


def _head_queries(qt, hd):
    pair = qt[(hd // 2) * 2 * NA_HEAD_DIM:(hd // 2 + 1) * 2 * NA_HEAD_DIM, :]
    row = lax.broadcasted_iota(jnp.int32, pair.shape, 0)
    own = (row >= (hd % 2) * NA_HEAD_DIM) & (row < (hd % 2 + 1) * NA_HEAD_DIM)
    return jnp.where(own, pair, jnp.zeros_like(pair))


def _head_pair_keys(k, hd):
    return k[:, (hd // 2) * 2 * NA_HEAD_DIM:(hd // 2 + 1) * 2 * NA_HEAD_DIM]


def _values_with_ones(vt):
    return jnp.concatenate([vt, jnp.ones((NA_ACC_ROWS - NA_HEAD_DIM, vt.shape[1]), BF16)], axis=0)


def _na_kernel(qt_ref, k_ref, vt_ref, kc_ref, vct_ref, bias_ref, o_ref, s_buf, m_buf, *, rows):
    i = pl.program_id(1)
    start = jnp.clip(i * NA_QROWS - NA_WIN_ROWS // 2, 0, rows - NA_KROWS) * GRID_W
    start = pl.multiple_of(start, 2 * GRID_W)
    n_keys = NA_KROWS * GRID_W
    qt = qt_ref[0]
    kb = k_ref[0, pl.ds(start, n_keys), :]
    kc = kc_ref[0]
    head = lambda hd: slice(hd * NA_HEAD_DIM, (hd + 1) * NA_HEAD_DIM)

    def stage_scores(hd):
        qh = _head_queries(qt, hd)
        s_loc = jnp.dot(_head_pair_keys(kb, hd), qh, preferred_element_type=F32) + bias_ref[0, hd]
        s_ctx = jnp.dot(_head_pair_keys(kc, hd), qh, preferred_element_type=F32)
        s_buf[hd % 2, 0:n_keys, :] = s_loc
        s_buf[hd % 2, n_keys:, :] = s_ctx
        m_buf[hd % 2] = jnp.maximum(jnp.max(s_loc, axis=0, keepdims=True), jnp.max(s_ctx, axis=0, keepdims=True))

    def finish(hd, p):
        o = (jnp.dot(_values_with_ones(vt_ref[0, head(hd), pl.ds(start, n_keys)]), p[0:n_keys],
                     preferred_element_type=F32)
             + jnp.dot(_values_with_ones(vct_ref[0, head(hd), :]), p[n_keys:], preferred_element_type=F32))
        return o[0:NA_HEAD_DIM] / o[NA_HEAD_DIM:NA_HEAD_DIM + 1]

    stage_scores(0)
    pending = None
    outs = []
    for hd in range(NA_HEADS):
        if hd + 1 < NA_HEADS:
            stage_scores(hd + 1)
        if pending is not None:
            outs.append(finish(*pending))
        pending = (hd, jnp.exp2(s_buf[hd % 2] - m_buf[hd % 2]).astype(BF16))
    outs.append(finish(*pending))
    o_ref[0] = jnp.concatenate(outs, axis=0).T.astype(BF16)


def _na(qt, k, vt, kc, vct, bias, layer):
    nb, s, _ = k.shape
    rows = s // GRID_W
    n_steps = rows // NA_QROWS
    tq = NA_QROWS * GRID_W
    n_ctx = kc.shape[1]

    def pattern(b, i):
        return (layer, jnp.where(i == 0, 0, jnp.where(i == n_steps - 1, 2, 1)), 0, 0, 0)

    return pl.pallas_call(
        functools.partial(_na_kernel, rows=rows),
        out_shape=jax.ShapeDtypeStruct((nb, s, NA_WIDTH), BF16),
        grid=(nb, n_steps),
        in_specs=[pl.BlockSpec((1, NA_WIDTH, tq), lambda b, i: (b, 0, i)),
                  _resident((1, s, NA_WIDTH), lambda b, i: (b, 0, 0)),
                  _resident((1, NA_WIDTH, s), lambda b, i: (b, 0, 0)),
                  pl.BlockSpec((1, n_ctx, NA_WIDTH), lambda b, i: (b, 0, 0)),
                  pl.BlockSpec((1, NA_WIDTH, n_ctx), lambda b, i: (b, 0, 0)),
                  pl.BlockSpec((None, 1, NA_HEADS, NA_KROWS * GRID_W, tq), pattern)],
        out_specs=pl.BlockSpec((1, tq, NA_WIDTH), lambda b, i: (b, i, 0)),
        scratch_shapes=[pltpu.VMEM((2, NA_KROWS * GRID_W + n_ctx, tq), F32),
                        pltpu.VMEM((2, 1, tq), F32)],
        compiler_params=_params("parallel", "arbitrary"),
        name="neighbourhood_attn",
    )(qt, k, vt, kc, vct, bias)


def _na_bias_tables(rpb, rows):
    n_l, n_h, n_dr, _ = rpb.shape
    pad = GRID_W - NA_WIN_COLS
    w = 2 * GRID_W
    ext = jnp.pad(rpb.astype(F32) * math.log2(math.e), ((0, 0), (0, 0), (0, 0), (pad, pad + 1)), mode="edge")
    skew = jnp.broadcast_to(ext[..., None, :], (n_l, n_h, n_dr, GRID_W, w)).reshape(n_l, n_h, n_dr, GRID_W * w)
    skew = skew[..., :GRID_W * (w - 1)].reshape(n_l, n_h, n_dr, GRID_W, w - 1)
    toe = skew[..., GRID_W - 1:]
    qc = np.arange(GRID_W)[:, None]
    kc = np.arange(GRID_W)[None, :]
    win_c0 = np.clip(qc - NA_WIN_COLS // 2, 0, GRID_W - NA_WIN_COLS)
    in_cols = (kc >= win_c0) & (kc < win_c0 + NA_WIN_COLS)
    toe = jnp.where(jnp.asarray(in_cols), toe, NEG_INF)
    toe = jnp.swapaxes(toe, 3, 4)
    toe = jnp.pad(toe, ((0, 0), (0, 0), (NA_KROWS, NA_KROWS), (0, 0), (0, 0)), constant_values=NEG_INF)
    pair = jnp.concatenate([toe[:, :, 1:], toe[:, :, :-1]], axis=-1)
    assert NA_QROWS == 4
    tables = []
    for r0 in (0, NA_QROWS, rows - NA_QROWS):
        start = int(np.clip(r0 - NA_WIN_ROWS // 2, 0, rows - NA_KROWS))
        c = start - r0 + NA_WIN_ROWS - 1 + NA_KROWS
        halves = [pair[:, :, c - 1 - qr:c - 1 - qr + NA_KROWS].reshape(n_l, n_h, NA_KROWS * GRID_W, 2 * GRID_W)
                  for qr in (0, 2)]
        valid = np.zeros((NA_KROWS, NA_QROWS), bool)
        for qr in range(NA_QROWS):
            ws = int(np.clip(r0 + qr - NA_WIN_ROWS // 2, 0, rows - NA_WIN_ROWS))
            valid[:, qr] = [ws <= start + kl < ws + NA_WIN_ROWS for kl in range(NA_KROWS)]
        valid = np.repeat(np.repeat(valid, GRID_W, axis=0), GRID_W, axis=1)
        tables.append(jnp.where(jnp.asarray(valid), jnp.concatenate(halves, axis=-1), NEG_INF))
    return jnp.stack(tables, axis=1)


def _diff_kernel(*refs, has_ctx, lam_init, n_kv, tk):
    if has_ctx:
        qt_ref, k_ref, vt_ref, kc_ref, vct_ref, lam_ref, g_ref, o_ref = refs[:8]
    else:
        qt_ref, k_ref, vt_ref, lam_ref, g_ref, o_ref = refs[:6]
    qpad_ref, m_ref, acc_ref, s_buf, mblk_ref, p_tail_ref, alpha_tail_ref = refs[-7:]
    head_w = 2 * DIFF_QK_DIM
    n_stream = 2 * DIFF_HEADS
    last = n_stream - 1


    def scores(idx, kh):
        return jnp.dot(kh, qpad_ref[idx], preferred_element_type=F32)

    def softmax_step(idx, s):
        m_old = m_ref[idx:idx + 1, :]
        m_new = jnp.maximum(m_old, jnp.max(s, axis=0, keepdims=True))
        alpha = jnp.exp2(m_old - m_new)
        m_ref[idx:idx + 1, :] = m_new
        return alpha, jnp.exp2(s - m_new).astype(BF16)

    def accumulate(idx, alpha, p, vth):
        v_ext = jnp.concatenate([vth, jnp.ones((DIFF_ACC_ROWS - DIFF_V_DIM, vth.shape[1]), BF16)], axis=0)
        acc_ref[idx] = alpha * acc_ref[idx] + jnp.dot(v_ext, p, preferred_element_type=F32)

    def single_block(kblk, vtblk):
        k_head = lambda idx: kblk[:, (idx // 2) * head_w:(idx // 2 + 1) * head_w]
        v_head = lambda idx: vtblk[(idx // 2) * DIFF_V_DIM:(idx // 2 + 1) * DIFF_V_DIM, :]
        s_next = scores(0, k_head(0))
        pending = None
        for idx in range(n_stream):
            s = s_next
            if idx < last:
                s_next = scores(idx + 1, k_head(idx + 1))
            if pending is not None:
                accumulate(*pending)
            alpha, p = softmax_step(idx, s)
            pending = (idx, alpha, p, v_head(idx))
        accumulate(*pending)

    def k_head_at(j, idx):
        return k_ref[0, pl.ds(pl.multiple_of(j * tk, tk), tk), (idx // 2) * head_w:(idx // 2 + 1) * head_w]

    def v_head_at(j, idx):
        return vt_ref[0, (idx // 2) * DIFF_V_DIM:(idx // 2 + 1) * DIFF_V_DIM, pl.ds(pl.multiple_of(j * tk, tk), tk)]

    def stage_scores(idx, kh):
        s = scores(idx, kh)
        s_buf[idx % 2] = s
        mblk_ref[idx % 2:idx % 2 + 1, :] = jnp.max(s, axis=0, keepdims=True)

    def softmax_staged(idx):
        slot = idx % 2
        m_old = m_ref[idx:idx + 1, :]
        m_new = jnp.maximum(m_old, mblk_ref[slot:slot + 1, :])
        alpha = jnp.exp2(m_old - m_new)
        m_ref[idx:idx + 1, :] = m_new
        return alpha, jnp.exp2(s_buf[slot] - m_new).astype(BF16)

    def pipelined_block(j, carry):
        pending = (last, alpha_tail_ref[...], p_tail_ref[...], v_head_at(jnp.maximum(j - 1, 0), last))
        for idx in range(n_stream):
            if idx < last:
                stage_scores(idx + 1, k_head_at(j, idx + 1))
            else:
                stage_scores(0, k_head_at(jnp.minimum(j + 1, n_kv - 1), 0))
            accumulate(*pending)
            alpha, p = softmax_staged(idx)
            pending = (idx, alpha, p, v_head_at(j, idx))
        alpha_tail_ref[...] = pending[1]
        p_tail_ref[...] = pending[2]
        return carry

    row = lax.broadcasted_iota(jnp.int32, (head_w, qt_ref.shape[2]), 0)
    for hd in range(DIFF_HEADS):
        qh = qt_ref[0, hd * head_w:(hd + 1) * head_w, :]
        qpad_ref[2 * hd] = jnp.where(row < DIFF_QK_DIM, qh, jnp.zeros_like(qh))
        qpad_ref[2 * hd + 1] = jnp.where(row >= DIFF_QK_DIM, qh, jnp.zeros_like(qh))
    m_ref[...] = jnp.full(m_ref.shape, -jnp.inf, F32)
    acc_ref[...] = jnp.zeros(acc_ref.shape, F32)
    if has_ctx:
        single_block(kc_ref[0], vct_ref[0])

    if n_kv == 1:
        single_block(k_ref[0], vt_ref[0])
    else:
        stage_scores(0, k_head_at(0, 0))
        alpha_tail_ref[...] = jnp.ones(alpha_tail_ref.shape, F32)
        p_tail_ref[...] = jnp.zeros(p_tail_ref.shape, BF16)
        lax.fori_loop(0, n_kv, pipelined_block, 0)
        accumulate(last, alpha_tail_ref[...], p_tail_ref[...], v_head_at(n_kv - 1, last))

    lf = lam_ref[...]
    lam = (jnp.exp(jnp.sum(lf[0:1] * lf[1:2], axis=1, keepdims=True))
           - jnp.exp(jnp.sum(lf[2:3] * lf[3:4], axis=1, keepdims=True)) + lam_init)
    for hd in range(DIFF_HEADS):
        o1 = acc_ref[2 * hd, 0:DIFF_V_DIM, :] / acc_ref[2 * hd, DIFF_V_DIM:DIFF_V_DIM + 1, :]
        o2 = acc_ref[2 * hd + 1, 0:DIFF_V_DIM, :] / acc_ref[2 * hd + 1, DIFF_V_DIM:DIFF_V_DIM + 1, :]
        o = o1 - lam * o2
        o = o * lax.rsqrt(jnp.mean(o * o, axis=0, keepdims=True) + EPS) * g_ref[...] * (1.0 - lam_init)
        o_ref[0, :, hd * DIFF_V_DIM:(hd + 1) * DIFF_V_DIM] = o.T.astype(BF16)


def _diff(qt, k, vt, ctx_kv, lam_vec, subln_g, layer, lam_init):
    nb, _, n = qt.shape
    n_keys = k.shape[1]
    tq = min(DIFF_TQ, n)
    tk = min(DIFF_TK, n_keys)
    n_kv = n_keys // tk
    has_ctx = ctx_kv is not None
    in_specs = [pl.BlockSpec((1, DIFF_QK_WIDTH, tq), lambda b, i: (b, 0, i)),
                _resident((1, n_keys, DIFF_QK_WIDTH), lambda b, i: (b, 0, 0)),
                _resident((1, DIFF_WIDTH, n_keys), lambda b, i: (b, 0, 0))]
    args = [qt, k, vt]
    if has_ctx:
        n_ctx = ctx_kv[0].shape[1]
        in_specs += [pl.BlockSpec((1, n_ctx, DIFF_QK_WIDTH), lambda b, i: (b, 0, 0)),
                     pl.BlockSpec((1, DIFF_WIDTH, n_ctx), lambda b, i: (b, 0, 0))]
        args += list(ctx_kv)
    in_specs += [pl.BlockSpec((None, 4, DIFF_QK_DIM), lambda b, i: (layer, 0, 0)),
                 pl.BlockSpec((None, DIFF_V_DIM, 1), lambda b, i: (layer, 0, 0))]
    args += [lam_vec, subln_g.reshape(DEPTH, DIFF_V_DIM, 1)]
    n_stream = 2 * DIFF_HEADS
    return pl.pallas_call(
        functools.partial(_diff_kernel, has_ctx=has_ctx, lam_init=lam_init, n_kv=n_kv, tk=tk),
        out_shape=jax.ShapeDtypeStruct((nb, n, DIFF_WIDTH), BF16),
        grid=(nb, n // tq),
        in_specs=in_specs,
        out_specs=pl.BlockSpec((1, tq, DIFF_WIDTH), lambda b, i: (b, i, 0)),
        scratch_shapes=[pltpu.VMEM((n_stream, 2 * DIFF_QK_DIM, tq), BF16),
                        pltpu.VMEM((n_stream, tq), F32),
                        pltpu.VMEM((n_stream, DIFF_ACC_ROWS, tq), F32),
                        pltpu.VMEM((2, tk, tq), F32),
                        pltpu.VMEM((2, tq), F32),
                        pltpu.VMEM((tk, tq), BF16),
                        pltpu.VMEM((1, tq), F32)],
        compiler_params=_params("parallel", "arbitrary"),
        name="diff_attn",
    )(*args)


def _ctx_attn_kernel(qt_ref, k_ref, vt_ref, o_ref):
    qt, k = qt_ref[0], k_ref[0]
    outs = []
    for hd in range(NA_HEADS):
        s = jnp.dot(_head_pair_keys(k, hd), _head_queries(qt, hd), preferred_element_type=F32)
        p = jnp.exp2(s - jnp.max(s, axis=0, keepdims=True)).astype(BF16)
        o = jnp.dot(_values_with_ones(vt_ref[0, hd * NA_HEAD_DIM:(hd + 1) * NA_HEAD_DIM, :]), p,
                    preferred_element_type=F32)
        outs.append(o[0:NA_HEAD_DIM] / o[NA_HEAD_DIM:NA_HEAD_DIM + 1])
    o_ref[0] = jnp.concatenate(outs, axis=0).T.astype(BF16)


def _ctx_attn(qt, k, vt):
    nb, n, w = k.shape
    tok = pl.BlockSpec((1, n, w), lambda b: (b, 0, 0))
    feat = pl.BlockSpec((1, w, n), lambda b: (b, 0, 0))
    return pl.pallas_call(
        _ctx_attn_kernel,
        out_shape=jax.ShapeDtypeStruct(k.shape, BF16),
        grid=(nb,),
        in_specs=[feat, tok, feat],
        out_specs=tok,
        compiler_params=_params("parallel"),
        name="ctx_dense_attn",
    )(qt, k, vt)


def _outproj_ffn_kernel(x_ref, mod_ref, g_ref, a_ref, pprev_ref, pcur_ref, pnext_ref, c_ref,
                        pw_ref, ps_ref, w_ref, fmod_ref, fg_ref, w1_ref, w2_ref, o_ref, act_ref,
                        *, tm, seq_len, res_w):
    i = pl.program_id(1)
    n = tm + 2 * POOL_HALO
    u = jnp.concatenate([pprev_ref[0], pcur_ref[0], pnext_ref[0]], axis=0)
    t_ext = i * tm - POOL_HALO + lax.broadcasted_iota(jnp.int32, (n, 1), 0)
    u = jnp.where((t_ext >= 0) & (t_ext < seq_len), u, 0.0)
    s2 = u + pltpu.roll(u, 1, 0)
    s4 = pltpu.roll(s2, 1, 0) + pltpu.roll(s2, n - 1, 0)
    s8 = pltpu.roll(s4, 2, 0) + pltpu.roll(s4, n - 2, 0)
    s16 = pltpu.roll(s8, 4, 0) + pltpu.roll(s8, n - 4, 0)
    cur = slice(POOL_HALO, POOL_HALO + tm)
    group = lax.broadcasted_iota(jnp.int32, (1, POOL_WIDTH), 1) // POOL_GROUP_DIM
    wsum = jnp.where(group == 0, s2[cur], jnp.where(group == 1, s4[cur], jnp.where(group == 2, s8[cur], s16[cur])))
    half = jnp.where(group == 0, POOL_WINDOWS[0] // 2,
                     jnp.where(group == 1, POOL_WINDOWS[1] // 2,
                               jnp.where(group == 2, POOL_WINDOWS[2] // 2, POOL_WINDOWS[3] // 2)))
    t = i * tm + lax.broadcasted_iota(jnp.int32, (tm, 1), 0)
    count = (jnp.clip(t + half, 0, seq_len) - jnp.clip(t - half, 0, seq_len)).astype(F32)
    pooled = (wsum / count - u[cur]).astype(BF16)
    b = jnp.dot(pooled, pw_ref[...], preferred_element_type=F32) * ps_ref[...]
    cat = jnp.concatenate([a_ref[0], b.astype(BF16), c_ref[0]], axis=1)
    y = jnp.dot(cat, w_ref[...], preferred_element_type=F32)
    x_mixed = x_ref[0] + mod_ref[0, 2:3, :] * _rms(y, g_ref[...])
    o_ref[0] = _swiglu_half_step(x_mixed, fmod_ref, fg_ref, w1_ref, w2_ref, act_ref, res_w)


def _outproj_ffn(x, mods, mod_row, norm_g, a, pin, c, pool_bd, pool_scale, w, w1, w2, layer, res_w):
    nb, n, _ = x.shape
    tm = min(TOKEN_TILE, n)
    halo_per_tile = tm // POOL_HALO
    n_halo = n // POOL_HALO
    tok = lambda width: pl.BlockSpec((1, tm, width), lambda b, i: (b, i, 0))
    return pl.pallas_call(
        functools.partial(_outproj_ffn_kernel, tm=tm, seq_len=n, res_w=res_w),
        out_shape=jax.ShapeDtypeStruct(x.shape, F32),
        grid=(nb, n // tm),
        in_specs=[tok(D_MODEL),
                  _mod_spec(layer, mod_row, 1),
                  _norm_spec(layer, 3, 1),
                  tok(NA_WIDTH),
                  pl.BlockSpec((1, POOL_HALO, POOL_WIDTH),
                               lambda b, i: (b, jnp.maximum(i * halo_per_tile - 1, 0), 0)),
                  tok(POOL_WIDTH),
                  pl.BlockSpec((1, POOL_HALO, POOL_WIDTH),
                               lambda b, i: (b, jnp.minimum((i + 1) * halo_per_tile, n_halo - 1), 0)),
                  tok(DIFF_WIDTH),
                  pl.BlockSpec((None, POOL_WIDTH, POOL_WIDTH), lambda b, i: (layer, 0, 0)),
                  pl.BlockSpec((None, 1, POOL_WIDTH), lambda b, i: (layer, 0, 0)),
                  _resident((None, MIX_WIDTH, D_MODEL), lambda b, i: (layer, 0, 0)),
                  _mod_spec(layer, mod_row, 2),
                  _norm_spec(layer, 4, 2),
                  _resident((None, None, D_MODEL, 2 * D_FF), lambda b, i: (layer, 1, 0, 0)),
                  _resident((None, None, D_FF, D_MODEL), lambda b, i: (layer, 1, 0, 0))],
        out_specs=tok(D_MODEL),
        scratch_shapes=[pltpu.VMEM((tm, D_FF), BF16)],
        compiler_params=_params("parallel", "parallel"),
        name="mixer_out_proj_ffn",
    )(x, mods, norm_g.reshape(DEPTH, 6, 1, D_MODEL), a, pin, pin, pin, c, pool_bd, pool_scale, w,
      mods, norm_g.reshape(DEPTH, 3, 2, D_MODEL), w1, w2)


def _rope_tables(n_tokens):
    t = np.arange(n_tokens)
    row = (t // GRID_W).astype(np.float32)
    col = (t % GRID_W).astype(np.float32)
    n_freq = DIFF_QK_DIM // 4
    inv_freq = np.power(np.float32(ROPE_THETA), -np.arange(n_freq, dtype=np.float32) / np.float32(n_freq))
    ang = np.concatenate([row[:, None] * inv_freq, col[:, None] * inv_freq], axis=-1).astype(np.float32)
    cos, sin = np.cos(ang), np.sin(ang)
    tables = (np.tile(cos, (1, 4)), np.concatenate([-sin, sin, -sin, sin], axis=-1), cos.T, sin.T)
    return tuple(jnp.asarray(np.ascontiguousarray(a), F32) for a in tables)


def _identity_rope_tables(n_tokens):
    half = DIFF_QK_DIM // 2
    return (jnp.ones((n_tokens, 2 * DIFF_QK_DIM), F32), jnp.zeros((n_tokens, 2 * DIFF_QK_DIM), F32),
            jnp.ones((half, n_tokens), F32), jnp.zeros((half, n_tokens), F32))


def _block_diag(w):
    n_l, n_g, cg, _ = w.shape
    eye = jnp.eye(n_g, dtype=w.dtype)
    return (eye[None, :, None, :, None] * w[:, :, :, None, :]).reshape(n_l, n_g * cg, n_g * cg)


def kernel(x, c, ctx, c_ctx, w_ada, b_ada, norm_g, ffn_w1, ffn_w2, w_in, w_out, na_rpb, pool_w, pool_scale,
           diff_lambda, diff_subln_g):
    nb, s, _ = x.shape
    n_ctx = ctx.shape[1]
    assert nb <= 7 and s % (GRID_W * NA_QROWS) == 0 and s % DIFF_TK == 0 and n_ctx % POOL_HALO == 0

    w1 = ffn_w1.astype(BF16)
    w2 = ffn_w2.astype(BF16)
    q_lo = 3 * NA_WIDTH + POOL_WIDTH
    k_lo = q_lo + DIFF_QK_WIDTH
    v_lo = k_lo + DIFF_QK_WIDTH
    w_in_b = w_in.astype(BF16)
    w_tok = w_in_b
    w_feat = jnp.swapaxes(jnp.concatenate([w_in_b[:, :, q_lo:k_lo], w_in_b[:, :, v_lo:], w_in_b[:, :, :NA_WIDTH],
                                           w_in_b[:, :, 2 * NA_WIDTH:3 * NA_WIDTH]], axis=2), 1, 2)
    w_out_b = w_out.astype(BF16)

    ctx_row = nb
    cvec = jnp.zeros((8, D_MODEL), F32).at[:nb].set(c).at[ctx_row].set(c_ctx)
    mods = _ada(cvec, w_ada, b_ada).reshape(DEPTH, 8, N_MOD // 3, 3, D_MODEL)
    lat_row = lambda b: b
    ctx_mod_row = lambda b: ctx_row

    rope_lat = _rope_tables(s)
    rope_ctx = _identity_rope_tables(n_ctx)
    bias = _na_bias_tables(na_rpb, s // GRID_W)
    pool_bd = _block_diag(pool_w).astype(BF16)
    pool_sc = pool_scale.reshape(DEPTH, 1, POOL_WIDTH)

    x_lat, x_ctx = x, ctx
    for layer in range(DEPTH):
        last = layer == DEPTH - 1
        lam_init = 0.8 - 0.6 * math.exp(-0.3 * layer)

        x_lat = _ffn(x_lat, mods, lat_row, norm_g, w1, w2, layer, 0, 0.5)
        x_ctx = _ffn(x_ctx, mods, ctx_mod_row, norm_g, w1, w2, layer, 0, 0.5)

        aq, ak, av, pin, dq, dk, dv = _inproj(x_lat, mods, lat_row, norm_g, w_tok, w_feat, layer, rope_lat)
        aqc, akc, avc, pinc, dqc, dkc, dvc = _inproj(x_ctx, mods, ctx_mod_row, norm_g, w_tok, w_feat, layer, rope_ctx)
        a_lat = _na(aq, ak, av, akc, avc, bias, layer)
        c_lat = _diff(dq, dk, dv, (dkc, dvc), diff_lambda, diff_subln_g, layer, lam_init)
        x_lat = _outproj_ffn(x_lat, mods, lat_row, norm_g, a_lat, pin, c_lat, pool_bd, pool_sc, w_out_b, w1, w2,
                             layer, 0.5)

        if not last:
            a_ctx = _ctx_attn(aqc, akc, avc)
            c_ctx_out = _diff(dqc, dkc, dvc, None, diff_lambda, diff_subln_g, layer, lam_init)
            x_ctx = _outproj_ffn(x_ctx, mods, ctx_mod_row, norm_g, a_ctx, pinc, c_ctx_out, pool_bd, pool_sc, w_out_b,
                                 w1, w2, layer, 0.5)
    return x_lat
```

```python
import functools
import math

import numpy as np
import jax
import jax.numpy as jnp
from jax import lax
from jax.experimental import pallas as pl
from jax.experimental.pallas import tpu as pltpu

D_MODEL = 1024
DEPTH = 2
GRID_W = 64
N_MOD = 9
D_FF = 2816
EPS = 1e-6
NEG_INF = -1e30
ROPE_THETA = 10000.0

NA_HEADS = 4
NA_HEAD_DIM = 64
NA_WIN_ROWS = 8
NA_WIN_COLS = 16
NA_WIDTH = NA_HEADS * NA_HEAD_DIM

POOL_WINDOWS = (2, 4, 8, 16)
POOL_GROUP_DIM = 64
POOL_WIDTH = len(POOL_WINDOWS) * POOL_GROUP_DIM
POOL_HALO = max(POOL_WINDOWS) // 2

DIFF_HEADS = 4
DIFF_QK_DIM = 64
DIFF_V_DIM = 2 * DIFF_QK_DIM
DIFF_QK_WIDTH = DIFF_HEADS * 2 * DIFF_QK_DIM
DIFF_WIDTH = DIFF_HEADS * DIFF_V_DIM
DIFF_Q_SCALE = DIFF_QK_DIM ** -0.5 * math.log2(math.e)
NA_Q_SCALE = NA_HEAD_DIM ** -0.5 * math.log2(math.e)

MIX_WIDTH = NA_WIDTH + POOL_WIDTH + DIFF_WIDTH
FEAT_ROWS = DIFF_QK_WIDTH + DIFF_WIDTH + 2 * NA_WIDTH
IN_WIDTH = 3 * NA_WIDTH + POOL_WIDTH + 2 * DIFF_QK_WIDTH + DIFF_WIDTH

BF16 = jnp.bfloat16
F32 = jnp.float32

V7X_VMEM_BYTES = 64 * 1024 * 1024
VMEM_LIMIT = V7X_VMEM_BYTES * 3 // 4

TOKEN_TILE = 512
FFN_TILE = 1024
FF_CHUNK = 256
ADA_COLS = 1152
NA_QROWS = 4
NA_KROWS = NA_QROWS + NA_WIN_ROWS
NA_ACC_ROWS = NA_HEAD_DIM + 16
DIFF_TQ = 512
DIFF_TK = 1024
DIFF_ACC_ROWS = DIFF_V_DIM + 16

_NT = (((1,), (1,)), ((), ()))


def _params(*sem):
    return pltpu.CompilerParams(dimension_semantics=sem, vmem_limit_bytes=VMEM_LIMIT)


def _resident(shape, index_map):
    return pl.BlockSpec(shape, index_map, pipeline_mode=pl.Buffered(1))


def _rms(x, g):
    return x * lax.rsqrt(jnp.mean(x * x, axis=-1, keepdims=True) + EPS) * g


def _modulate(x, mod_ref, g):
    shift, scale = mod_ref[0, 0:1, :], mod_ref[0, 1:2, :]
    return _rms(x, g) * (1.0 + scale) + shift


def _ada_kernel(c_ref, w_ref, b_ref, o_ref):
    cv = c_ref[...]
    o_ref[0] = jnp.dot(cv * jax.nn.sigmoid(cv), w_ref[0], preferred_element_type=F32) + b_ref[0]


def _ada(cvec, w_ada, b_ada):
    n_out = N_MOD * D_MODEL
    return pl.pallas_call(
        _ada_kernel,
        out_shape=jax.ShapeDtypeStruct((DEPTH, 8, n_out), F32),
        grid=(DEPTH, n_out // ADA_COLS),
        in_specs=[pl.BlockSpec((8, D_MODEL), lambda l, j: (0, 0)),
                  pl.BlockSpec((1, D_MODEL, ADA_COLS), lambda l, j: (l, 0, j)),
                  pl.BlockSpec((1, 1, ADA_COLS), lambda l, j: (l, 0, j))],
        out_specs=pl.BlockSpec((1, 8, ADA_COLS), lambda l, j: (l, 0, j)),
        compiler_params=_params("parallel", "parallel"),
        name="ada_mod",
    )(cvec, w_ada, b_ada.reshape(DEPTH, 1, n_out))


def _swiglu_half_step(x, mod_ref, g_ref, w1_ref, w2_ref, act_ref, res_w):
    h = _modulate(x, mod_ref, g_ref[0:1, :]).astype(BF16)
    for c in range(D_FF // FF_CHUNK):
        lo = c * FF_CHUNK
        a = jnp.dot(h, w1_ref[:, lo:lo + FF_CHUNK], preferred_element_type=F32)
        b = jnp.dot(h, w1_ref[:, D_FF + lo:D_FF + lo + FF_CHUNK], preferred_element_type=F32)
        act_ref[:, lo:lo + FF_CHUNK] = (a * jax.nn.sigmoid(a) * b).astype(BF16)
    y = jnp.dot(act_ref[...], w2_ref[...], preferred_element_type=F32)
    return x + res_w * mod_ref[0, 2:3, :] * _rms(y, g_ref[1:2, :])


def _ffn_kernel(x_ref, mod_ref, g_ref, w1_ref, w2_ref, o_ref, act_ref, *, res_w):
    o_ref[0] = _swiglu_half_step(x_ref[0], mod_ref, g_ref, w1_ref, w2_ref, act_ref, res_w)


def _mod_spec(layer, mod_row, sub):
    return pl.BlockSpec((None, 1, None, 3, D_MODEL), lambda b, i: (layer, mod_row(b), sub, 0, 0))


def _norm_spec(layer, first, count):
    return pl.BlockSpec((None, None, count, D_MODEL), lambda b, i: (layer, first // count, 0, 0))


def _ffn(x, mods, mod_row, norm_g, w1, w2, layer, which, res_w):
    nb, n, _ = x.shape
    tm = min(FFN_TILE, n)
    return pl.pallas_call(
        functools.partial(_ffn_kernel, res_w=res_w),
        out_shape=jax.ShapeDtypeStruct(x.shape, F32),
        grid=(nb, n // tm),
        in_specs=[pl.BlockSpec((1, tm, D_MODEL), lambda b, i: (b, i, 0)),
                  _mod_spec(layer, mod_row, 2 * which),
                  _norm_spec(layer, 4 * which, 2),
                  _resident((None, None, D_MODEL, 2 * D_FF), lambda b, i: (layer, which, 0, 0)),
                  _resident((None, None, D_FF, D_MODEL), lambda b, i: (layer, which, 0, 0))],
        out_specs=pl.BlockSpec((1, tm, D_MODEL), lambda b, i: (b, i, 0)),
        scratch_shapes=[pltpu.VMEM((tm, D_FF), BF16)],
        compiler_params=_params("parallel", "parallel"),
        name="ffn",
    )(x, mods, norm_g.reshape(DEPTH, 3, 2, D_MODEL), w1, w2)


def _inproj_kernel(x_ref, mod_ref, g_ref, w_ref, wt_ref, cos_ref, sin_ref, cos_t_ref, sin_t_ref,
                   naqt_ref, nak_ref, navt_ref, pin_ref, dqt_ref, dk_ref, dvt_ref):
    x = x_ref[0]
    h = _modulate(x, mod_ref, g_ref[0:1, :]).astype(BF16)

    def proj(lo, width):
        return jnp.dot(h, w_ref[:, lo:lo + width], preferred_element_type=F32)

    nak_ref[0] = proj(NA_WIDTH, NA_WIDTH).astype(BF16)
    pin_ref[0] = proj(3 * NA_WIDTH, POOL_WIDTH)

    cos, sin = cos_ref[...], sin_ref[...]
    lane = lax.broadcasted_iota(jnp.int32, cos.shape, 1)
    half = DIFF_QK_DIM // 2
    first_half = (lane % DIFF_QK_DIM) < half
    head_w = 2 * DIFF_QK_DIM
    k = proj(3 * NA_WIDTH + POOL_WIDTH + DIFF_QK_WIDTH, DIFF_QK_WIDTH)
    for hd in range(DIFF_HEADS):
        t = k[:, hd * head_w:(hd + 1) * head_w]
        partner = jnp.where(first_half, pltpu.roll(t, head_w - half, 1), pltpu.roll(t, half, 1))
        dk_ref[0, :, hd * head_w:(hd + 1) * head_w] = (t * cos + partner * sin).astype(BF16)

    qv = lax.dot_general(wt_ref[...], h, _NT, preferred_element_type=F32)
    na_lo = DIFF_QK_WIDTH + DIFF_WIDTH
    naqt_ref[0] = (qv[na_lo:na_lo + NA_WIDTH] * NA_Q_SCALE).astype(BF16)
    navt_ref[0] = qv[na_lo + NA_WIDTH:].astype(BF16)
    cos_t, sin_t = cos_t_ref[...], sin_t_ref[...]
    for st in range(2 * DIFF_HEADS):
        base = st * DIFF_QK_DIM
        x1 = qv[base:base + half] * DIFF_Q_SCALE
        x2 = qv[base + half:base + DIFF_QK_DIM] * DIFF_Q_SCALE
        dqt_ref[0, base:base + half, :] = (x1 * cos_t - x2 * sin_t).astype(BF16)
        dqt_ref[0, base + half:base + DIFF_QK_DIM, :] = (x1 * sin_t + x2 * cos_t).astype(BF16)
    dvt_ref[0] = qv[DIFF_QK_WIDTH:na_lo].astype(BF16)


def _inproj(x, mods, mod_row, norm_g, w, wt, layer, rope):
    nb, n, _ = x.shape
    tm = min(TOKEN_TILE, n)
    cos, sin, cos_t, sin_t = rope
    n_tok_major = w.shape[2]
    tok = lambda width: pl.BlockSpec((1, tm, width), lambda b, i: (b, i, 0))
    feat = lambda width: pl.BlockSpec((1, width, tm), lambda b, i: (b, 0, i))
    out = lambda width, dt: jax.ShapeDtypeStruct((nb, n, width), dt)
    out_t = lambda width, dt: jax.ShapeDtypeStruct((nb, width, n), dt)
    return pl.pallas_call(
        _inproj_kernel,
        out_shape=(out_t(NA_WIDTH, BF16), out(NA_WIDTH, BF16), out_t(NA_WIDTH, BF16), out(POOL_WIDTH, F32),
                   out_t(DIFF_QK_WIDTH, BF16), out(DIFF_QK_WIDTH, BF16), out_t(DIFF_WIDTH, BF16)),
        grid=(nb, n // tm),
        in_specs=[tok(D_MODEL),
                  _mod_spec(layer, mod_row, 1),
                  _norm_spec(layer, 2, 1),
                  _resident((None, D_MODEL, n_tok_major), lambda b, i: (layer, 0, 0)),
                  _resident((None, FEAT_ROWS, D_MODEL), lambda b, i: (layer, 0, 0)),
                  pl.BlockSpec((tm, 2 * DIFF_QK_DIM), lambda b, i: (i, 0)),
                  pl.BlockSpec((tm, 2 * DIFF_QK_DIM), lambda b, i: (i, 0)),
                  pl.BlockSpec((DIFF_QK_DIM // 2, tm), lambda b, i: (0, i)),
                  pl.BlockSpec((DIFF_QK_DIM // 2, tm), lambda b, i: (0, i))],
        out_specs=(feat(NA_WIDTH), tok(NA_WIDTH), feat(NA_WIDTH), tok(POOL_WIDTH),
                   feat(DIFF_QK_WIDTH), tok(DIFF_QK_WIDTH), feat(DIFF_WIDTH)),
        compiler_params=_params("parallel", "parallel"),
        name="mixer_in_proj",
    )(x, mods, norm_g.reshape(DEPTH, 6, 1, D_MODEL), w, wt, cos, sin, cos_t, sin_t)


def _head_queries(qt, hd):
    pair = qt[(hd // 2) * 2 * NA_HEAD_DIM:(hd // 2 + 1) * 2 * NA_HEAD_DIM, :]
    row = lax.broadcasted_iota(jnp.int32, pair.shape, 0)
    own = (row >= (hd % 2) * NA_HEAD_DIM) & (row < (hd % 2 + 1) * NA_HEAD_DIM)
    return jnp.where(own, pair, jnp.zeros_like(pair))


def _head_pair_keys(k, hd):
    return k[:, (hd // 2) * 2 * NA_HEAD_DIM:(hd // 2 + 1) * 2 * NA_HEAD_DIM]


def _values_with_ones(vt):
    return jnp.concatenate([vt, jnp.ones((NA_ACC_ROWS - NA_HEAD_DIM, vt.shape[1]), BF16)], axis=0)


def _na_kernel(qt_ref, k_ref, vt_ref, kc_ref, vct_ref, bias_ref, o_ref, s_buf, m_buf, *, rows):
    i = pl.program_id(1)
    start = jnp.clip(i * NA_QROWS - NA_WIN_ROWS // 2, 0, rows - NA_KROWS) * GRID_W
    start = pl.multiple_of(start, 2 * GRID_W)
    n_keys = NA_KROWS * GRID_W
    qt = qt_ref[0]
    kb = k_ref[0, pl.ds(start, n_keys), :]
    kc = kc_ref[0]
    head = lambda hd: slice(hd * NA_HEAD_DIM, (hd + 1) * NA_HEAD_DIM)

    def stage_scores(hd):
        qh = _head_queries(qt, hd)
        s_loc = jnp.dot(_head_pair_keys(kb, hd), qh, preferred_element_type=F32) + bias_ref[0, hd]
        s_ctx = jnp.dot(_head_pair_keys(kc, hd), qh, preferred_element_type=F32)
        s_buf[hd % 2, 0:n_keys, :] = s_loc
        s_buf[hd % 2, n_keys:, :] = s_ctx
        m_buf[hd % 2] = jnp.maximum(jnp.max(s_loc, axis=0, keepdims=True), jnp.max(s_ctx, axis=0, keepdims=True))

    def finish(hd, p):
        o = (jnp.dot(_values_with_ones(vt_ref[0, head(hd), pl.ds(start, n_keys)]), p[0:n_keys],
                     preferred_element_type=F32)
             + jnp.dot(_values_with_ones(vct_ref[0, head(hd), :]), p[n_keys:], preferred_element_type=F32))
        return o[0:NA_HEAD_DIM] / o[NA_HEAD_DIM:NA_HEAD_DIM + 1]

    stage_scores(0)
    pending = None
    outs = []
    for hd in range(NA_HEADS):
        if hd + 1 < NA_HEADS:
            stage_scores(hd + 1)
        if pending is not None:
            outs.append(finish(*pending))
        pending = (hd, jnp.exp2(s_buf[hd % 2] - m_buf[hd % 2]).astype(BF16))
    outs.append(finish(*pending))
    o_ref[0] = jnp.concatenate(outs, axis=0).T.astype(BF16)


def _na(qt, k, vt, kc, vct, bias, layer):
    nb, s, _ = k.shape
    rows = s // GRID_W
    n_steps = rows // NA_QROWS
    tq = NA_QROWS * GRID_W
    n_ctx = kc.shape[1]

    def pattern(b, i):
        return (layer, jnp.where(i == 0, 0, jnp.where(i == n_steps - 1, 2, 1)), 0, 0, 0)

    return pl.pallas_call(
        functools.partial(_na_kernel, rows=rows),
        out_shape=jax.ShapeDtypeStruct((nb, s, NA_WIDTH), BF16),
        grid=(nb, n_steps),
        in_specs=[pl.BlockSpec((1, NA_WIDTH, tq), lambda b, i: (b, 0, i)),
                  _resident((1, s, NA_WIDTH), lambda b, i: (b, 0, 0)),
                  _resident((1, NA_WIDTH, s), lambda b, i: (b, 0, 0)),
                  pl.BlockSpec((1, n_ctx, NA_WIDTH), lambda b, i: (b, 0, 0)),
                  pl.BlockSpec((1, NA_WIDTH, n_ctx), lambda b, i: (b, 0, 0)),
                  pl.BlockSpec((None, 1, NA_HEADS, NA_KROWS * GRID_W, tq), pattern)],
        out_specs=pl.BlockSpec((1, tq, NA_WIDTH), lambda b, i: (b, i, 0)),
        scratch_shapes=[pltpu.VMEM((2, NA_KROWS * GRID_W + n_ctx, tq), F32),
                        pltpu.VMEM((2, 1, tq), F32)],
        compiler_params=_params("parallel", "arbitrary"),
        name="neighbourhood_attn",
    )(qt, k, vt, kc, vct, bias)


def _na_bias_tables(rpb, rows):
    n_l, n_h, n_dr, _ = rpb.shape
    pad = GRID_W - NA_WIN_COLS
    w = 2 * GRID_W
    ext = jnp.pad(rpb.astype(F32) * math.log2(math.e), ((0, 0), (0, 0), (0, 0), (pad, pad + 1)), mode="edge")
    skew = jnp.broadcast_to(ext[..., None, :], (n_l, n_h, n_dr, GRID_W, w)).reshape(n_l, n_h, n_dr, GRID_W * w)
    skew = skew[..., :GRID_W * (w - 1)].reshape(n_l, n_h, n_dr, GRID_W, w - 1)
    toe = skew[..., GRID_W - 1:]
    qc = np.arange(GRID_W)[:, None]
    kc = np.arange(GRID_W)[None, :]
    win_c0 = np.clip(qc - NA_WIN_COLS // 2, 0, GRID_W - NA_WIN_COLS)
    in_cols = (kc >= win_c0) & (kc < win_c0 + NA_WIN_COLS)
    toe = jnp.where(jnp.asarray(in_cols), toe, NEG_INF)
    toe = jnp.swapaxes(toe, 3, 4)
    toe = jnp.pad(toe, ((0, 0), (0, 0), (NA_KROWS, NA_KROWS), (0, 0), (0, 0)), constant_values=NEG_INF)
    pair = jnp.concatenate([toe[:, :, 1:], toe[:, :, :-1]], axis=-1)
    assert NA_QROWS == 4
    tables = []
    for r0 in (0, NA_QROWS, rows - NA_QROWS):
        start = int(np.clip(r0 - NA_WIN_ROWS // 2, 0, rows - NA_KROWS))
        c = start - r0 + NA_WIN_ROWS - 1 + NA_KROWS
        halves = [pair[:, :, c - 1 - qr:c - 1 - qr + NA_KROWS].reshape(n_l, n_h, NA_KROWS * GRID_W, 2 * GRID_W)
                  for qr in (0, 2)]
        valid = np.zeros((NA_KROWS, NA_QROWS), bool)
        for qr in range(NA_QROWS):
            ws = int(np.clip(r0 + qr - NA_WIN_ROWS // 2, 0, rows - NA_WIN_ROWS))
            valid[:, qr] = [ws <= start + kl < ws + NA_WIN_ROWS for kl in range(NA_KROWS)]
        valid = np.repeat(np.repeat(valid, GRID_W, axis=0), GRID_W, axis=1)
        tables.append(jnp.where(jnp.asarray(valid), jnp.concatenate(halves, axis=-1), NEG_INF))
    return jnp.stack(tables, axis=1)


def _diff_kernel(*refs, has_ctx, lam_init, n_kv, tk):
    if has_ctx:
        qt_ref, k_ref, vt_ref, kc_ref, vct_ref, lam_ref, g_ref, o_ref = refs[:8]
    else:
        qt_ref, k_ref, vt_ref, lam_ref, g_ref, o_ref = refs[:6]
    qpad_ref, m_ref, acc_ref, s_buf, mblk_ref, p_tail_ref, alpha_tail_ref = refs[-7:]
    head_w = 2 * DIFF_QK_DIM
    n_stream = 2 * DIFF_HEADS
    last = n_stream - 1


    def scores(idx, kh):
        return jnp.dot(kh, qpad_ref[idx], preferred_element_type=F32)

    def softmax_step(idx, s):
        m_old = m_ref[idx:idx + 1, :]
        m_new = jnp.maximum(m_old, jnp.max(s, axis=0, keepdims=True))
        alpha = jnp.exp2(m_old - m_new)
        m_ref[idx:idx + 1, :] = m_new
        return alpha, jnp.exp2(s - m_new).astype(BF16)

    def accumulate(idx, alpha, p, vth):
        v_ext = jnp.concatenate([vth, jnp.ones((DIFF_ACC_ROWS - DIFF_V_DIM, vth.shape[1]), BF16)], axis=0)
        acc_ref[idx] = alpha * acc_ref[idx] + jnp.dot(v_ext, p, preferred_element_type=F32)

    def single_block(kblk, vtblk):
        k_head = lambda idx: kblk[:, (idx // 2) * head_w:(idx // 2 + 1) * head_w]
        v_head = lambda idx: vtblk[(idx // 2) * DIFF_V_DIM:(idx // 2 + 1) * DIFF_V_DIM, :]
        s_next = scores(0, k_head(0))
        pending = None
        for idx in range(n_stream):
            s = s_next
            if idx < last:
                s_next = scores(idx + 1, k_head(idx + 1))
            if pending is not None:
                accumulate(*pending)
            alpha, p = softmax_step(idx, s)
            pending = (idx, alpha, p, v_head(idx))
        accumulate(*pending)

    def k_head_at(j, idx):
        return k_ref[0, pl.ds(pl.multiple_of(j * tk, tk), tk), (idx // 2) * head_w:(idx // 2 + 1) * head_w]

    def v_head_at(j, idx):
        return vt_ref[0, (idx // 2) * DIFF_V_DIM:(idx // 2 + 1) * DIFF_V_DIM, pl.ds(pl.multiple_of(j * tk, tk), tk)]

    def stage_scores(idx, kh):
        s = scores(idx, kh)
        s_buf[idx % 2] = s
        mblk_ref[idx % 2:idx % 2 + 1, :] = jnp.max(s, axis=0, keepdims=True)

    def softmax_staged(idx):
        slot = idx % 2
        m_old = m_ref[idx:idx + 1, :]
        m_new = jnp.maximum(m_old, mblk_ref[slot:slot + 1, :])
        alpha = jnp.exp2(m_old - m_new)
        m_ref[idx:idx + 1, :] = m_new
        return alpha, jnp.exp2(s_buf[slot] - m_new).astype(BF16)

    def pipelined_block(j, carry):
        pending = (last, alpha_tail_ref[...], p_tail_ref[...], v_head_at(jnp.maximum(j - 1, 0), last))
        for idx in range(n_stream):
            if idx < last:
                stage_scores(idx + 1, k_head_at(j, idx + 1))
            else:
                stage_scores(0, k_head_at(jnp.minimum(j + 1, n_kv - 1), 0))
            accumulate(*pending)
            alpha, p = softmax_staged(idx)
            pending = (idx, alpha, p, v_head_at(j, idx))
        alpha_tail_ref[...] = pending[1]
        p_tail_ref[...] = pending[2]
        return carry

    row = lax.broadcasted_iota(jnp.int32, (head_w, qt_ref.shape[2]), 0)
    for hd in range(DIFF_HEADS):
        qh = qt_ref[0, hd * head_w:(hd + 1) * head_w, :]
        qpad_ref[2 * hd] = jnp.where(row < DIFF_QK_DIM, qh, jnp.zeros_like(qh))
        qpad_ref[2 * hd + 1] = jnp.where(row >= DIFF_QK_DIM, qh, jnp.zeros_like(qh))
    m_ref[...] = jnp.full(m_ref.shape, -jnp.inf, F32)
    acc_ref[...] = jnp.zeros(acc_ref.shape, F32)
    if has_ctx:
        single_block(kc_ref[0], vct_ref[0])

    if n_kv == 1:
        single_block(k_ref[0], vt_ref[0])
    else:
        stage_scores(0, k_head_at(0, 0))
        alpha_tail_ref[...] = jnp.ones(alpha_tail_ref.shape, F32)
        p_tail_ref[...] = jnp.zeros(p_tail_ref.shape, BF16)
        lax.fori_loop(0, n_kv, pipelined_block, 0)
        accumulate(last, alpha_tail_ref[...], p_tail_ref[...], v_head_at(n_kv - 1, last))

    lf = lam_ref[...]
    lam = (jnp.exp(jnp.sum(lf[0:1] * lf[1:2], axis=1, keepdims=True))
           - jnp.exp(jnp.sum(lf[2:3] * lf[3:4], axis=1, keepdims=True)) + lam_init)
    for hd in range(DIFF_HEADS):
        o1 = acc_ref[2 * hd, 0:DIFF_V_DIM, :] / acc_ref[2 * hd, DIFF_V_DIM:DIFF_V_DIM + 1, :]
        o2 = acc_ref[2 * hd + 1, 0:DIFF_V_DIM, :] / acc_ref[2 * hd + 1, DIFF_V_DIM:DIFF_V_DIM + 1, :]
        o = o1 - lam * o2
        o = o * lax.rsqrt(jnp.mean(o * o, axis=0, keepdims=True) + EPS) * g_ref[...] * (1.0 - lam_init)
        o_ref[0, :, hd * DIFF_V_DIM:(hd + 1) * DIFF_V_DIM] = o.T.astype(BF16)


def _diff(qt, k, vt, ctx_kv, lam_vec, subln_g, layer, lam_init):
    nb, _, n = qt.shape
    n_keys = k.shape[1]
    tq = min(DIFF_TQ, n)
    tk = min(DIFF_TK, n_keys)
    n_kv = n_keys // tk
    has_ctx = ctx_kv is not None
    in_specs = [pl.BlockSpec((1, DIFF_QK_WIDTH, tq), lambda b, i: (b, 0, i)),
                _resident((1, n_keys, DIFF_QK_WIDTH), lambda b, i: (b, 0, 0)),
                _resident((1, DIFF_WIDTH, n_keys), lambda b, i: (b, 0, 0))]
    args = [qt, k, vt]
    if has_ctx:
        n_ctx = ctx_kv[0].shape[1]
        in_specs += [pl.BlockSpec((1, n_ctx, DIFF_QK_WIDTH), lambda b, i: (b, 0, 0)),
                     pl.BlockSpec((1, DIFF_WIDTH, n_ctx), lambda b, i: (b, 0, 0))]
        args += list(ctx_kv)
    in_specs += [pl.BlockSpec((None, 4, DIFF_QK_DIM), lambda b, i: (layer, 0, 0)),
                 pl.BlockSpec((None, DIFF_V_DIM, 1), lambda b, i: (layer, 0, 0))]
    args += [lam_vec, subln_g.reshape(DEPTH, DIFF_V_DIM, 1)]
    n_stream = 2 * DIFF_HEADS
    return pl.pallas_call(
        functools.partial(_diff_kernel, has_ctx=has_ctx, lam_init=lam_init, n_kv=n_kv, tk=tk),
        out_shape=jax.ShapeDtypeStruct((nb, n, DIFF_WIDTH), BF16),
        grid=(nb, n // tq),
        in_specs=in_specs,
        out_specs=pl.BlockSpec((1, tq, DIFF_WIDTH), lambda b, i: (b, i, 0)),
        scratch_shapes=[pltpu.VMEM((n_stream, 2 * DIFF_QK_DIM, tq), BF16),
                        pltpu.VMEM((n_stream, tq), F32),
                        pltpu.VMEM((n_stream, DIFF_ACC_ROWS, tq), F32),
                        pltpu.VMEM((2, tk, tq), F32),
                        pltpu.VMEM((2, tq), F32),
                        pltpu.VMEM((tk, tq), BF16),
                        pltpu.VMEM((1, tq), F32)],
        compiler_params=_params("parallel", "arbitrary"),
        name="diff_attn",
    )(*args)


def _ctx_attn_kernel(qt_ref, k_ref, vt_ref, o_ref):
    qt, k = qt_ref[0], k_ref[0]
    outs = []
    for hd in range(NA_HEADS):
        s = jnp.dot(_head_pair_keys(k, hd), _head_queries(qt, hd), preferred_element_type=F32)
        p = jnp.exp2(s - jnp.max(s, axis=0, keepdims=True)).astype(BF16)
        o = jnp.dot(_values_with_ones(vt_ref[0, hd * NA_HEAD_DIM:(hd + 1) * NA_HEAD_DIM, :]), p,
                    preferred_element_type=F32)
        outs.append(o[0:NA_HEAD_DIM] / o[NA_HEAD_DIM:NA_HEAD_DIM + 1])
    o_ref[0] = jnp.concatenate(outs, axis=0).T.astype(BF16)


def _ctx_attn(qt, k, vt):
    nb, n, w = k.shape
    tok = pl.BlockSpec((1, n, w), lambda b: (b, 0, 0))
    feat = pl.BlockSpec((1, w, n), lambda b: (b, 0, 0))
    return pl.pallas_call(
        _ctx_attn_kernel,
        out_shape=jax.ShapeDtypeStruct(k.shape, BF16),
        grid=(nb,),
        in_specs=[feat, tok, feat],
        out_specs=tok,
        compiler_params=_params("parallel"),
        name="ctx_dense_attn",
    )(qt, k, vt)


def _outproj_ffn_kernel(x_ref, mod_ref, g_ref, a_ref, pprev_ref, pcur_ref, pnext_ref, c_ref,
                        pw_ref, ps_ref, w_ref, fmod_ref, fg_ref, w1_ref, w2_ref, o_ref, act_ref,
                        *, tm, seq_len, res_w):
    i = pl.program_id(1)
    n = tm + 2 * POOL_HALO
    u = jnp.concatenate([pprev_ref[0], pcur_ref[0], pnext_ref[0]], axis=0)
    t_ext = i * tm - POOL_HALO + lax.broadcasted_iota(jnp.int32, (n, 1), 0)
    u = jnp.where((t_ext >= 0) & (t_ext < seq_len), u, 0.0)
    s2 = u + pltpu.roll(u, 1, 0)
    s4 = pltpu.roll(s2, 1, 0) + pltpu.roll(s2, n - 1, 0)
    s8 = pltpu.roll(s4, 2, 0) + pltpu.roll(s4, n - 2, 0)
    s16 = pltpu.roll(s8, 4, 0) + pltpu.roll(s8, n - 4, 0)
    cur = slice(POOL_HALO, POOL_HALO + tm)
    group = lax.broadcasted_iota(jnp.int32, (1, POOL_WIDTH), 1) // POOL_GROUP_DIM
    wsum = jnp.where(group == 0, s2[cur], jnp.where(group == 1, s4[cur], jnp.where(group == 2, s8[cur], s16[cur])))
    half = jnp.where(group == 0, POOL_WINDOWS[0] // 2,
                     jnp.where(group == 1, POOL_WINDOWS[1] // 2,
                               jnp.where(group == 2, POOL_WINDOWS[2] // 2, POOL_WINDOWS[3] // 2)))
    t = i * tm + lax.broadcasted_iota(jnp.int32, (tm, 1), 0)
    count = (jnp.clip(t + half, 0, seq_len) - jnp.clip(t - half, 0, seq_len)).astype(F32)
    pooled = (wsum / count - u[cur]).astype(BF16)
    b = jnp.dot(pooled, pw_ref[...], preferred_element_type=F32) * ps_ref[...]
    cat = jnp.concatenate([a_ref[0], b.astype(BF16), c_ref[0]], axis=1)
    y = jnp.dot(cat, w_ref[...], preferred_element_type=F32)
    x_mixed = x_ref[0] + mod_ref[0, 2:3, :] * _rms(y, g_ref[...])
    o_ref[0] = _swiglu_half_step(x_mixed, fmod_ref, fg_ref, w1_ref, w2_ref, act_ref, res_w)


def _outproj_ffn(x, mods, mod_row, norm_g, a, pin, c, pool_bd, pool_scale, w, w1, w2, layer, res_w):
    nb, n, _ = x.shape
    tm = min(TOKEN_TILE, n)
    halo_per_tile = tm // POOL_HALO
    n_halo = n // POOL_HALO
    tok = lambda width: pl.BlockSpec((1, tm, width), lambda b, i: (b, i, 0))
    return pl.pallas_call(
        functools.partial(_outproj_ffn_kernel, tm=tm, seq_len=n, res_w=res_w),
        out_shape=jax.ShapeDtypeStruct(x.shape, F32),
        grid=(nb, n // tm),
        in_specs=[tok(D_MODEL),
                  _mod_spec(layer, mod_row, 1),
                  _norm_spec(layer, 3, 1),
                  tok(NA_WIDTH),
                  pl.BlockSpec((1, POOL_HALO, POOL_WIDTH),
                               lambda b, i: (b, jnp.maximum(i * halo_per_tile - 1, 0), 0)),
                  tok(POOL_WIDTH),
                  pl.BlockSpec((1, POOL_HALO, POOL_WIDTH),
                               lambda b, i: (b, jnp.minimum((i + 1) * halo_per_tile, n_halo - 1), 0)),
                  tok(DIFF_WIDTH),
                  pl.BlockSpec((None, POOL_WIDTH, POOL_WIDTH), lambda b, i: (layer, 0, 0)),
                  pl.BlockSpec((None, 1, POOL_WIDTH), lambda b, i: (layer, 0, 0)),
                  _resident((None, MIX_WIDTH, D_MODEL), lambda b, i: (layer, 0, 0)),
                  _mod_spec(layer, mod_row, 2),
                  _norm_spec(layer, 4, 2),
                  _resident((None, None, D_MODEL, 2 * D_FF), lambda b, i: (layer, 1, 0, 0)),
                  _resident((None, None, D_FF, D_MODEL), lambda b, i: (layer, 1, 0, 0))],
        out_specs=tok(D_MODEL),
        scratch_shapes=[pltpu.VMEM((tm, D_FF), BF16)],
        compiler_params=_params("parallel", "parallel"),
        name="mixer_out_proj_ffn",
    )(x, mods, norm_g.reshape(DEPTH, 6, 1, D_MODEL), a, pin, pin, pin, c, pool_bd, pool_scale, w,
      mods, norm_g.reshape(DEPTH, 3, 2, D_MODEL), w1, w2)


def _rope_tables(n_tokens):
    t = np.arange(n_tokens)
    row = (t // GRID_W).astype(np.float32)
    col = (t % GRID_W).astype(np.float32)
    n_freq = DIFF_QK_DIM // 4
    inv_freq = np.power(np.float32(ROPE_THETA), -np.arange(n_freq, dtype=np.float32) / np.float32(n_freq))
    ang = np.concatenate([row[:, None] * inv_freq, col[:, None] * inv_freq], axis=-1).astype(np.float32)
    cos, sin = np.cos(ang), np.sin(ang)
    tables = (np.tile(cos, (1, 4)), np.concatenate([-sin, sin, -sin, sin], axis=-1), cos.T, sin.T)
    return tuple(jnp.asarray(np.ascontiguousarray(a), F32) for a in tables)


def _identity_rope_tables(n_tokens):
    half = DIFF_QK_DIM // 2
    return (jnp.ones((n_tokens, 2 * DIFF_QK_DIM), F32), jnp.zeros((n_tokens, 2 * DIFF_QK_DIM), F32),
            jnp.ones((half, n_tokens), F32), jnp.zeros((half, n_tokens), F32))


def _block_diag(w):
    n_l, n_g, cg, _ = w.shape
    eye = jnp.eye(n_g, dtype=w.dtype)
    return (eye[None, :, None, :, None] * w[:, :, :, None, :]).reshape(n_l, n_g * cg, n_g * cg)


def kernel(x, c, ctx, c_ctx, w_ada, b_ada, norm_g, ffn_w1, ffn_w2, w_in, w_out, na_rpb, pool_w, pool_scale,
           diff_lambda, diff_subln_g):
    nb, s, _ = x.shape
    n_ctx = ctx.shape[1]
    assert nb <= 7 and s % (GRID_W * NA_QROWS) == 0 and s % DIFF_TK == 0 and n_ctx % POOL_HALO == 0

    w1 = ffn_w1.astype(BF16)
    w2 = ffn_w2.astype(BF16)
    q_lo = 3 * NA_WIDTH + POOL_WIDTH
    k_lo = q_lo + DIFF_QK_WIDTH
    v_lo = k_lo + DIFF_QK_WIDTH
    w_in_b = w_in.astype(BF16)
    w_tok = w_in_b
    w_feat = jnp.swapaxes(jnp.concatenate([w_in_b[:, :, q_lo:k_lo], w_in_b[:, :, v_lo:], w_in_b[:, :, :NA_WIDTH],
                                           w_in_b[:, :, 2 * NA_WIDTH:3 * NA_WIDTH]], axis=2), 1, 2)
    w_out_b = w_out.astype(BF16)

    ctx_row = nb
    cvec = jnp.zeros((8, D_MODEL), F32).at[:nb].set(c).at[ctx_row].set(c_ctx)
    mods = _ada(cvec, w_ada, b_ada).reshape(DEPTH, 8, N_MOD // 3, 3, D_MODEL)
    lat_row = lambda b: b
    ctx_mod_row = lambda b: ctx_row

    rope_lat = _rope_tables(s)
    rope_ctx = _identity_rope_tables(n_ctx)
    bias = _na_bias_tables(na_rpb, s // GRID_W)
    pool_bd = _block_diag(pool_w).astype(BF16)
    pool_sc = pool_scale.reshape(DEPTH, 1, POOL_WIDTH)

    x_lat, x_ctx = x, ctx
    for layer in range(DEPTH):
        last = layer == DEPTH - 1
        lam_init = 0.8 - 0.6 * math.exp(-0.3 * layer)

        x_lat = _ffn(x_lat, mods, lat_row, norm_g, w1, w2, layer, 0, 0.5)
        x_ctx = _ffn(x_ctx, mods, ctx_mod_row, norm_g, w1, w2, layer, 0, 0.5)

        aq, ak, av, pin, dq, dk, dv = _inproj(x_lat, mods, lat_row, norm_g, w_tok, w_feat, layer, rope_lat)
        aqc, akc, avc, pinc, dqc, dkc, dvc = _inproj(x_ctx, mods, ctx_mod_row, norm_g, w_tok, w_feat, layer, rope_ctx)
        a_lat = _na(aq, ak, av, akc, avc, bias, layer)
        c_lat = _diff(dq, dk, dv, (dkc, dvc), diff_lambda, diff_subln_g, layer, lam_init)
        x_lat = _outproj_ffn(x_lat, mods, lat_row, norm_g, a_lat, pin, c_lat, pool_bd, pool_sc, w_out_b, w1, w2,
                             layer, 0.5)

        if not last:
            a_ctx = _ctx_attn(aqc, akc, avc)
            c_ctx_out = _diff(dqc, dkc, dvc, None, diff_lambda, diff_subln_g, layer, lam_init)
            x_ctx = _outproj_ffn(x_ctx, mods, ctx_mod_row, norm_g, a_ctx, pinc, c_ctx_out, pool_bd, pool_sc, w_out_b,
                                 w1, w2, layer, 0.5)
    return x_lat
```

```python
import functools
import math

import numpy as np
import jax
import jax.numpy as jnp
from jax import lax
from jax.experimental import pallas as pl
from jax.experimental.pallas import tpu as pltpu

D_MODEL = 1024
DEPTH = 2
GRID_W = 64
N_MOD = 9
D_FF = 2816
EPS = 1e-6
NEG_INF = -1e30
ROPE_THETA = 10000.0

NA_HEADS = 4
NA_HEAD_DIM = 64
NA_WIN_ROWS = 8
NA_WIN_COLS = 16
NA_WIDTH = NA_HEADS * NA_HEAD_DIM

POOL_WINDOWS = (2, 4, 8, 16)
POOL_GROUP_DIM = 64
POOL_WIDTH = len(POOL_WINDOWS) * POOL_GROUP_DIM
POOL_HALO = max(POOL_WINDOWS) // 2

DIFF_HEADS = 4
DIFF_QK_DIM = 64
DIFF_V_DIM = 2 * DIFF_QK_DIM
DIFF_QK_WIDTH = DIFF_HEADS * 2 * DIFF_QK_DIM
DIFF_WIDTH = DIFF_HEADS * DIFF_V_DIM
DIFF_Q_SCALE = DIFF_QK_DIM ** -0.5 * math.log2(math.e)
NA_Q_SCALE = NA_HEAD_DIM ** -0.5 * math.log2(math.e)

MIX_WIDTH = NA_WIDTH + POOL_WIDTH + DIFF_WIDTH
FEAT_ROWS = DIFF_QK_WIDTH + DIFF_WIDTH + 2 * NA_WIDTH
IN_WIDTH = 3 * NA_WIDTH + POOL_WIDTH + 2 * DIFF_QK_WIDTH + DIFF_WIDTH
MOD_ROWS = 8

BF16 = jnp.bfloat16
F32 = jnp.float32

V7X_VMEM_BYTES = 64 * 1024 * 1024
VMEM_LIMIT = V7X_VMEM_BYTES * 3 // 4

TOKEN_TILE = 512
FFN_TILE = 1024
FF_CHUNK = 256
ADA_COLS = 1152
NA_QROWS = 4
NA_KROWS = NA_QROWS + NA_WIN_ROWS
NA_ACC_ROWS = NA_HEAD_DIM + 16
DIFF_TQ = 512
DIFF_TK = 1024
DIFF_ACC_ROWS = DIFF_V_DIM + 16

_NT = (((1,), (1,)), ((), ()))


def _params(*sem):
    return pltpu.CompilerParams(dimension_semantics=sem, vmem_limit_bytes=VMEM_LIMIT)


def _resident(shape, index_map):
    return pl.BlockSpec(shape, index_map, pipeline_mode=pl.Buffered(1))


def _rms(x, g):
    return x * lax.rsqrt(jnp.mean(x * x, axis=-1, keepdims=True) + EPS) * g


def _modulate(x, mod_ref, g):
    shift, scale = mod_ref[0, 0:1, :], mod_ref[0, 1:2, :]
    return _rms(x, g) * (1.0 + scale) + shift


def _ada_kernel(c_ref, w_ref, b_ref, o_ref):
    cv = c_ref[...]
    o_ref[0] = jnp.dot(cv * jax.nn.sigmoid(cv), w_ref[0], preferred_element_type=F32) + b_ref[0]


def _ada(cvec, w_ada, b_ada):
    n_out = N_MOD * D_MODEL
    return pl.pallas_call(
        _ada_kernel,
        out_shape=jax.ShapeDtypeStruct((DEPTH, MOD_ROWS, n_out), F32),
        grid=(DEPTH, n_out // ADA_COLS),
        in_specs=[pl.BlockSpec((MOD_ROWS, D_MODEL), lambda l, j: (0, 0)),
                  pl.BlockSpec((1, D_MODEL, ADA_COLS), lambda l, j: (l, 0, j)),
                  pl.BlockSpec((1, 1, ADA_COLS), lambda l, j: (l, 0, j))],
        out_specs=pl.BlockSpec((1, MOD_ROWS, ADA_COLS), lambda l, j: (l, 0, j)),
        compiler_params=_params("parallel", "parallel"),
        name="ada_mod",
    )(cvec, w_ada, b_ada.reshape(DEPTH, 1, n_out))


def _swiglu_half_step(x, mod_ref, g_ref, w1_ref, w2_ref, act_ref, res_w):
    h = _modulate(x, mod_ref, g_ref[0:1, :]).astype(BF16)
    for c in range(D_FF // FF_CHUNK):
        lo = c * FF_CHUNK
        a = jnp.dot(h, w1_ref[:, lo:lo + FF_CHUNK], preferred_element_type=F32)
        b = jnp.dot(h, w1_ref[:, D_FF + lo:D_FF + lo + FF_CHUNK], preferred_element_type=F32)
        act_ref[:, lo:lo + FF_CHUNK] = (a * jax.nn.sigmoid(a) * b).astype(BF16)
    y = jnp.dot(act_ref[...], w2_ref[...], preferred_element_type=F32)
    return x + res_w * mod_ref[0, 2:3, :] * _rms(y, g_ref[1:2, :])


def _ffn_kernel(x_ref, mod_ref, g_ref, w1_ref, w2_ref, o_ref, act_ref, *, res_w):
    o_ref[0] = _swiglu_half_step(x_ref[0], mod_ref, g_ref, w1_ref, w2_ref, act_ref, res_w)


def _mod_spec(layer, mod_row, sub):
    return pl.BlockSpec((None, 1, None, 3, D_MODEL), lambda b, i: (layer, mod_row(b), sub, 0, 0))


def _norm_spec(layer, first, count):
    return pl.BlockSpec((None, None, count, D_MODEL), lambda b, i: (layer, first // count, 0, 0))


def _ffn(x, mods, mod_row, norm_g, w1, w2, layer, which, res_w):
    nb, n, _ = x.shape
    tm = min(FFN_TILE, n)
    return pl.pallas_call(
        functools.partial(_ffn_kernel, res_w=res_w),
        out_shape=jax.ShapeDtypeStruct(x.shape, F32),
        grid=(nb, n // tm),
        in_specs=[pl.BlockSpec((1, tm, D_MODEL), lambda b, i: (b, i, 0)),
                  _mod_spec(layer, mod_row, 2 * which),
                  _norm_spec(layer, 4 * which, 2),
                  _resident((None, None, D_MODEL, 2 * D_FF), lambda b, i: (layer, which, 0, 0)),
                  _resident((None, None, D_FF, D_MODEL), lambda b, i: (layer, which, 0, 0))],
        out_specs=pl.BlockSpec((1, tm, D_MODEL), lambda b, i: (b, i, 0)),
        scratch_shapes=[pltpu.VMEM((tm, D_FF), BF16)],
        compiler_params=_params("parallel", "parallel"),
        name="ffn",
    )(x, mods, norm_g.reshape(DEPTH, 3, 2, D_MODEL), w1, w2)


def _inproj_kernel(x_ref, mod_ref, g_ref, w_ref, wt_ref, cos_ref, sin_ref, cos_t_ref, sin_t_ref,
                   naqt_ref, nak_ref, navt_ref, pin_ref, dqt_ref, dk_ref, dvt_ref):
    x = x_ref[0]
    h = _modulate(x, mod_ref, g_ref[0:1, :]).astype(BF16)

    def proj(lo, width):
        return jnp.dot(h, w_ref[:, lo:lo + width], preferred_element_type=F32)

    nak_ref[0] = proj(NA_WIDTH, NA_WIDTH).astype(BF16)
    pin_ref[0] = proj(3 * NA_WIDTH, POOL_WIDTH)

    cos, sin = cos_ref[...], sin_ref[...]
    lane = lax.broadcasted_iota(jnp.int32, cos.shape, 1)
    half = DIFF_QK_DIM // 2
    first_half = (lane % DIFF_QK_DIM) < half
    head_w = 2 * DIFF_QK_DIM
    k = proj(3 * NA_WIDTH + POOL_WIDTH + DIFF_QK_WIDTH, DIFF_QK_WIDTH)
    for hd in range(DIFF_HEADS):
        t = k[:, hd * head_w:(hd + 1) * head_w]
        partner = jnp.where(first_half, pltpu.roll(t, head_w - half, 1), pltpu.roll(t, half, 1))
        dk_ref[0, :, hd * head_w:(hd + 1) * head_w] = (t * cos + partner * sin).astype(BF16)

    qv = lax.dot_general(wt_ref[...], h, _NT, preferred_element_type=F32)
    na_lo = DIFF_QK_WIDTH + DIFF_WIDTH
    naqt_ref[0] = (qv[na_lo:na_lo + NA_WIDTH] * NA_Q_SCALE).astype(BF16)
    navt_ref[0] = qv[na_lo + NA_WIDTH:].astype(BF16)
    cos_t, sin_t = cos_t_ref[...], sin_t_ref[...]
    for st in range(2 * DIFF_HEADS):
        base = st * DIFF_QK_DIM
        x1 = qv[base:base + half] * DIFF_Q_SCALE
        x2 = qv[base + half:base + DIFF_QK_DIM] * DIFF_Q_SCALE
        dqt_ref[0, base:base + half, :] = (x1 * cos_t - x2 * sin_t).astype(BF16)
        dqt_ref[0, base + half:base + DIFF_QK_DIM, :] = (x1 * sin_t + x2 * cos_t).astype(BF16)
    dvt_ref[0] = qv[DIFF_QK_WIDTH:na_lo].astype(BF16)


def _inproj(x, mods, mod_row, norm_g, w, wt, layer, rope):
    nb, n, _ = x.shape
    tm = min(TOKEN_TILE, n)
    cos, sin, cos_t, sin_t = rope
    n_tok_major = w.shape[2]
    tok = lambda width: pl.BlockSpec((1, tm, width), lambda b, i: (b, i, 0))
    feat = lambda width: pl.BlockSpec((1, width, tm), lambda b, i: (b, 0, i))
    out = lambda width, dt: jax.ShapeDtypeStruct((nb, n, width), dt)
    out_t = lambda width, dt: jax.ShapeDtypeStruct((nb, width, n), dt)
    return pl.pallas_call(
        _inproj_kernel,
        out_shape=(out_t(NA_WIDTH, BF16), out(NA_WIDTH, BF16), out_t(NA_WIDTH, BF16), out(POOL_WIDTH, F32),
                   out_t(DIFF_QK_WIDTH, BF16), out(DIFF_QK_WIDTH, BF16), out_t(DIFF_WIDTH, BF16)),
        grid=(nb, n // tm),
        in_specs=[tok(D_MODEL),
                  _mod_spec(layer, mod_row, 1),
                  _norm_spec(layer, 2, 1),
                  _resident((None, D_MODEL, n_tok_major), lambda b, i: (layer, 0, 0)),
                  _resident((None, FEAT_ROWS, D_MODEL), lambda b, i: (layer, 0, 0)),
                  pl.BlockSpec((tm, 2 * DIFF_QK_DIM), lambda b, i: (i, 0)),
                  pl.BlockSpec((tm, 2 * DIFF_QK_DIM), lambda b, i: (i, 0)),
                  pl.BlockSpec((DIFF_QK_DIM // 2, tm), lambda b, i: (0, i)),
                  pl.BlockSpec((DIFF_QK_DIM // 2, tm), lambda b, i: (0, i))],
        out_specs=(feat(NA_WIDTH), tok(NA_WIDTH), feat(NA_WIDTH), tok(POOL_WIDTH),
                   feat(DIFF_QK_WIDTH), tok(DIFF_QK_WIDTH), feat(DIFF_WIDTH)),
        compiler_params=_params("parallel", "parallel"),
        name="mixer_in_proj",
    )(x, mods, norm_g.reshape(DEPTH, 6, 1, D_MODEL), w, wt, cos, sin, cos_t, sin_t)


def _head_queries(qt, hd):
    pair = qt[(hd // 2) * 2 * NA_HEAD_DIM:(hd // 2 + 1) * 2 * NA_HEAD_DIM, :]
    row = lax.broadcasted_iota(jnp.int32, pair.shape, 0)
    own = (row >= (hd % 2) * NA_HEAD_DIM) & (row < (hd % 2 + 1) * NA_HEAD_DIM)
    return jnp.where(own, pair, jnp.zeros_like(pair))


def _head_pair_keys(k, hd):
    return k[:, (hd // 2) * 2 * NA_HEAD_DIM:(hd // 2 + 1) * 2 * NA_HEAD_DIM]


def _values_with_ones(vt):
    return jnp.concatenate([vt, jnp.ones((NA_ACC_ROWS - NA_HEAD_DIM, vt.shape[1]), BF16)], axis=0)


def _na_kernel(qt_ref, k_ref, vt_ref, kc_ref, vct_ref, bias_ref, o_ref, s_buf, m_buf, *, rows):
    i = pl.program_id(1)
    start = jnp.clip(i * NA_QROWS - NA_WIN_ROWS // 2, 0, rows - NA_KROWS) * GRID_W
    start = pl.multiple_of(start, 2 * GRID_W)
    n_keys = NA_KROWS * GRID_W
    qt = qt_ref[0]
    kb = k_ref[0, pl.ds(start, n_keys), :]
    kc = kc_ref[0]
    head = lambda hd: slice(hd * NA_HEAD_DIM, (hd + 1) * NA_HEAD_DIM)

    def stage_scores(hd):
        qh = _head_queries(qt, hd)
        s_loc = jnp.dot(_head_pair_keys(kb, hd), qh, preferred_element_type=F32) + bias_ref[0, hd]
        s_ctx = jnp.dot(_head_pair_keys(kc, hd), qh, preferred_element_type=F32)
        s_buf[hd % 2, 0:n_keys, :] = s_loc
        s_buf[hd % 2, n_keys:, :] = s_ctx
        m_buf[hd % 2] = jnp.maximum(jnp.max(s_loc, axis=0, keepdims=True), jnp.max(s_ctx, axis=0, keepdims=True))

    def finish(hd, p):
        o = (jnp.dot(_values_with_ones(vt_ref[0, head(hd), pl.ds(start, n_keys)]), p[0:n_keys],
                     preferred_element_type=F32)
             + jnp.dot(_values_with_ones(vct_ref[0, head(hd), :]), p[n_keys:], preferred_element_type=F32))
        return o[0:NA_HEAD_DIM] / o[NA_HEAD_DIM:NA_HEAD_DIM + 1]

    stage_scores(0)
    pending = None
    outs = []
    for hd in range(NA_HEADS):
        if hd + 1 < NA_HEADS:
            stage_scores(hd + 1)
        if pending is not None:
            outs.append(finish(*pending))
        pending = (hd, jnp.exp2(s_buf[hd % 2] - m_buf[hd % 2]).astype(BF16))
    outs.append(finish(*pending))
    o_ref[0] = jnp.concatenate(outs, axis=0).T.astype(BF16)


def _na(qt, k, vt, kc, vct, bias, layer):
    nb, s, _ = k.shape
    rows = s // GRID_W
    n_steps = rows // NA_QROWS
    tq = NA_QROWS * GRID_W
    n_ctx = kc.shape[1]

    def pattern(b, i):
        return (layer, jnp.where(i == 0, 0, jnp.where(i == n_steps - 1, 2, 1)), 0, 0, 0)

    return pl.pallas_call(
        functools.partial(_na_kernel, rows=rows),
        out_shape=jax.ShapeDtypeStruct((nb, s, NA_WIDTH), BF16),
        grid=(nb, n_steps),
        in_specs=[pl.BlockSpec((1, NA_WIDTH, tq), lambda b, i: (b, 0, i)),
                  _resident((1, s, NA_WIDTH), lambda b, i: (b, 0, 0)),
                  _resident((1, NA_WIDTH, s), lambda b, i: (b, 0, 0)),
                  pl.BlockSpec((1, n_ctx, NA_WIDTH), lambda b, i: (b, 0, 0)),
                  pl.BlockSpec((1, NA_WIDTH, n_ctx), lambda b, i: (b, 0, 0)),
                  pl.BlockSpec((None, 1, NA_HEADS, NA_KROWS * GRID_W, tq), pattern)],
        out_specs=pl.BlockSpec((1, tq, NA_WIDTH), lambda b, i: (b, i, 0)),
        scratch_shapes=[pltpu.VMEM((2, NA_KROWS * GRID_W + n_ctx, tq), F32),
                        pltpu.VMEM((2, 1, tq), F32)],
        compiler_params=_params("parallel", "arbitrary"),
        name="neighbourhood_attn",
    )(qt, k, vt, kc, vct, bias)


def _na_bias_tables(rpb, rows):
    n_l, n_h, n_dr, _ = rpb.shape
    pad = GRID_W - NA_WIN_COLS
    w = 2 * GRID_W
    ext = jnp.pad(rpb.astype(F32) * math.log2(math.e), ((0, 0), (0, 0), (0, 0), (pad, pad + 1)), mode="edge")
    skew = jnp.broadcast_to(ext[..., None, :], (n_l, n_h, n_dr, GRID_W, w)).reshape(n_l, n_h, n_dr, GRID_W * w)
    skew = skew[..., :GRID_W * (w - 1)].reshape(n_l, n_h, n_dr, GRID_W, w - 1)
    toe = skew[..., GRID_W - 1:]
    qc = np.arange(GRID_W)[:, None]
    kc = np.arange(GRID_W)[None, :]
    win_c0 = np.clip(qc - NA_WIN_COLS // 2, 0, GRID_W - NA_WIN_COLS)
    in_cols = (kc >= win_c0) & (kc < win_c0 + NA_WIN_COLS)
    toe = jnp.where(jnp.asarray(in_cols), toe, NEG_INF)
    toe = jnp.swapaxes(toe, 3, 4)
    toe = jnp.pad(toe, ((0, 0), (0, 0), (NA_KROWS, NA_KROWS), (0, 0), (0, 0)), constant_values=NEG_INF)
    pair = jnp.concatenate([toe[:, :, 1:], toe[:, :, :-1]], axis=-1)
    assert NA_QROWS == 4
    tables = []
    for r0 in (0, NA_QROWS, rows - NA_QROWS):
        start = int(np.clip(r0 - NA_WIN_ROWS // 2, 0, rows - NA_KROWS))
        c = start - r0 + NA_WIN_ROWS - 1 + NA_KROWS
        halves = [pair[:, :, c - 1 - qr:c - 1 - qr + NA_KROWS].reshape(n_l, n_h, NA_KROWS * GRID_W, 2 * GRID_W)
                  for qr in (0, 2)]
        valid = np.zeros((NA_KROWS, NA_QROWS), bool)
        for qr in range(NA_QROWS):
            ws = int(np.clip(r0 + qr - NA_WIN_ROWS // 2, 0, rows - NA_WIN_ROWS))
            valid[:, qr] = [ws <= start + kl < ws + NA_WIN_ROWS for kl in range(NA_KROWS)]
        valid = np.repeat(np.repeat(valid, GRID_W, axis=0), GRID_W, axis=1)
        tables.append(jnp.where(jnp.asarray(valid), jnp.concatenate(halves, axis=-1), NEG_INF))
    return jnp.stack(tables, axis=1)


def _diff_kernel(*refs, has_ctx, lam_init, n_kv, tk):
    if has_ctx:
        qt_ref, k_ref, vt_ref, kc_ref, vct_ref, lam_ref, g_ref, o_ref = refs[:8]
    else:
        qt_ref, k_ref, vt_ref, lam_ref, g_ref, o_ref = refs[:6]
    qpad_ref, m_ref, acc_ref, s_buf, mblk_ref, p_tail_ref, alpha_tail_ref = refs[-7:]
    head_w = 2 * DIFF_QK_DIM
    n_stream = 2 * DIFF_HEADS
    last = n_stream - 1


    def stage_scores(idx, kh):
        n = kh.shape[0]
        s = jnp.dot(kh, qpad_ref[idx], preferred_element_type=F32)
        s_buf[idx % 2, 0:n, :] = s
        mblk_ref[idx % 2:idx % 2 + 1, :] = jnp.max(s, axis=0, keepdims=True)

    def softmax_staged(idx, n):
        slot = idx % 2
        m_old = m_ref[idx:idx + 1, :]
        m_new = jnp.maximum(m_old, mblk_ref[slot:slot + 1, :])
        alpha = jnp.exp2(m_old - m_new)
        m_ref[idx:idx + 1, :] = m_new
        return alpha, jnp.exp2(s_buf[slot, 0:n, :] - m_new).astype(BF16)

    def accumulate(idx, alpha, p, vth):
        v_ext = jnp.concatenate([vth, jnp.ones((DIFF_ACC_ROWS - DIFF_V_DIM, vth.shape[1]), BF16)], axis=0)
        acc_ref[idx] = alpha * acc_ref[idx] + jnp.dot(v_ext, p, preferred_element_type=F32)

    def single_block(kblk, vtblk):
        k_head = lambda idx: kblk[:, (idx // 2) * head_w:(idx // 2 + 1) * head_w]
        v_head = lambda idx: vtblk[(idx // 2) * DIFF_V_DIM:(idx // 2 + 1) * DIFF_V_DIM, :]
        stage_scores(0, k_head(0))
        pending = None
        for idx in range(n_stream):
            if idx < last:
                stage_scores(idx + 1, k_head(idx + 1))
            if pending is not None:
                accumulate(*pending)
            alpha, p = softmax_staged(idx, kblk.shape[0])
            pending = (idx, alpha, p, v_head(idx))
        accumulate(*pending)

    def k_head_at(j, idx):
        return k_ref[0, pl.ds(pl.multiple_of(j * tk, tk), tk), (idx // 2) * head_w:(idx // 2 + 1) * head_w]

    def v_head_at(j, idx):
        return vt_ref[0, (idx // 2) * DIFF_V_DIM:(idx // 2 + 1) * DIFF_V_DIM, pl.ds(pl.multiple_of(j * tk, tk), tk)]

    def pipelined_block(j, carry):
        pending = (last, alpha_tail_ref[...], p_tail_ref[...], v_head_at(jnp.maximum(j - 1, 0), last))
        for idx in range(n_stream):
            if idx < last:
                stage_scores(idx + 1, k_head_at(j, idx + 1))
            else:
                stage_scores(0, k_head_at(jnp.minimum(j + 1, n_kv - 1), 0))
            accumulate(*pending)
            alpha, p = softmax_staged(idx, tk)
            pending = (idx, alpha, p, v_head_at(j, idx))
        alpha_tail_ref[...] = pending[1]
        p_tail_ref[...] = pending[2]
        return carry

    row = lax.broadcasted_iota(jnp.int32, (head_w, qt_ref.shape[2]), 0)
    for hd in range(DIFF_HEADS):
        qh = qt_ref[0, hd * head_w:(hd + 1) * head_w, :]
        qpad_ref[2 * hd] = jnp.where(row < DIFF_QK_DIM, qh, jnp.zeros_like(qh))
        qpad_ref[2 * hd + 1] = jnp.where(row >= DIFF_QK_DIM, qh, jnp.zeros_like(qh))
    m_ref[...] = jnp.full(m_ref.shape, -jnp.inf, F32)
    acc_ref[...] = jnp.zeros(acc_ref.shape, F32)
    if has_ctx:
        single_block(kc_ref[0], vct_ref[0])

    if n_kv == 1:
        single_block(k_ref[0], vt_ref[0])
    else:
        stage_scores(0, k_head_at(0, 0))
        alpha_tail_ref[...] = jnp.ones(alpha_tail_ref.shape, F32)
        p_tail_ref[...] = jnp.zeros(p_tail_ref.shape, BF16)
        lax.fori_loop(0, n_kv, pipelined_block, 0)
        accumulate(last, alpha_tail_ref[...], p_tail_ref[...], v_head_at(n_kv - 1, last))

    lf = lam_ref[...]
    lam = (jnp.exp(jnp.sum(lf[0:1] * lf[1:2], axis=1, keepdims=True))
           - jnp.exp(jnp.sum(lf[2:3] * lf[3:4], axis=1, keepdims=True)) + lam_init)
    for hd in range(DIFF_HEADS):
        o1 = acc_ref[2 * hd, 0:DIFF_V_DIM, :] / acc_ref[2 * hd, DIFF_V_DIM:DIFF_V_DIM + 1, :]
        o2 = acc_ref[2 * hd + 1, 0:DIFF_V_DIM, :] / acc_ref[2 * hd + 1, DIFF_V_DIM:DIFF_V_DIM + 1, :]
        o = o1 - lam * o2
        o = o * lax.rsqrt(jnp.mean(o * o, axis=0, keepdims=True) + EPS) * g_ref[...] * (1.0 - lam_init)
        o_ref[0, :, hd * DIFF_V_DIM:(hd + 1) * DIFF_V_DIM] = o.T.astype(BF16)


def _diff(qt, k, vt, ctx_kv, lam_vec, subln_g, layer, lam_init):
    nb, _, n = qt.shape
    n_keys = k.shape[1]
    tq = min(DIFF_TQ, n)
    tk = min(DIFF_TK, n_keys)
    n_kv = n_keys // tk
    has_ctx = ctx_kv is not None
    in_specs = [pl.BlockSpec((1, DIFF_QK_WIDTH, tq), lambda b, i: (b, 0, i)),
                _resident((1, n_keys, DIFF_QK_WIDTH), lambda b, i: (b, 0, 0)),
                _resident((1, DIFF_WIDTH, n_keys), lambda b, i: (b, 0, 0))]
    args = [qt, k, vt]
    if has_ctx:
        n_ctx = ctx_kv[0].shape[1]
        in_specs += [pl.BlockSpec((1, n_ctx, DIFF_QK_WIDTH), lambda b, i: (b, 0, 0)),
                     pl.BlockSpec((1, DIFF_WIDTH, n_ctx), lambda b, i: (b, 0, 0))]
        args += list(ctx_kv)
    in_specs += [pl.BlockSpec((None, 4, DIFF_QK_DIM), lambda b, i: (layer, 0, 0)),
                 pl.BlockSpec((None, DIFF_V_DIM, 1), lambda b, i: (layer, 0, 0))]
    args += [lam_vec, subln_g.reshape(DEPTH, DIFF_V_DIM, 1)]
    n_stream = 2 * DIFF_HEADS
    return pl.pallas_call(
        functools.partial(_diff_kernel, has_ctx=has_ctx, lam_init=lam_init, n_kv=n_kv, tk=tk),
        out_shape=jax.ShapeDtypeStruct((nb, n, DIFF_WIDTH), BF16),
        grid=(nb, n // tq),
        in_specs=in_specs,
        out_specs=pl.BlockSpec((1, tq, DIFF_WIDTH), lambda b, i: (b, i, 0)),
        scratch_shapes=[pltpu.VMEM((n_stream, 2 * DIFF_QK_DIM, tq), BF16),
                        pltpu.VMEM((n_stream, tq), F32),
                        pltpu.VMEM((n_stream, DIFF_ACC_ROWS, tq), F32),
                        pltpu.VMEM((2, tk, tq), F32),
                        pltpu.VMEM((2, tq), F32),
                        pltpu.VMEM((tk, tq), BF16),
                        pltpu.VMEM((1, tq), F32)],
        compiler_params=_params("parallel", "arbitrary"),
        name="diff_attn",
    )(*args)


def _ctx_attn_kernel(qt_ref, k_ref, vt_ref, o_ref):
    qt, k = qt_ref[0], k_ref[0]
    outs = []
    for hd in range(NA_HEADS):
        s = jnp.dot(_head_pair_keys(k, hd), _head_queries(qt, hd), preferred_element_type=F32)
        p = jnp.exp2(s - jnp.max(s, axis=0, keepdims=True)).astype(BF16)
        o = jnp.dot(_values_with_ones(vt_ref[0, hd * NA_HEAD_DIM:(hd + 1) * NA_HEAD_DIM, :]), p,
                    preferred_element_type=F32)
        outs.append(o[0:NA_HEAD_DIM] / o[NA_HEAD_DIM:NA_HEAD_DIM + 1])
    o_ref[0] = jnp.concatenate(outs, axis=0).T.astype(BF16)


def _ctx_attn(qt, k, vt):
    nb, n, w = k.shape
    tok = pl.BlockSpec((1, n, w), lambda b: (b, 0, 0))
    feat = pl.BlockSpec((1, w, n), lambda b: (b, 0, 0))
    return pl.pallas_call(
        _ctx_attn_kernel,
        out_shape=jax.ShapeDtypeStruct(k.shape, BF16),
        grid=(nb,),
        in_specs=[feat, tok, feat],
        out_specs=tok,
        compiler_params=_params("parallel"),
        name="ctx_dense_attn",
    )(qt, k, vt)


def _outproj_ffn_kernel(x_ref, mod_ref, g_ref, a_ref, pprev_ref, pcur_ref, pnext_ref, c_ref,
                        pw_ref, ps_ref, w_ref, fmod_ref, fg_ref, w1_ref, w2_ref, o_ref, act_ref,
                        *, tm, seq_len, res_w):
    i = pl.program_id(1)
    n = tm + 2 * POOL_HALO
    u = jnp.concatenate([pprev_ref[0], pcur_ref[0], pnext_ref[0]], axis=0)
    t_ext = i * tm - POOL_HALO + lax.broadcasted_iota(jnp.int32, (n, 1), 0)
    u = jnp.where((t_ext >= 0) & (t_ext < seq_len), u, 0.0)
    s2 = u + pltpu.roll(u, 1, 0)
    s4 = pltpu.roll(s2, 1, 0) + pltpu.roll(s2, n - 1, 0)
    s8 = pltpu.roll(s4, 2, 0) + pltpu.roll(s4, n - 2, 0)
    s16 = pltpu.roll(s8, 4, 0) + pltpu.roll(s8, n - 4, 0)
    cur = slice(POOL_HALO, POOL_HALO + tm)
    group = lax.broadcasted_iota(jnp.int32, (1, POOL_WIDTH), 1) // POOL_GROUP_DIM
    wsum = jnp.where(group == 0, s2[cur], jnp.where(group == 1, s4[cur], jnp.where(group == 2, s8[cur], s16[cur])))
    half = jnp.where(group == 0, POOL_WINDOWS[0] // 2,
                     jnp.where(group == 1, POOL_WINDOWS[1] // 2,
                               jnp.where(group == 2, POOL_WINDOWS[2] // 2, POOL_WINDOWS[3] // 2)))
    t = i * tm + lax.broadcasted_iota(jnp.int32, (tm, 1), 0)
    count = (jnp.clip(t + half, 0, seq_len) - jnp.clip(t - half, 0, seq_len)).astype(F32)
    pooled = (wsum / count - u[cur]).astype(BF16)
    b = jnp.dot(pooled, pw_ref[...], preferred_element_type=F32) * ps_ref[...]
    cat = jnp.concatenate([a_ref[0], b.astype(BF16), c_ref[0]], axis=1)
    y = jnp.dot(cat, w_ref[...], preferred_element_type=F32)
    x_mixed = x_ref[0] + mod_ref[0, 2:3, :] * _rms(y, g_ref[...])
    o_ref[0] = _swiglu_half_step(x_mixed, fmod_ref, fg_ref, w1_ref, w2_ref, act_ref, res_w)


def _outproj_ffn(x, mods, mod_row, norm_g, a, pin, c, pool_bd, pool_scale, w, w1, w2, layer, res_w):
    nb, n, _ = x.shape
    tm = min(TOKEN_TILE, n)
    halo_per_tile = tm // POOL_HALO
    n_halo = n // POOL_HALO
    tok = lambda width: pl.BlockSpec((1, tm, width), lambda b, i: (b, i, 0))
    return pl.pallas_call(
        functools.partial(_outproj_ffn_kernel, tm=tm, seq_len=n, res_w=res_w),
        out_shape=jax.ShapeDtypeStruct(x.shape, F32),
        grid=(nb, n // tm),
        in_specs=[tok(D_MODEL),
                  _mod_spec(layer, mod_row, 1),
                  _norm_spec(layer, 3, 1),
                  tok(NA_WIDTH),
                  pl.BlockSpec((1, POOL_HALO, POOL_WIDTH),
                               lambda b, i: (b, jnp.maximum(i * halo_per_tile - 1, 0), 0)),
                  tok(POOL_WIDTH),
                  pl.BlockSpec((1, POOL_HALO, POOL_WIDTH),
                               lambda b, i: (b, jnp.minimum((i + 1) * halo_per_tile, n_halo - 1), 0)),
                  tok(DIFF_WIDTH),
                  pl.BlockSpec((None, POOL_WIDTH, POOL_WIDTH), lambda b, i: (layer, 0, 0)),
                  pl.BlockSpec((None, 1, POOL_WIDTH), lambda b, i: (layer, 0, 0)),
                  _resident((None, MIX_WIDTH, D_MODEL), lambda b, i: (layer, 0, 0)),
                  _mod_spec(layer, mod_row, 2),
                  _norm_spec(layer, 4, 2),
                  _resident((None, None, D_MODEL, 2 * D_FF), lambda b, i: (layer, 1, 0, 0)),
                  _resident((None, None, D_FF, D_MODEL), lambda b, i: (layer, 1, 0, 0))],
        out_specs=tok(D_MODEL),
        scratch_shapes=[pltpu.VMEM((tm, D_FF), BF16)],
        compiler_params=_params("parallel", "parallel"),
        name="mixer_out_proj_ffn",
    )(x, mods, norm_g.reshape(DEPTH, 6, 1, D_MODEL), a, pin, pin, pin, c, pool_bd, pool_scale, w,
      mods, norm_g.reshape(DEPTH, 3, 2, D_MODEL), w1, w2)


def _rope_tables(n_tokens):
    t = np.arange(n_tokens)
    row = (t // GRID_W).astype(np.float32)
    col = (t % GRID_W).astype(np.float32)
    n_freq = DIFF_QK_DIM // 4
    inv_freq = np.power(np.float32(ROPE_THETA), -np.arange(n_freq, dtype=np.float32) / np.float32(n_freq))
    ang = np.concatenate([row[:, None] * inv_freq, col[:, None] * inv_freq], axis=-1).astype(np.float32)
    cos, sin = np.cos(ang), np.sin(ang)
    tables = (np.tile(cos, (1, 4)), np.concatenate([-sin, sin, -sin, sin], axis=-1), cos.T, sin.T)
    return tuple(jnp.asarray(np.ascontiguousarray(a), F32) for a in tables)


def _identity_rope_tables(n_tokens):
    half = DIFF_QK_DIM // 2
    return (jnp.ones((n_tokens, 2 * DIFF_QK_DIM), F32), jnp.zeros((n_tokens, 2 * DIFF_QK_DIM), F32),
            jnp.ones((half, n_tokens), F32), jnp.zeros((half, n_tokens), F32))


def _block_diag(w):
    n_l, n_g, cg, _ = w.shape
    eye = jnp.eye(n_g, dtype=w.dtype)
    return (eye[None, :, None, :, None] * w[:, :, :, None, :]).reshape(n_l, n_g * cg, n_g * cg)


def kernel(x, c, ctx, c_ctx, w_ada, b_ada, norm_g, ffn_w1, ffn_w2, w_in, w_out, na_rpb, pool_w, pool_scale,
           diff_lambda, diff_subln_g):
    nb, s, _ = x.shape
    n_ctx = ctx.shape[1]
    assert nb < MOD_ROWS and s % (GRID_W * NA_QROWS) == 0 and s % DIFF_TK == 0 and n_ctx % POOL_HALO == 0
    assert n_ctx <= DIFF_TK

    w1 = ffn_w1.astype(BF16)
    w2 = ffn_w2.astype(BF16)
    q_lo = 3 * NA_WIDTH + POOL_WIDTH
    k_lo = q_lo + DIFF_QK_WIDTH
    v_lo = k_lo + DIFF_QK_WIDTH
    w_in_b = w_in.astype(BF16)
    w_tok = w_in_b
    w_feat = jnp.swapaxes(jnp.concatenate([w_in_b[:, :, q_lo:k_lo], w_in_b[:, :, v_lo:], w_in_b[:, :, :NA_WIDTH],
                                           w_in_b[:, :, 2 * NA_WIDTH:3 * NA_WIDTH]], axis=2), 1, 2)
    w_out_b = w_out.astype(BF16)

    ctx_row = nb
    cvec = jnp.zeros((MOD_ROWS, D_MODEL), F32).at[:nb].set(c).at[ctx_row].set(c_ctx)
    mods = _ada(cvec, w_ada, b_ada).reshape(DEPTH, MOD_ROWS, N_MOD // 3, 3, D_MODEL)
    lat_row = lambda b: b
    ctx_mod_row = lambda b: ctx_row

    rope_lat = _rope_tables(s)
    rope_ctx = _identity_rope_tables(n_ctx)
    bias = _na_bias_tables(na_rpb, s // GRID_W)
    pool_bd = _block_diag(pool_w).astype(BF16)
    pool_sc = pool_scale.reshape(DEPTH, 1, POOL_WIDTH)

    x_lat, x_ctx = x, ctx
    for layer in range(DEPTH):
        last = layer == DEPTH - 1
        lam_init = 0.8 - 0.6 * math.exp(-0.3 * layer)

        x_lat = _ffn(x_lat, mods, lat_row, norm_g, w1, w2, layer, 0, 0.5)
        x_ctx = _ffn(x_ctx, mods, ctx_mod_row, norm_g, w1, w2, layer, 0, 0.5)

        aq, ak, av, pin, dq, dk, dv = _inproj(x_lat, mods, lat_row, norm_g, w_tok, w_feat, layer, rope_lat)
        aqc, akc, avc, pinc, dqc, dkc, dvc = _inproj(x_ctx, mods, ctx_mod_row, norm_g, w_tok, w_feat, layer, rope_ctx)
        a_lat = _na(aq, ak, av, akc, avc, bias, layer)
        c_lat = _diff(dq, dk, dv, (dkc, dvc), diff_lambda, diff_subln_g, layer, lam_init)
        x_lat = _outproj_ffn(x_lat, mods, lat_row, norm_g, a_lat, pin, c_lat, pool_bd, pool_sc, w_out_b, w1, w2,
                             layer, 0.5)

        if not last:
            a_ctx = _ctx_attn(aqc, akc, avc)
            c_ctx_out = _diff(dqc, dkc, dvc, None, diff_lambda, diff_subln_g, layer, lam_init)
            x_ctx = _outproj_ffn(x_ctx, mods, ctx_mod_row, norm_g, a_ctx, pinc, c_ctx_out, pool_bd, pool_sc, w_out_b,
                                 w1, w2, layer, 0.5)
    return x_lat
```

```python
import functools
import math

import numpy as np
import jax
import jax.numpy as jnp
from jax import lax
from jax.experimental import pallas as pl
from jax.experimental.pallas import tpu as pltpu

D_MODEL = 1024
DEPTH = 2
GRID_W = 64
N_MOD = 9
D_FF = 2816
EPS = 1e-6
NEG_INF = -1e30
ROPE_THETA = 10000.0

NA_HEADS = 4
NA_HEAD_DIM = 64
NA_WIN_ROWS = 8
NA_WIN_COLS = 16
NA_WIDTH = NA_HEADS * NA_HEAD_DIM

POOL_WINDOWS = (2, 4, 8, 16)
POOL_GROUP_DIM = 64
POOL_WIDTH = len(POOL_WINDOWS) * POOL_GROUP_DIM
POOL_HALO = max(POOL_WINDOWS) // 2

DIFF_HEADS = 4
DIFF_QK_DIM = 64
DIFF_V_DIM = 2 * DIFF_QK_DIM
DIFF_QK_WIDTH = DIFF_HEADS * 2 * DIFF_QK_DIM
DIFF_WIDTH = DIFF_HEADS * DIFF_V_DIM
DIFF_Q_SCALE = DIFF_QK_DIM ** -0.5 * math.log2(math.e)
NA_Q_SCALE = NA_HEAD_DIM ** -0.5 * math.log2(math.e)

MIX_WIDTH = NA_WIDTH + POOL_WIDTH + DIFF_WIDTH
FEAT_ROWS = DIFF_QK_WIDTH + DIFF_WIDTH + 2 * NA_WIDTH
IN_WIDTH = 3 * NA_WIDTH + POOL_WIDTH + 2 * DIFF_QK_WIDTH + DIFF_WIDTH
MOD_ROWS = 8

BF16 = jnp.bfloat16
F32 = jnp.float32

V7X_VMEM_BYTES = 64 * 1024 * 1024
VMEM_LIMIT = V7X_VMEM_BYTES * 3 // 4

TOKEN_TILE = 512
FFN_TILE = 1024
FF_CHUNK = 256
ADA_COLS = 1152
NA_QROWS = 4
NA_KROWS = NA_QROWS + NA_WIN_ROWS
NA_ACC_ROWS = NA_HEAD_DIM + 16
DIFF_TQ = 512
DIFF_TK = 1024
DIFF_ACC_ROWS = DIFF_V_DIM + 16

_NT = (((1,), (1,)), ((), ()))


def _params(*sem):
    return pltpu.CompilerParams(dimension_semantics=sem, vmem_limit_bytes=VMEM_LIMIT)


def _resident(shape, index_map):
    return pl.BlockSpec(shape, index_map, pipeline_mode=pl.Buffered(1))


def _rms(x, g):
    return x * lax.rsqrt(jnp.mean(x * x, axis=-1, keepdims=True) + EPS) * g


def _modulate(x, mod_ref, g):
    shift, scale = mod_ref[0, 0:1, :], mod_ref[0, 1:2, :]
    return _rms(x, g) * (1.0 + scale) + shift


def _ada_kernel(c_ref, w_ref, b_ref, o_ref):
    cv = c_ref[...]
    o_ref[0] = jnp.dot(cv * jax.nn.sigmoid(cv), w_ref[0], preferred_element_type=F32) + b_ref[0]


def _ada(cvec, w_ada, b_ada):
    n_out = N_MOD * D_MODEL
    return pl.pallas_call(
        _ada_kernel,
        out_shape=jax.ShapeDtypeStruct((DEPTH, MOD_ROWS, n_out), F32),
        grid=(DEPTH, n_out // ADA_COLS),
        in_specs=[pl.BlockSpec((MOD_ROWS, D_MODEL), lambda l, j: (0, 0)),
                  pl.BlockSpec((1, D_MODEL, ADA_COLS), lambda l, j: (l, 0, j)),
                  pl.BlockSpec((1, 1, ADA_COLS), lambda l, j: (l, 0, j))],
        out_specs=pl.BlockSpec((1, MOD_ROWS, ADA_COLS), lambda l, j: (l, 0, j)),
        compiler_params=_params("parallel", "parallel"),
        name="ada_mod",
    )(cvec, w_ada, b_ada.reshape(DEPTH, 1, n_out))


def _swiglu_half_step(x, mod_ref, g_ref, w1_ref, w2_ref, act_ref, res_w):
    h = _modulate(x, mod_ref, g_ref[0:1, :]).astype(BF16)
    for c in range(D_FF // FF_CHUNK):
        lo = c * FF_CHUNK
        a = jnp.dot(h, w1_ref[:, lo:lo + FF_CHUNK], preferred_element_type=F32)
        b = jnp.dot(h, w1_ref[:, D_FF + lo:D_FF + lo + FF_CHUNK], preferred_element_type=F32)
        act_ref[:, lo:lo + FF_CHUNK] = (a * jax.nn.sigmoid(a) * b).astype(BF16)
    y = jnp.dot(act_ref[...], w2_ref[...], preferred_element_type=F32)
    return x + res_w * mod_ref[0, 2:3, :] * _rms(y, g_ref[1:2, :])


def _ffn_kernel(x_ref, mod_ref, g_ref, w1_ref, w2_ref, o_ref, act_ref, *, res_w):
    o_ref[0] = _swiglu_half_step(x_ref[0], mod_ref, g_ref, w1_ref, w2_ref, act_ref, res_w)


def _mod_spec(layer, mod_row, sub):
    return pl.BlockSpec((None, 1, None, 3, D_MODEL), lambda b, i: (layer, mod_row(b), sub, 0, 0))


def _norm_spec(layer, first, count):
    return pl.BlockSpec((None, None, count, D_MODEL), lambda b, i: (layer, first // count, 0, 0))


def _ffn(x, mods, mod_row, norm_g, w1, w2, layer, which, res_w):
    nb, n, _ = x.shape
    tm = min(FFN_TILE, n)
    return pl.pallas_call(
        functools.partial(_ffn_kernel, res_w=res_w),
        out_shape=jax.ShapeDtypeStruct(x.shape, F32),
        grid=(nb, n // tm),
        in_specs=[pl.BlockSpec((1, tm, D_MODEL), lambda b, i: (b, i, 0)),
                  _mod_spec(layer, mod_row, 2 * which),
                  _norm_spec(layer, 4 * which, 2),
                  _resident((None, None, D_MODEL, 2 * D_FF), lambda b, i: (layer, which, 0, 0)),
                  _resident((None, None, D_FF, D_MODEL), lambda b, i: (layer, which, 0, 0))],
        out_specs=pl.BlockSpec((1, tm, D_MODEL), lambda b, i: (b, i, 0)),
        scratch_shapes=[pltpu.VMEM((tm, D_FF), BF16)],
        compiler_params=_params("parallel", "parallel"),
        name="ffn",
    )(x, mods, norm_g.reshape(DEPTH, 3, 2, D_MODEL), w1, w2)


def _inproj_kernel(x_ref, mod_ref, g_ref, w_ref, wt_ref, cos_ref, sin_ref, cos_t_ref, sin_t_ref,
                   naqt_ref, nak_ref, navt_ref, pin_ref, dqt_ref, dk_ref, dvt_ref):
    x = x_ref[0]
    h = _modulate(x, mod_ref, g_ref[0:1, :]).astype(BF16)

    def proj(lo, width):
        return jnp.dot(h, w_ref[:, lo:lo + width], preferred_element_type=F32)

    nak_ref[0] = proj(NA_WIDTH, NA_WIDTH).astype(BF16)
    pin_ref[0] = proj(3 * NA_WIDTH, POOL_WIDTH)

    cos, sin = cos_ref[...], sin_ref[...]
    lane = lax.broadcasted_iota(jnp.int32, cos.shape, 1)
    half = DIFF_QK_DIM // 2
    first_half = (lane % DIFF_QK_DIM) < half
    head_w = 2 * DIFF_QK_DIM
    k = proj(3 * NA_WIDTH + POOL_WIDTH + DIFF_QK_WIDTH, DIFF_QK_WIDTH)
    for hd in range(DIFF_HEADS):
        t = k[:, hd * head_w:(hd + 1) * head_w]
        partner = jnp.where(first_half, pltpu.roll(t, head_w - half, 1), pltpu.roll(t, half, 1))
        dk_ref[0, :, hd * head_w:(hd + 1) * head_w] = (t * cos + partner * sin).astype(BF16)

    qv = lax.dot_general(wt_ref[...], h, _NT, preferred_element_type=F32)
    na_lo = DIFF_QK_WIDTH + DIFF_WIDTH
    naqt_ref[0] = (qv[na_lo:na_lo + NA_WIDTH] * NA_Q_SCALE).astype(BF16)
    navt_ref[0] = qv[na_lo + NA_WIDTH:].astype(BF16)
    cos_t, sin_t = cos_t_ref[...], sin_t_ref[...]
    for st in range(2 * DIFF_HEADS):
        base = st * DIFF_QK_DIM
        x1 = qv[base:base + half] * DIFF_Q_SCALE
        x2 = qv[base + half:base + DIFF_QK_DIM] * DIFF_Q_SCALE
        dqt_ref[0, base:base + half, :] = (x1 * cos_t - x2 * sin_t).astype(BF16)
        dqt_ref[0, base + half:base + DIFF_QK_DIM, :] = (x1 * sin_t + x2 * cos_t).astype(BF16)
    dvt_ref[0] = qv[DIFF_QK_WIDTH:na_lo].astype(BF16)


def _inproj(x, mods, mod_row, norm_g, w, wt, layer, rope):
    nb, n, _ = x.shape
    tm = min(FFN_TILE, n)
    cos, sin, cos_t, sin_t = rope
    n_tok_major = w.shape[2]
    tok = lambda width: pl.BlockSpec((1, tm, width), lambda b, i: (b, i, 0))
    feat = lambda width: pl.BlockSpec((1, width, tm), lambda b, i: (b, 0, i))
    out = lambda width, dt: jax.ShapeDtypeStruct((nb, n, width), dt)
    out_t = lambda width, dt: jax.ShapeDtypeStruct((nb, width, n), dt)
    return pl.pallas_call(
        _inproj_kernel,
        out_shape=(out_t(NA_WIDTH, BF16), out(NA_WIDTH, BF16), out_t(NA_WIDTH, BF16), out(POOL_WIDTH, F32),
                   out_t(DIFF_QK_WIDTH, BF16), out(DIFF_QK_WIDTH, BF16), out_t(DIFF_WIDTH, BF16)),
        grid=(nb, n // tm),
        in_specs=[tok(D_MODEL),
                  _mod_spec(layer, mod_row, 1),
                  _norm_spec(layer, 2, 1),
                  _resident((None, D_MODEL, n_tok_major), lambda b, i: (layer, 0, 0)),
                  _resident((None, FEAT_ROWS, D_MODEL), lambda b, i: (layer, 0, 0)),
                  pl.BlockSpec((tm, 2 * DIFF_QK_DIM), lambda b, i: (i, 0)),
                  pl.BlockSpec((tm, 2 * DIFF_QK_DIM), lambda b, i: (i, 0)),
                  pl.BlockSpec((DIFF_QK_DIM // 2, tm), lambda b, i: (0, i)),
                  pl.BlockSpec((DIFF_QK_DIM // 2, tm), lambda b, i: (0, i))],
        out_specs=(feat(NA_WIDTH), tok(NA_WIDTH), feat(NA_WIDTH), tok(POOL_WIDTH),
                   feat(DIFF_QK_WIDTH), tok(DIFF_QK_WIDTH), feat(DIFF_WIDTH)),
        compiler_params=_params("parallel", "parallel"),
        name="mixer_in_proj",
    )(x, mods, norm_g.reshape(DEPTH, 6, 1, D_MODEL), w, wt, cos, sin, cos_t, sin_t)


def _head_queries(qt, hd):
    pair = qt[(hd // 2) * 2 * NA_HEAD_DIM:(hd // 2 + 1) * 2 * NA_HEAD_DIM, :]
    row = lax.broadcasted_iota(jnp.int32, pair.shape, 0)
    own = (row >= (hd % 2) * NA_HEAD_DIM) & (row < (hd % 2 + 1) * NA_HEAD_DIM)
    return jnp.where(own, pair, jnp.zeros_like(pair))


def _head_pair_keys(k, hd):
    return k[:, (hd // 2) * 2 * NA_HEAD_DIM:(hd // 2 + 1) * 2 * NA_HEAD_DIM]


def _values_with_ones(vt):
    return jnp.concatenate([vt, jnp.ones((NA_ACC_ROWS - NA_HEAD_DIM, vt.shape[1]), BF16)], axis=0)


def _na_kernel(qt_ref, k_ref, vt_ref, kc_ref, vct_ref, bias_ref, o_ref, s_buf, m_buf, *, rows):
    i = pl.program_id(1)
    start = jnp.clip(i * NA_QROWS - NA_WIN_ROWS // 2, 0, rows - NA_KROWS) * GRID_W
    start = pl.multiple_of(start, 2 * GRID_W)
    n_keys = NA_KROWS * GRID_W
    qt = qt_ref[0]
    kb = k_ref[0, pl.ds(start, n_keys), :]
    kc = kc_ref[0]
    head = lambda hd: slice(hd * NA_HEAD_DIM, (hd + 1) * NA_HEAD_DIM)

    def stage_scores(hd):
        qh = _head_queries(qt, hd)
        s_loc = jnp.dot(_head_pair_keys(kb, hd), qh, preferred_element_type=F32) + bias_ref[0, hd]
        s_ctx = jnp.dot(_head_pair_keys(kc, hd), qh, preferred_element_type=F32)
        s_buf[hd % 2, 0:n_keys, :] = s_loc
        s_buf[hd % 2, n_keys:, :] = s_ctx
        m_buf[hd % 2] = jnp.maximum(jnp.max(s_loc, axis=0, keepdims=True), jnp.max(s_ctx, axis=0, keepdims=True))

    def finish(hd, p):
        o = (jnp.dot(_values_with_ones(vt_ref[0, head(hd), pl.ds(start, n_keys)]), p[0:n_keys],
                     preferred_element_type=F32)
             + jnp.dot(_values_with_ones(vct_ref[0, head(hd), :]), p[n_keys:], preferred_element_type=F32))
        return o[0:NA_HEAD_DIM] / o[NA_HEAD_DIM:NA_HEAD_DIM + 1]

    stage_scores(0)
    pending = None
    outs = []
    for hd in range(NA_HEADS):
        if hd + 1 < NA_HEADS:
            stage_scores(hd + 1)
        if pending is not None:
            outs.append(finish(*pending))
        pending = (hd, jnp.exp2(s_buf[hd % 2] - m_buf[hd % 2]).astype(BF16))
    outs.append(finish(*pending))
    o_ref[0] = jnp.concatenate(outs, axis=0).T.astype(BF16)


def _na(qt, k, vt, kc, vct, bias, layer):
    nb, s, _ = k.shape
    rows = s // GRID_W
    n_steps = rows // NA_QROWS
    tq = NA_QROWS * GRID_W
    n_ctx = kc.shape[1]

    def pattern(b, i):
        return (layer, jnp.where(i == 0, 0, jnp.where(i == n_steps - 1, 2, 1)), 0, 0, 0)

    return pl.pallas_call(
        functools.partial(_na_kernel, rows=rows),
        out_shape=jax.ShapeDtypeStruct((nb, s, NA_WIDTH), BF16),
        grid=(nb, n_steps),
        in_specs=[pl.BlockSpec((1, NA_WIDTH, tq), lambda b, i: (b, 0, i)),
                  _resident((1, s, NA_WIDTH), lambda b, i: (b, 0, 0)),
                  _resident((1, NA_WIDTH, s), lambda b, i: (b, 0, 0)),
                  pl.BlockSpec((1, n_ctx, NA_WIDTH), lambda b, i: (b, 0, 0)),
                  pl.BlockSpec((1, NA_WIDTH, n_ctx), lambda b, i: (b, 0, 0)),
                  pl.BlockSpec((None, 1, NA_HEADS, NA_KROWS * GRID_W, tq), pattern)],
        out_specs=pl.BlockSpec((1, tq, NA_WIDTH), lambda b, i: (b, i, 0)),
        scratch_shapes=[pltpu.VMEM((2, NA_KROWS * GRID_W + n_ctx, tq), F32),
                        pltpu.VMEM((2, 1, tq), F32)],
        compiler_params=_params("parallel", "arbitrary"),
        name="neighbourhood_attn",
    )(qt, k, vt, kc, vct, bias)


def _na_bias_tables(rpb, rows):
    n_l, n_h, n_dr, _ = rpb.shape
    pad = GRID_W - NA_WIN_COLS
    w = 2 * GRID_W
    ext = jnp.pad(rpb.astype(F32) * math.log2(math.e), ((0, 0), (0, 0), (0, 0), (pad, pad + 1)), mode="edge")
    skew = jnp.broadcast_to(ext[..., None, :], (n_l, n_h, n_dr, GRID_W, w)).reshape(n_l, n_h, n_dr, GRID_W * w)
    skew = skew[..., :GRID_W * (w - 1)].reshape(n_l, n_h, n_dr, GRID_W, w - 1)
    toe = skew[..., GRID_W - 1:]
    qc = np.arange(GRID_W)[:, None]
    kc = np.arange(GRID_W)[None, :]
    win_c0 = np.clip(qc - NA_WIN_COLS // 2, 0, GRID_W - NA_WIN_COLS)
    in_cols = (kc >= win_c0) & (kc < win_c0 + NA_WIN_COLS)
    toe = jnp.where(jnp.asarray(in_cols), toe, NEG_INF)
    toe = jnp.swapaxes(toe, 3, 4)
    toe = jnp.pad(toe, ((0, 0), (0, 0), (NA_KROWS, NA_KROWS), (0, 0), (0, 0)), constant_values=NEG_INF)
    pair = jnp.concatenate([toe[:, :, 1:], toe[:, :, :-1]], axis=-1)
    assert NA_QROWS == 4
    tables = []
    for r0 in (0, NA_QROWS, rows - NA_QROWS):
        start = int(np.clip(r0 - NA_WIN_ROWS // 2, 0, rows - NA_KROWS))
        c = start - r0 + NA_WIN_ROWS - 1 + NA_KROWS
        halves = [pair[:, :, c - 1 - qr:c - 1 - qr + NA_KROWS].reshape(n_l, n_h, NA_KROWS * GRID_W, 2 * GRID_W)
                  for qr in (0, 2)]
        valid = np.zeros((NA_KROWS, NA_QROWS), bool)
        for qr in range(NA_QROWS):
            ws = int(np.clip(r0 + qr - NA_WIN_ROWS // 2, 0, rows - NA_WIN_ROWS))
            valid[:, qr] = [ws <= start + kl < ws + NA_WIN_ROWS for kl in range(NA_KROWS)]
        valid = np.repeat(np.repeat(valid, GRID_W, axis=0), GRID_W, axis=1)
        tables.append(jnp.where(jnp.asarray(valid), jnp.concatenate(halves, axis=-1), NEG_INF))
    return jnp.stack(tables, axis=1)


def _diff_kernel(*refs, has_ctx, lam_init, n_kv, tk):
    if has_ctx:
        qt_ref, k_ref, vt_ref, kc_ref, vct_ref, lam_ref, g_ref, o_ref = refs[:8]
    else:
        qt_ref, k_ref, vt_ref, lam_ref, g_ref, o_ref = refs[:6]
    qpad_ref, m_ref, acc_ref, s_buf, mblk_ref, p_tail_ref, alpha_tail_ref = refs[-7:]
    head_w = 2 * DIFF_QK_DIM
    n_stream = 2 * DIFF_HEADS
    last = n_stream - 1


    def stage_scores(idx, kh):
        n = kh.shape[0]
        s = jnp.dot(kh, qpad_ref[idx], preferred_element_type=F32)
        s_buf[idx % 2, 0:n, :] = s
        mblk_ref[idx % 2:idx % 2 + 1, :] = jnp.max(s, axis=0, keepdims=True)

    def softmax_staged(idx, n):
        slot = idx % 2
        m_old = m_ref[idx:idx + 1, :]
        m_new = jnp.maximum(m_old, mblk_ref[slot:slot + 1, :])
        alpha = jnp.exp2(m_old - m_new)
        m_ref[idx:idx + 1, :] = m_new
        return alpha, jnp.exp2(s_buf[slot, 0:n, :] - m_new).astype(BF16)

    def accumulate(idx, alpha, p, vth):
        v_ext = jnp.concatenate([vth, jnp.ones((DIFF_ACC_ROWS - DIFF_V_DIM, vth.shape[1]), BF16)], axis=0)
        acc_ref[idx] = alpha * acc_ref[idx] + jnp.dot(v_ext, p, preferred_element_type=F32)

    def single_block(kblk, vtblk):
        k_head = lambda idx: kblk[:, (idx // 2) * head_w:(idx // 2 + 1) * head_w]
        v_head = lambda idx: vtblk[(idx // 2) * DIFF_V_DIM:(idx // 2 + 1) * DIFF_V_DIM, :]
        stage_scores(0, k_head(0))
        pending = None
        for idx in range(n_stream):
            if idx < last:
                stage_scores(idx + 1, k_head(idx + 1))
            if pending is not None:
                accumulate(*pending)
            alpha, p = softmax_staged(idx, kblk.shape[0])
            pending = (idx, alpha, p, v_head(idx))
        accumulate(*pending)

    def k_head_at(j, idx):
        return k_ref[0, pl.ds(pl.multiple_of(j * tk, tk), tk), (idx // 2) * head_w:(idx // 2 + 1) * head_w]

    def v_head_at(j, idx):
        return vt_ref[0, (idx // 2) * DIFF_V_DIM:(idx // 2 + 1) * DIFF_V_DIM, pl.ds(pl.multiple_of(j * tk, tk), tk)]

    def pipelined_block(j, carry):
        pending = (last, alpha_tail_ref[...], p_tail_ref[...], v_head_at(jnp.maximum(j - 1, 0), last))
        for idx in range(n_stream):
            if idx < last:
                stage_scores(idx + 1, k_head_at(j, idx + 1))
            else:
                stage_scores(0, k_head_at(jnp.minimum(j + 1, n_kv - 1), 0))
            accumulate(*pending)
            alpha, p = softmax_staged(idx, tk)
            pending = (idx, alpha, p, v_head_at(j, idx))
        alpha_tail_ref[...] = pending[1]
        p_tail_ref[...] = pending[2]
        return carry

    row = lax.broadcasted_iota(jnp.int32, (head_w, qt_ref.shape[2]), 0)
    for hd in range(DIFF_HEADS):
        qh = qt_ref[0, hd * head_w:(hd + 1) * head_w, :]
        qpad_ref[2 * hd] = jnp.where(row < DIFF_QK_DIM, qh, jnp.zeros_like(qh))
        qpad_ref[2 * hd + 1] = jnp.where(row >= DIFF_QK_DIM, qh, jnp.zeros_like(qh))
    m_ref[...] = jnp.full(m_ref.shape, -jnp.inf, F32)
    acc_ref[...] = jnp.zeros(acc_ref.shape, F32)
    if has_ctx:
        single_block(kc_ref[0], vct_ref[0])

    if n_kv == 1:
        single_block(k_ref[0], vt_ref[0])
    else:
        stage_scores(0, k_head_at(0, 0))
        alpha_tail_ref[...] = jnp.ones(alpha_tail_ref.shape, F32)
        p_tail_ref[...] = jnp.zeros(p_tail_ref.shape, BF16)
        lax.fori_loop(0, n_kv, pipelined_block, 0)
        accumulate(last, alpha_tail_ref[...], p_tail_ref[...], v_head_at(n_kv - 1, last))

    lf = lam_ref[...]
    lam = (jnp.exp(jnp.sum(lf[0:1] * lf[1:2], axis=1, keepdims=True))
           - jnp.exp(jnp.sum(lf[2:3] * lf[3:4], axis=1, keepdims=True)) + lam_init)
    for hd in range(DIFF_HEADS):
        o1 = acc_ref[2 * hd, 0:DIFF_V_DIM, :] / acc_ref[2 * hd, DIFF_V_DIM:DIFF_V_DIM + 1, :]
        o2 = acc_ref[2 * hd + 1, 0:DIFF_V_DIM, :] / acc_ref[2 * hd + 1, DIFF_V_DIM:DIFF_V_DIM + 1, :]
        o = o1 - lam * o2
        o = o * lax.rsqrt(jnp.mean(o * o, axis=0, keepdims=True) + EPS) * g_ref[...] * (1.0 - lam_init)
        o_ref[0, :, hd * DIFF_V_DIM:(hd + 1) * DIFF_V_DIM] = o.T.astype(BF16)


def _diff(qt, k, vt, ctx_kv, lam_vec, subln_g, layer, lam_init):
    nb, _, n = qt.shape
    n_keys = k.shape[1]
    tq = min(DIFF_TQ, n)
    tk = min(DIFF_TK, n_keys)
    n_kv = n_keys // tk
    has_ctx = ctx_kv is not None
    in_specs = [pl.BlockSpec((1, DIFF_QK_WIDTH, tq), lambda b, i: (b, 0, i)),
                _resident((1, n_keys, DIFF_QK_WIDTH), lambda b, i: (b, 0, 0)),
                _resident((1, DIFF_WIDTH, n_keys), lambda b, i: (b, 0, 0))]
    args = [qt, k, vt]
    if has_ctx:
        n_ctx = ctx_kv[0].shape[1]
        in_specs += [pl.BlockSpec((1, n_ctx, DIFF_QK_WIDTH), lambda b, i: (b, 0, 0)),
                     pl.BlockSpec((1, DIFF_WIDTH, n_ctx), lambda b, i: (b, 0, 0))]
        args += list(ctx_kv)
    in_specs += [pl.BlockSpec((None, 4, DIFF_QK_DIM), lambda b, i: (layer, 0, 0)),
                 pl.BlockSpec((None, DIFF_V_DIM, 1), lambda b, i: (layer, 0, 0))]
    args += [lam_vec, subln_g.reshape(DEPTH, DIFF_V_DIM, 1)]
    n_stream = 2 * DIFF_HEADS
    return pl.pallas_call(
        functools.partial(_diff_kernel, has_ctx=has_ctx, lam_init=lam_init, n_kv=n_kv, tk=tk),
        out_shape=jax.ShapeDtypeStruct((nb, n, DIFF_WIDTH), BF16),
        grid=(nb, n // tq),
        in_specs=in_specs,
        out_specs=pl.BlockSpec((1, tq, DIFF_WIDTH), lambda b, i: (b, i, 0)),
        scratch_shapes=[pltpu.VMEM((n_stream, 2 * DIFF_QK_DIM, tq), BF16),
                        pltpu.VMEM((n_stream, tq), F32),
                        pltpu.VMEM((n_stream, DIFF_ACC_ROWS, tq), F32),
                        pltpu.VMEM((2, tk, tq), F32),
                        pltpu.VMEM((2, tq), F32),
                        pltpu.VMEM((tk, tq), BF16),
                        pltpu.VMEM((1, tq), F32)],
        compiler_params=_params("parallel", "arbitrary"),
        name="diff_attn",
    )(*args)


def _ctx_attn_kernel(qt_ref, k_ref, vt_ref, o_ref):
    qt, k = qt_ref[0], k_ref[0]
    outs = []
    for hd in range(NA_HEADS):
        s = jnp.dot(_head_pair_keys(k, hd), _head_queries(qt, hd), preferred_element_type=F32)
        p = jnp.exp2(s - jnp.max(s, axis=0, keepdims=True)).astype(BF16)
        o = jnp.dot(_values_with_ones(vt_ref[0, hd * NA_HEAD_DIM:(hd + 1) * NA_HEAD_DIM, :]), p,
                    preferred_element_type=F32)
        outs.append(o[0:NA_HEAD_DIM] / o[NA_HEAD_DIM:NA_HEAD_DIM + 1])
    o_ref[0] = jnp.concatenate(outs, axis=0).T.astype(BF16)


def _ctx_attn(qt, k, vt):
    nb, n, w = k.shape
    tok = pl.BlockSpec((1, n, w), lambda b: (b, 0, 0))
    feat = pl.BlockSpec((1, w, n), lambda b: (b, 0, 0))
    return pl.pallas_call(
        _ctx_attn_kernel,
        out_shape=jax.ShapeDtypeStruct(k.shape, BF16),
        grid=(nb,),
        in_specs=[feat, tok, feat],
        out_specs=tok,
        compiler_params=_params("parallel"),
        name="ctx_dense_attn",
    )(qt, k, vt)


def _outproj_ffn_kernel(x_ref, mod_ref, g_ref, a_ref, pprev_ref, pcur_ref, pnext_ref, c_ref,
                        pw_ref, ps_ref, w_ref, fmod_ref, fg_ref, w1_ref, w2_ref, o_ref, act_ref,
                        *, tm, seq_len, res_w):
    i = pl.program_id(1)
    n = tm + 2 * POOL_HALO
    u = jnp.concatenate([pprev_ref[0], pcur_ref[0], pnext_ref[0]], axis=0)
    t_ext = i * tm - POOL_HALO + lax.broadcasted_iota(jnp.int32, (n, 1), 0)
    u = jnp.where((t_ext >= 0) & (t_ext < seq_len), u, 0.0)
    s2 = u + pltpu.roll(u, 1, 0)
    s4 = pltpu.roll(s2, 1, 0) + pltpu.roll(s2, n - 1, 0)
    s8 = pltpu.roll(s4, 2, 0) + pltpu.roll(s4, n - 2, 0)
    s16 = pltpu.roll(s8, 4, 0) + pltpu.roll(s8, n - 4, 0)
    cur = slice(POOL_HALO, POOL_HALO + tm)
    group = lax.broadcasted_iota(jnp.int32, (1, POOL_WIDTH), 1) // POOL_GROUP_DIM
    wsum = jnp.where(group == 0, s2[cur], jnp.where(group == 1, s4[cur], jnp.where(group == 2, s8[cur], s16[cur])))
    half = jnp.where(group == 0, POOL_WINDOWS[0] // 2,
                     jnp.where(group == 1, POOL_WINDOWS[1] // 2,
                               jnp.where(group == 2, POOL_WINDOWS[2] // 2, POOL_WINDOWS[3] // 2)))
    t = i * tm + lax.broadcasted_iota(jnp.int32, (tm, 1), 0)
    count = (jnp.clip(t + half, 0, seq_len) - jnp.clip(t - half, 0, seq_len)).astype(F32)
    pooled = (wsum / count - u[cur]).astype(BF16)
    b = jnp.dot(pooled, pw_ref[...], preferred_element_type=F32) * ps_ref[...]
    cat = jnp.concatenate([a_ref[0], b.astype(BF16), c_ref[0]], axis=1)
    y = jnp.dot(cat, w_ref[...], preferred_element_type=F32)
    x_mixed = x_ref[0] + mod_ref[0, 2:3, :] * _rms(y, g_ref[...])
    o_ref[0] = _swiglu_half_step(x_mixed, fmod_ref, fg_ref, w1_ref, w2_ref, act_ref, res_w)


def _outproj_ffn(x, mods, mod_row, norm_g, a, pin, c, pool_bd, pool_scale, w, w1, w2, layer, res_w):
    nb, n, _ = x.shape
    tm = min(TOKEN_TILE, n)
    halo_per_tile = tm // POOL_HALO
    n_halo = n // POOL_HALO
    tok = lambda width: pl.BlockSpec((1, tm, width), lambda b, i: (b, i, 0))
    return pl.pallas_call(
        functools.partial(_outproj_ffn_kernel, tm=tm, seq_len=n, res_w=res_w),
        out_shape=jax.ShapeDtypeStruct(x.shape, F32),
        grid=(nb, n // tm),
        in_specs=[tok(D_MODEL),
                  _mod_spec(layer, mod_row, 1),
                  _norm_spec(layer, 3, 1),
                  tok(NA_WIDTH),
                  pl.BlockSpec((1, POOL_HALO, POOL_WIDTH),
                               lambda b, i: (b, jnp.maximum(i * halo_per_tile - 1, 0), 0)),
                  tok(POOL_WIDTH),
                  pl.BlockSpec((1, POOL_HALO, POOL_WIDTH),
                               lambda b, i: (b, jnp.minimum((i + 1) * halo_per_tile, n_halo - 1), 0)),
                  tok(DIFF_WIDTH),
                  pl.BlockSpec((None, POOL_WIDTH, POOL_WIDTH), lambda b, i: (layer, 0, 0)),
                  pl.BlockSpec((None, 1, POOL_WIDTH), lambda b, i: (layer, 0, 0)),
                  _resident((None, MIX_WIDTH, D_MODEL), lambda b, i: (layer, 0, 0)),
                  _mod_spec(layer, mod_row, 2),
                  _norm_spec(layer, 4, 2),
                  _resident((None, None, D_MODEL, 2 * D_FF), lambda b, i: (layer, 1, 0, 0)),
                  _resident((None, None, D_FF, D_MODEL), lambda b, i: (layer, 1, 0, 0))],
        out_specs=tok(D_MODEL),
        scratch_shapes=[pltpu.VMEM((tm, D_FF), BF16)],
        compiler_params=_params("parallel", "parallel"),
        name="mixer_out_proj_ffn",
    )(x, mods, norm_g.reshape(DEPTH, 6, 1, D_MODEL), a, pin, pin, pin, c, pool_bd, pool_scale, w,
      mods, norm_g.reshape(DEPTH, 3, 2, D_MODEL), w1, w2)


def _rope_tables(n_tokens):
    t = np.arange(n_tokens)
    row = (t // GRID_W).astype(np.float32)
    col = (t % GRID_W).astype(np.float32)
    n_freq = DIFF_QK_DIM // 4
    inv_freq = np.power(np.float32(ROPE_THETA), -np.arange(n_freq, dtype=np.float32) / np.float32(n_freq))
    ang = np.concatenate([row[:, None] * inv_freq, col[:, None] * inv_freq], axis=-1).astype(np.float32)
    cos, sin = np.cos(ang), np.sin(ang)
    tables = (np.tile(cos, (1, 4)), np.concatenate([-sin, sin, -sin, sin], axis=-1), cos.T, sin.T)
    return tuple(jnp.asarray(np.ascontiguousarray(a), F32) for a in tables)


def _identity_rope_tables(n_tokens):
    half = DIFF_QK_DIM // 2
    return (jnp.ones((n_tokens, 2 * DIFF_QK_DIM), F32), jnp.zeros((n_tokens, 2 * DIFF_QK_DIM), F32),
            jnp.ones((half, n_tokens), F32), jnp.zeros((half, n_tokens), F32))


def _block_diag(w):
    n_l, n_g, cg, _ = w.shape
    eye = jnp.eye(n_g, dtype=w.dtype)
    return (eye[None, :, None, :, None] * w[:, :, :, None, :]).reshape(n_l, n_g * cg, n_g * cg)


def kernel(x, c, ctx, c_ctx, w_ada, b_ada, norm_g, ffn_w1, ffn_w2, w_in, w_out, na_rpb, pool_w, pool_scale,
           diff_lambda, diff_subln_g):
    nb, s, _ = x.shape
    n_ctx = ctx.shape[1]
    assert nb < MOD_ROWS and s % (GRID_W * NA_QROWS) == 0 and s % DIFF_TK == 0 and n_ctx % POOL_HALO == 0
    assert n_ctx <= DIFF_TK

    w1 = ffn_w1.astype(BF16)
    w2 = ffn_w2.astype(BF16)
    q_lo = 3 * NA_WIDTH + POOL_WIDTH
    k_lo = q_lo + DIFF_QK_WIDTH
    v_lo = k_lo + DIFF_QK_WIDTH
    w_in_b = w_in.astype(BF16)
    w_tok = w_in_b
    w_feat = jnp.swapaxes(jnp.concatenate([w_in_b[:, :, q_lo:k_lo], w_in_b[:, :, v_lo:], w_in_b[:, :, :NA_WIDTH],
                                           w_in_b[:, :, 2 * NA_WIDTH:3 * NA_WIDTH]], axis=2), 1, 2)
    w_out_b = w_out.astype(BF16)

    ctx_row = nb
    cvec = jnp.zeros((MOD_ROWS, D_MODEL), F32).at[:nb].set(c).at[ctx_row].set(c_ctx)
    mods = _ada(cvec, w_ada, b_ada).reshape(DEPTH, MOD_ROWS, N_MOD // 3, 3, D_MODEL)
    lat_row = lambda b: b
    ctx_mod_row = lambda b: ctx_row

    rope_lat = _rope_tables(s)
    rope_ctx = _identity_rope_tables(n_ctx)
    bias = _na_bias_tables(na_rpb, s // GRID_W)
    pool_bd = _block_diag(pool_w).astype(BF16)
    pool_sc = pool_scale.reshape(DEPTH, 1, POOL_WIDTH)

    x_lat, x_ctx = x, ctx
    for layer in range(DEPTH):
        last = layer == DEPTH - 1
        lam_init = 0.8 - 0.6 * math.exp(-0.3 * layer)

        x_lat = _ffn(x_lat, mods, lat_row, norm_g, w1, w2, layer, 0, 0.5)
        x_ctx = _ffn(x_ctx, mods, ctx_mod_row, norm_g, w1, w2, layer, 0, 0.5)

        aq, ak, av, pin, dq, dk, dv = _inproj(x_lat, mods, lat_row, norm_g, w_tok, w_feat, layer, rope_lat)
        aqc, akc, avc, pinc, dqc, dkc, dvc = _inproj(x_ctx, mods, ctx_mod_row, norm_g, w_tok, w_feat, layer, rope_ctx)
        a_lat = _na(aq, ak, av, akc, avc, bias, layer)
        c_lat = _diff(dq, dk, dv, (dkc, dvc), diff_lambda, diff_subln_g, layer, lam_init)
        x_lat = _outproj_ffn(x_lat, mods, lat_row, norm_g, a_lat, pin, c_lat, pool_bd, pool_sc, w_out_b, w1, w2,
                             layer, 0.5)

        if not last:
            a_ctx = _ctx_attn(aqc, akc, avc)
            c_ctx_out = _diff(dqc, dkc, dvc, None, diff_lambda, diff_subln_g, layer, lam_init)
            x_ctx = _outproj_ffn(x_ctx, mods, ctx_mod_row, norm_g, a_ctx, pinc, c_ctx_out, pool_bd, pool_sc, w_out_b,
                                 w1, w2, layer, 0.5)
    return x_lat
```

```python
import functools
import math

import numpy as np
import jax
import jax.numpy as jnp
from jax import lax
from jax.experimental import pallas as pl
from jax.experimental.pallas import tpu as pltpu

D_MODEL = 1024
DEPTH = 2
GRID_W = 64
N_MOD = 9
D_FF = 2816
EPS = 1e-6
NEG_INF = -1e30
ROPE_THETA = 10000.0

NA_HEADS = 4
NA_HEAD_DIM = 64
NA_WIN_ROWS = 8
NA_WIN_COLS = 16
NA_WIDTH = NA_HEADS * NA_HEAD_DIM

POOL_WINDOWS = (2, 4, 8, 16)
POOL_GROUP_DIM = 64
POOL_WIDTH = len(POOL_WINDOWS) * POOL_GROUP_DIM
POOL_HALO = max(POOL_WINDOWS) // 2

DIFF_HEADS = 4
DIFF_QK_DIM = 64
DIFF_V_DIM = 2 * DIFF_QK_DIM
DIFF_QK_WIDTH = DIFF_HEADS * 2 * DIFF_QK_DIM
DIFF_WIDTH = DIFF_HEADS * DIFF_V_DIM
DIFF_Q_SCALE = DIFF_QK_DIM ** -0.5 * math.log2(math.e)
NA_Q_SCALE = NA_HEAD_DIM ** -0.5 * math.log2(math.e)

MIX_WIDTH = NA_WIDTH + POOL_WIDTH + DIFF_WIDTH
FEAT_ROWS = DIFF_QK_WIDTH + DIFF_WIDTH + 2 * NA_WIDTH
IN_WIDTH = 3 * NA_WIDTH + POOL_WIDTH + 2 * DIFF_QK_WIDTH + DIFF_WIDTH
MOD_ROWS = 8

BF16 = jnp.bfloat16
F32 = jnp.float32

V7X_VMEM_BYTES = 64 * 1024 * 1024
VMEM_LIMIT = V7X_VMEM_BYTES * 3 // 4

TOKEN_TILE = 512
FFN_TILE = 1024
FF_CHUNK = 256
ADA_COLS = 1152
NA_QROWS = 4
NA_KROWS = NA_QROWS + NA_WIN_ROWS
NA_ACC_ROWS = NA_HEAD_DIM + 16
DIFF_TQ = 512
DIFF_TK = 1024
DIFF_PV_CHUNK = 256
DIFF_ACC_ROWS = DIFF_V_DIM + 16

_NT = (((1,), (1,)), ((), ()))


def _params(*sem):
    return pltpu.CompilerParams(dimension_semantics=sem, vmem_limit_bytes=VMEM_LIMIT)


def _resident(shape, index_map):
    return pl.BlockSpec(shape, index_map, pipeline_mode=pl.Buffered(1))


def _rms(x, g):
    return x * lax.rsqrt(jnp.mean(x * x, axis=-1, keepdims=True) + EPS) * g


def _modulate(x, mod_ref, g):
    shift, scale = mod_ref[0, 0:1, :], mod_ref[0, 1:2, :]
    return _rms(x, g) * (1.0 + scale) + shift


def _ada_kernel(c_ref, w_ref, b_ref, o_ref):
    cv = c_ref[...]
    o_ref[0] = jnp.dot(cv * jax.nn.sigmoid(cv), w_ref[0], preferred_element_type=F32) + b_ref[0]


def _ada(cvec, w_ada, b_ada):
    n_out = N_MOD * D_MODEL
    return pl.pallas_call(
        _ada_kernel,
        out_shape=jax.ShapeDtypeStruct((DEPTH, MOD_ROWS, n_out), F32),
        grid=(DEPTH, n_out // ADA_COLS),
        in_specs=[pl.BlockSpec((MOD_ROWS, D_MODEL), lambda l, j: (0, 0)),
                  pl.BlockSpec((1, D_MODEL, ADA_COLS), lambda l, j: (l, 0, j)),
                  pl.BlockSpec((1, 1, ADA_COLS), lambda l, j: (l, 0, j))],
        out_specs=pl.BlockSpec((1, MOD_ROWS, ADA_COLS), lambda l, j: (l, 0, j)),
        compiler_params=_params("parallel", "parallel"),
        name="ada_mod",
    )(cvec, w_ada, b_ada.reshape(DEPTH, 1, n_out))


def _swiglu_half_step(x, mod_ref, g_ref, w1_ref, w2_ref, act_ref, res_w):
    h = _modulate(x, mod_ref, g_ref[0:1, :]).astype(BF16)
    for c in range(D_FF // FF_CHUNK):
        lo = c * FF_CHUNK
        a = jnp.dot(h, w1_ref[:, lo:lo + FF_CHUNK], preferred_element_type=F32)
        b = jnp.dot(h, w1_ref[:, D_FF + lo:D_FF + lo + FF_CHUNK], preferred_element_type=F32)
        act_ref[:, lo:lo + FF_CHUNK] = (a * jax.nn.sigmoid(a) * b).astype(BF16)
    y = jnp.dot(act_ref[...], w2_ref[...], preferred_element_type=F32)
    return x + res_w * mod_ref[0, 2:3, :] * _rms(y, g_ref[1:2, :])


def _ffn_kernel(x_ref, mod_ref, g_ref, w1_ref, w2_ref, o_ref, act_ref, *, res_w):
    o_ref[0] = _swiglu_half_step(x_ref[0], mod_ref, g_ref, w1_ref, w2_ref, act_ref, res_w)


def _mod_spec(layer, mod_row, sub):
    return pl.BlockSpec((None, 1, None, 3, D_MODEL), lambda b, i: (layer, mod_row(b), sub, 0, 0))


def _norm_spec(layer, first, count):
    return pl.BlockSpec((None, None, count, D_MODEL), lambda b, i: (layer, first // count, 0, 0))


def _ffn(x, mods, mod_row, norm_g, w1, w2, layer, which, res_w):
    nb, n, _ = x.shape
    tm = min(FFN_TILE, n)
    return pl.pallas_call(
        functools.partial(_ffn_kernel, res_w=res_w),
        out_shape=jax.ShapeDtypeStruct(x.shape, F32),
        grid=(nb, n // tm),
        in_specs=[pl.BlockSpec((1, tm, D_MODEL), lambda b, i: (b, i, 0)),
                  _mod_spec(layer, mod_row, 2 * which),
                  _norm_spec(layer, 4 * which, 2),
                  _resident((None, None, D_MODEL, 2 * D_FF), lambda b, i: (layer, which, 0, 0)),
                  _resident((None, None, D_FF, D_MODEL), lambda b, i: (layer, which, 0, 0))],
        out_specs=pl.BlockSpec((1, tm, D_MODEL), lambda b, i: (b, i, 0)),
        scratch_shapes=[pltpu.VMEM((tm, D_FF), BF16)],
        compiler_params=_params("parallel", "parallel"),
        name="ffn",
    )(x, mods, norm_g.reshape(DEPTH, 3, 2, D_MODEL), w1, w2)


def _inproj_kernel(x_ref, mod_ref, g_ref, w_ref, wt_ref, cos_ref, sin_ref, cos_t_ref, sin_t_ref,
                   naqt_ref, nak_ref, navt_ref, pin_ref, dqt_ref, dk_ref, dvt_ref):
    x = x_ref[0]
    h = _modulate(x, mod_ref, g_ref[0:1, :]).astype(BF16)

    def proj(lo, width):
        return jnp.dot(h, w_ref[:, lo:lo + width], preferred_element_type=F32)

    nak_ref[0] = proj(NA_WIDTH, NA_WIDTH).astype(BF16)
    pin_ref[0] = proj(3 * NA_WIDTH, POOL_WIDTH)

    cos, sin = cos_ref[...], sin_ref[...]
    lane = lax.broadcasted_iota(jnp.int32, cos.shape, 1)
    half = DIFF_QK_DIM // 2
    first_half = (lane % DIFF_QK_DIM) < half
    head_w = 2 * DIFF_QK_DIM
    k = proj(3 * NA_WIDTH + POOL_WIDTH + DIFF_QK_WIDTH, DIFF_QK_WIDTH)
    for hd in range(DIFF_HEADS):
        t = k[:, hd * head_w:(hd + 1) * head_w]
        partner = jnp.where(first_half, pltpu.roll(t, head_w - half, 1), pltpu.roll(t, half, 1))
        dk_ref[0, :, hd * head_w:(hd + 1) * head_w] = (t * cos + partner * sin).astype(BF16)

    qv = lax.dot_general(wt_ref[...], h, _NT, preferred_element_type=F32)
    na_lo = DIFF_QK_WIDTH + DIFF_WIDTH
    naqt_ref[0] = (qv[na_lo:na_lo + NA_WIDTH] * NA_Q_SCALE).astype(BF16)
    navt_ref[0] = qv[na_lo + NA_WIDTH:].astype(BF16)
    cos_t, sin_t = cos_t_ref[...], sin_t_ref[...]
    for st in range(2 * DIFF_HEADS):
        base = st * DIFF_QK_DIM
        x1 = qv[base:base + half] * DIFF_Q_SCALE
        x2 = qv[base + half:base + DIFF_QK_DIM] * DIFF_Q_SCALE
        dqt_ref[0, base:base + half, :] = (x1 * cos_t - x2 * sin_t).astype(BF16)
        dqt_ref[0, base + half:base + DIFF_QK_DIM, :] = (x1 * sin_t + x2 * cos_t).astype(BF16)
    dvt_ref[0] = qv[DIFF_QK_WIDTH:na_lo].astype(BF16)


def _inproj(x, mods, mod_row, norm_g, w, wt, layer, rope):
    nb, n, _ = x.shape
    tm = min(FFN_TILE, n)
    cos, sin, cos_t, sin_t = rope
    n_tok_major = w.shape[2]
    tok = lambda width: pl.BlockSpec((1, tm, width), lambda b, i: (b, i, 0))
    feat = lambda width: pl.BlockSpec((1, width, tm), lambda b, i: (b, 0, i))
    out = lambda width, dt: jax.ShapeDtypeStruct((nb, n, width), dt)
    out_t = lambda width, dt: jax.ShapeDtypeStruct((nb, width, n), dt)
    return pl.pallas_call(
        _inproj_kernel,
        out_shape=(out_t(NA_WIDTH, BF16), out(NA_WIDTH, BF16), out_t(NA_WIDTH, BF16), out(POOL_WIDTH, F32),
                   out_t(DIFF_QK_WIDTH, BF16), out(DIFF_QK_WIDTH, BF16), out_t(DIFF_WIDTH, BF16)),
        grid=(nb, n // tm),
        in_specs=[tok(D_MODEL),
                  _mod_spec(layer, mod_row, 1),
                  _norm_spec(layer, 2, 1),
                  _resident((None, D_MODEL, n_tok_major), lambda b, i: (layer, 0, 0)),
                  _resident((None, FEAT_ROWS, D_MODEL), lambda b, i: (layer, 0, 0)),
                  pl.BlockSpec((tm, 2 * DIFF_QK_DIM), lambda b, i: (i, 0)),
                  pl.BlockSpec((tm, 2 * DIFF_QK_DIM), lambda b, i: (i, 0)),
                  pl.BlockSpec((DIFF_QK_DIM // 2, tm), lambda b, i: (0, i)),
                  pl.BlockSpec((DIFF_QK_DIM // 2, tm), lambda b, i: (0, i))],
        out_specs=(feat(NA_WIDTH), tok(NA_WIDTH), feat(NA_WIDTH), tok(POOL_WIDTH),
                   feat(DIFF_QK_WIDTH), tok(DIFF_QK_WIDTH), feat(DIFF_WIDTH)),
        compiler_params=_params("parallel", "parallel"),
        name="mixer_in_proj",
    )(x, mods, norm_g.reshape(DEPTH, 6, 1, D_MODEL), w, wt, cos, sin, cos_t, sin_t)


def _head_queries(qt, hd):
    pair = qt[(hd // 2) * 2 * NA_HEAD_DIM:(hd // 2 + 1) * 2 * NA_HEAD_DIM, :]
    row = lax.broadcasted_iota(jnp.int32, pair.shape, 0)
    own = (row >= (hd % 2) * NA_HEAD_DIM) & (row < (hd % 2 + 1) * NA_HEAD_DIM)
    return jnp.where(own, pair, jnp.zeros_like(pair))


def _head_pair_keys(k, hd):
    return k[:, (hd // 2) * 2 * NA_HEAD_DIM:(hd // 2 + 1) * 2 * NA_HEAD_DIM]


def _values_with_ones(vt):
    return jnp.concatenate([vt, jnp.ones((NA_ACC_ROWS - NA_HEAD_DIM, vt.shape[1]), BF16)], axis=0)


def _na_kernel(qt_ref, k_ref, vt_ref, kc_ref, vct_ref, bias_ref, o_ref, s_buf, m_buf, *, rows):
    i = pl.program_id(1)
    start = jnp.clip(i * NA_QROWS - NA_WIN_ROWS // 2, 0, rows - NA_KROWS) * GRID_W
    start = pl.multiple_of(start, 2 * GRID_W)
    n_keys = NA_KROWS * GRID_W
    qt = qt_ref[0]
    kb = k_ref[0, pl.ds(start, n_keys), :]
    kc = kc_ref[0]
    head = lambda hd: slice(hd * NA_HEAD_DIM, (hd + 1) * NA_HEAD_DIM)

    def stage_scores(hd):
        qh = _head_queries(qt, hd)
        s_loc = jnp.dot(_head_pair_keys(kb, hd), qh, preferred_element_type=F32) + bias_ref[0, hd]
        s_ctx = jnp.dot(_head_pair_keys(kc, hd), qh, preferred_element_type=F32)
        s_buf[hd % 2, 0:n_keys, :] = s_loc
        s_buf[hd % 2, n_keys:, :] = s_ctx
        m_buf[hd % 2] = jnp.maximum(jnp.max(s_loc, axis=0, keepdims=True), jnp.max(s_ctx, axis=0, keepdims=True))

    def finish(hd, p):
        o = (jnp.dot(_values_with_ones(vt_ref[0, head(hd), pl.ds(start, n_keys)]), p[0:n_keys],
                     preferred_element_type=F32)
             + jnp.dot(_values_with_ones(vct_ref[0, head(hd), :]), p[n_keys:], preferred_element_type=F32))
        return o[0:NA_HEAD_DIM] / o[NA_HEAD_DIM:NA_HEAD_DIM + 1]

    stage_scores(0)
    pending = None
    outs = []
    for hd in range(NA_HEADS):
        if hd + 1 < NA_HEADS:
            stage_scores(hd + 1)
        if pending is not None:
            outs.append(finish(*pending))
        pending = (hd, jnp.exp2(s_buf[hd % 2] - m_buf[hd % 2]).astype(BF16))
    outs.append(finish(*pending))
    o_ref[0] = jnp.concatenate(outs, axis=0).T.astype(BF16)


def _na(qt, k, vt, kc, vct, bias, layer):
    nb, s, _ = k.shape
    rows = s // GRID_W
    n_steps = rows // NA_QROWS
    tq = NA_QROWS * GRID_W
    n_ctx = kc.shape[1]

    def pattern(b, i):
        return (layer, jnp.where(i == 0, 0, jnp.where(i == n_steps - 1, 2, 1)), 0, 0, 0)

    return pl.pallas_call(
        functools.partial(_na_kernel, rows=rows),
        out_shape=jax.ShapeDtypeStruct((nb, s, NA_WIDTH), BF16),
        grid=(nb, n_steps),
        in_specs=[pl.BlockSpec((1, NA_WIDTH, tq), lambda b, i: (b, 0, i)),
                  _resident((1, s, NA_WIDTH), lambda b, i: (b, 0, 0)),
                  _resident((1, NA_WIDTH, s), lambda b, i: (b, 0, 0)),
                  pl.BlockSpec((1, n_ctx, NA_WIDTH), lambda b, i: (b, 0, 0)),
                  pl.BlockSpec((1, NA_WIDTH, n_ctx), lambda b, i: (b, 0, 0)),
                  pl.BlockSpec((None, 1, NA_HEADS, NA_KROWS * GRID_W, tq), pattern)],
        out_specs=pl.BlockSpec((1, tq, NA_WIDTH), lambda b, i: (b, i, 0)),
        scratch_shapes=[pltpu.VMEM((2, NA_KROWS * GRID_W + n_ctx, tq), F32),
                        pltpu.VMEM((2, 1, tq), F32)],
        compiler_params=_params("parallel", "arbitrary"),
        name="neighbourhood_attn",
    )(qt, k, vt, kc, vct, bias)


def _na_bias_tables(rpb, rows):
    n_l, n_h, n_dr, _ = rpb.shape
    pad = GRID_W - NA_WIN_COLS
    w = 2 * GRID_W
    ext = jnp.pad(rpb.astype(F32) * math.log2(math.e), ((0, 0), (0, 0), (0, 0), (pad, pad + 1)), mode="edge")
    skew = jnp.broadcast_to(ext[..., None, :], (n_l, n_h, n_dr, GRID_W, w)).reshape(n_l, n_h, n_dr, GRID_W * w)
    skew = skew[..., :GRID_W * (w - 1)].reshape(n_l, n_h, n_dr, GRID_W, w - 1)
    toe = skew[..., GRID_W - 1:]
    qc = np.arange(GRID_W)[:, None]
    kc = np.arange(GRID_W)[None, :]
    win_c0 = np.clip(qc - NA_WIN_COLS // 2, 0, GRID_W - NA_WIN_COLS)
    in_cols = (kc >= win_c0) & (kc < win_c0 + NA_WIN_COLS)
    toe = jnp.where(jnp.asarray(in_cols), toe, NEG_INF)
    toe = jnp.swapaxes(toe, 3, 4)
    toe = jnp.pad(toe, ((0, 0), (0, 0), (NA_KROWS, NA_KROWS), (0, 0), (0, 0)), constant_values=NEG_INF)
    pair = jnp.concatenate([toe[:, :, 1:], toe[:, :, :-1]], axis=-1)
    assert NA_QROWS == 4
    tables = []
    for r0 in (0, NA_QROWS, rows - NA_QROWS):
        start = int(np.clip(r0 - NA_WIN_ROWS // 2, 0, rows - NA_KROWS))
        c = start - r0 + NA_WIN_ROWS - 1 + NA_KROWS
        halves = [pair[:, :, c - 1 - qr:c - 1 - qr + NA_KROWS].reshape(n_l, n_h, NA_KROWS * GRID_W, 2 * GRID_W)
                  for qr in (0, 2)]
        valid = np.zeros((NA_KROWS, NA_QROWS), bool)
        for qr in range(NA_QROWS):
            ws = int(np.clip(r0 + qr - NA_WIN_ROWS // 2, 0, rows - NA_WIN_ROWS))
            valid[:, qr] = [ws <= start + kl < ws + NA_WIN_ROWS for kl in range(NA_KROWS)]
        valid = np.repeat(np.repeat(valid, GRID_W, axis=0), GRID_W, axis=1)
        tables.append(jnp.where(jnp.asarray(valid), jnp.concatenate(halves, axis=-1), NEG_INF))
    return jnp.stack(tables, axis=1)


def _diff_kernel(*refs, has_ctx, lam_init, n_kv, tk):
    if has_ctx:
        qt_ref, k_ref, vt_ref, kc_ref, vct_ref, lam_ref, g_ref, o_ref = refs[:8]
    else:
        qt_ref, k_ref, vt_ref, lam_ref, g_ref, o_ref = refs[:6]
    qpad_ref, m_ref, acc_ref, s_buf, mblk_ref = refs[-5:]
    head_w = 2 * DIFF_QK_DIM
    n_stream = 2 * DIFF_HEADS
    last = n_stream - 1


    def stage_scores(idx, kh):
        n = kh.shape[0]
        s = jnp.dot(kh, qpad_ref[idx], preferred_element_type=F32)
        s_buf[idx % 2, 0:n, :] = s
        mblk_ref[idx % 2:idx % 2 + 1, :] = jnp.max(s, axis=0, keepdims=True)

    def softmax_staged(idx, n):
        slot = idx % 2
        m_old = m_ref[idx:idx + 1, :]
        m_new = jnp.maximum(m_old, mblk_ref[slot:slot + 1, :])
        alpha = jnp.exp2(m_old - m_new)
        m_ref[idx:idx + 1, :] = m_new
        return alpha, m_new

    def softmax_pv(idx, n, vth):
        slot = idx % 2
        alpha, m_new = softmax_staged(idx, n)
        ones = jnp.ones((DIFF_ACC_ROWS - DIFF_V_DIM, DIFF_PV_CHUNK), BF16)
        total = None
        for lo in range(0, n, DIFF_PV_CHUNK):
            p = jnp.exp2(s_buf[slot, lo:lo + DIFF_PV_CHUNK, :] - m_new).astype(BF16)
            v_ext = jnp.concatenate([vth[:, lo:lo + DIFF_PV_CHUNK], ones], axis=0)
            part = jnp.dot(v_ext, p, preferred_element_type=F32)
            total = part if total is None else total + part
        acc_ref[idx] = alpha * acc_ref[idx] + total

    def single_block(kblk, vtblk):
        k_head = lambda idx: kblk[:, (idx // 2) * head_w:(idx // 2 + 1) * head_w]
        v_head = lambda idx: vtblk[(idx // 2) * DIFF_V_DIM:(idx // 2 + 1) * DIFF_V_DIM, :]
        stage_scores(0, k_head(0))
        for idx in range(n_stream):
            if idx < last:
                stage_scores(idx + 1, k_head(idx + 1))
            softmax_pv(idx, kblk.shape[0], v_head(idx))

    def k_head_at(j, idx):
        return k_ref[0, pl.ds(pl.multiple_of(j * tk, tk), tk), (idx // 2) * head_w:(idx // 2 + 1) * head_w]

    def v_head_at(j, idx):
        return vt_ref[0, (idx // 2) * DIFF_V_DIM:(idx // 2 + 1) * DIFF_V_DIM, pl.ds(pl.multiple_of(j * tk, tk), tk)]

    def pipelined_block(j, carry):
        for idx in range(n_stream):
            if idx < last:
                stage_scores(idx + 1, k_head_at(j, idx + 1))
            else:
                stage_scores(0, k_head_at(jnp.minimum(j + 1, n_kv - 1), 0))
            softmax_pv(idx, tk, v_head_at(j, idx))
        return carry

    row = lax.broadcasted_iota(jnp.int32, (head_w, qt_ref.shape[2]), 0)
    for hd in range(DIFF_HEADS):
        qh = qt_ref[0, hd * head_w:(hd + 1) * head_w, :]
        qpad_ref[2 * hd] = jnp.where(row < DIFF_QK_DIM, qh, jnp.zeros_like(qh))
        qpad_ref[2 * hd + 1] = jnp.where(row >= DIFF_QK_DIM, qh, jnp.zeros_like(qh))
    m_ref[...] = jnp.full(m_ref.shape, -jnp.inf, F32)
    acc_ref[...] = jnp.zeros(acc_ref.shape, F32)
    if has_ctx:
        single_block(kc_ref[0], vct_ref[0])

    if n_kv == 1:
        single_block(k_ref[0], vt_ref[0])
    else:
        stage_scores(0, k_head_at(0, 0))
        lax.fori_loop(0, n_kv, pipelined_block, 0)

    lf = lam_ref[...]
    lam = (jnp.exp(jnp.sum(lf[0:1] * lf[1:2], axis=1, keepdims=True))
           - jnp.exp(jnp.sum(lf[2:3] * lf[3:4], axis=1, keepdims=True)) + lam_init)
    for hd in range(DIFF_HEADS):
        o1 = acc_ref[2 * hd, 0:DIFF_V_DIM, :] / acc_ref[2 * hd, DIFF_V_DIM:DIFF_V_DIM + 1, :]
        o2 = acc_ref[2 * hd + 1, 0:DIFF_V_DIM, :] / acc_ref[2 * hd + 1, DIFF_V_DIM:DIFF_V_DIM + 1, :]
        o = o1 - lam * o2
        o = o * lax.rsqrt(jnp.mean(o * o, axis=0, keepdims=True) + EPS) * g_ref[...] * (1.0 - lam_init)
        o_ref[0, :, hd * DIFF_V_DIM:(hd + 1) * DIFF_V_DIM] = o.T.astype(BF16)


def _diff(qt, k, vt, ctx_kv, lam_vec, subln_g, layer, lam_init):
    nb, _, n = qt.shape
    n_keys = k.shape[1]
    tq = min(DIFF_TQ, n)
    tk = min(DIFF_TK, n_keys)
    n_kv = n_keys // tk
    has_ctx = ctx_kv is not None
    in_specs = [pl.BlockSpec((1, DIFF_QK_WIDTH, tq), lambda b, i: (b, 0, i)),
                _resident((1, n_keys, DIFF_QK_WIDTH), lambda b, i: (b, 0, 0)),
                _resident((1, DIFF_WIDTH, n_keys), lambda b, i: (b, 0, 0))]
    args = [qt, k, vt]
    if has_ctx:
        n_ctx = ctx_kv[0].shape[1]
        in_specs += [pl.BlockSpec((1, n_ctx, DIFF_QK_WIDTH), lambda b, i: (b, 0, 0)),
                     pl.BlockSpec((1, DIFF_WIDTH, n_ctx), lambda b, i: (b, 0, 0))]
        args += list(ctx_kv)
    in_specs += [pl.BlockSpec((None, 4, DIFF_QK_DIM), lambda b, i: (layer, 0, 0)),
                 pl.BlockSpec((None, DIFF_V_DIM, 1), lambda b, i: (layer, 0, 0))]
    args += [lam_vec, subln_g.reshape(DEPTH, DIFF_V_DIM, 1)]
    n_stream = 2 * DIFF_HEADS
    return pl.pallas_call(
        functools.partial(_diff_kernel, has_ctx=has_ctx, lam_init=lam_init, n_kv=n_kv, tk=tk),
        out_shape=jax.ShapeDtypeStruct((nb, n, DIFF_WIDTH), BF16),
        grid=(nb, n // tq),
        in_specs=in_specs,
        out_specs=pl.BlockSpec((1, tq, DIFF_WIDTH), lambda b, i: (b, i, 0)),
        scratch_shapes=[pltpu.VMEM((n_stream, 2 * DIFF_QK_DIM, tq), BF16),
                        pltpu.VMEM((n_stream, tq), F32),
                        pltpu.VMEM((n_stream, DIFF_ACC_ROWS, tq), F32),
                        pltpu.VMEM((2, tk, tq), F32),
                        pltpu.VMEM((2, tq), F32)],
        compiler_params=_params("parallel", "arbitrary"),
        name="diff_attn",
    )(*args)


def _ctx_attn_kernel(qt_ref, k_ref, vt_ref, o_ref):
    qt, k = qt_ref[0], k_ref[0]
    outs = []
    for hd in range(NA_HEADS):
        s = jnp.dot(_head_pair_keys(k, hd), _head_queries(qt, hd), preferred_element_type=F32)
        p = jnp.exp2(s - jnp.max(s, axis=0, keepdims=True)).astype(BF16)
        o = jnp.dot(_values_with_ones(vt_ref[0, hd * NA_HEAD_DIM:(hd + 1) * NA_HEAD_DIM, :]), p,
                    preferred_element_type=F32)
        outs.append(o[0:NA_HEAD_DIM] / o[NA_HEAD_DIM:NA_HEAD_DIM + 1])
    o_ref[0] = jnp.concatenate(outs, axis=0).T.astype(BF16)


def _ctx_attn(qt, k, vt):
    nb, n, w = k.shape
    tok = pl.BlockSpec((1, n, w), lambda b: (b, 0, 0))
    feat = pl.BlockSpec((1, w, n), lambda b: (b, 0, 0))
    return pl.pallas_call(
        _ctx_attn_kernel,
        out_shape=jax.ShapeDtypeStruct(k.shape, BF16),
        grid=(nb,),
        in_specs=[feat, tok, feat],
        out_specs=tok,
        compiler_params=_params("parallel"),
        name="ctx_dense_attn",
    )(qt, k, vt)


def _outproj_ffn_kernel(x_ref, mod_ref, g_ref, a_ref, pprev_ref, pcur_ref, pnext_ref, c_ref,
                        pw_ref, ps_ref, w_ref, fmod_ref, fg_ref, w1_ref, w2_ref, o_ref, act_ref,
                        *, tm, seq_len, res_w):
    i = pl.program_id(1)
    n = tm + 2 * POOL_HALO
    u = jnp.concatenate([pprev_ref[0], pcur_ref[0], pnext_ref[0]], axis=0)
    t_ext = i * tm - POOL_HALO + lax.broadcasted_iota(jnp.int32, (n, 1), 0)
    u = jnp.where((t_ext >= 0) & (t_ext < seq_len), u, 0.0)
    s2 = u + pltpu.roll(u, 1, 0)
    s4 = pltpu.roll(s2, 1, 0) + pltpu.roll(s2, n - 1, 0)
    s8 = pltpu.roll(s4, 2, 0) + pltpu.roll(s4, n - 2, 0)
    s16 = pltpu.roll(s8, 4, 0) + pltpu.roll(s8, n - 4, 0)
    cur = slice(POOL_HALO, POOL_HALO + tm)
    group = lax.broadcasted_iota(jnp.int32, (1, POOL_WIDTH), 1) // POOL_GROUP_DIM
    wsum = jnp.where(group == 0, s2[cur], jnp.where(group == 1, s4[cur], jnp.where(group == 2, s8[cur], s16[cur])))
    half = jnp.where(group == 0, POOL_WINDOWS[0] // 2,
                     jnp.where(group == 1, POOL_WINDOWS[1] // 2,
                               jnp.where(group == 2, POOL_WINDOWS[2] // 2, POOL_WINDOWS[3] // 2)))
    t = i * tm + lax.broadcasted_iota(jnp.int32, (tm, 1), 0)
    count = (jnp.clip(t + half, 0, seq_len) - jnp.clip(t - half, 0, seq_len)).astype(F32)
    pooled = (wsum / count - u[cur]).astype(BF16)
    b = jnp.dot(pooled, pw_ref[...], preferred_element_type=F32) * ps_ref[...]
    cat = jnp.concatenate([a_ref[0], b.astype(BF16), c_ref[0]], axis=1)
    y = jnp.dot(cat, w_ref[...], preferred_element_type=F32)
    x_mixed = x_ref[0] + mod_ref[0, 2:3, :] * _rms(y, g_ref[...])
    o_ref[0] = _swiglu_half_step(x_mixed, fmod_ref, fg_ref, w1_ref, w2_ref, act_ref, res_w)


def _outproj_ffn(x, mods, mod_row, norm_g, a, pin, c, pool_bd, pool_scale, w, w1, w2, layer, res_w):
    nb, n, _ = x.shape
    tm = min(TOKEN_TILE, n)
    halo_per_tile = tm // POOL_HALO
    n_halo = n // POOL_HALO
    tok = lambda width: pl.BlockSpec((1, tm, width), lambda b, i: (b, i, 0))
    return pl.pallas_call(
        functools.partial(_outproj_ffn_kernel, tm=tm, seq_len=n, res_w=res_w),
        out_shape=jax.ShapeDtypeStruct(x.shape, F32),
        grid=(nb, n // tm),
        in_specs=[tok(D_MODEL),
                  _mod_spec(layer, mod_row, 1),
                  _norm_spec(layer, 3, 1),
                  tok(NA_WIDTH),
                  pl.BlockSpec((1, POOL_HALO, POOL_WIDTH),
                               lambda b, i: (b, jnp.maximum(i * halo_per_tile - 1, 0), 0)),
                  tok(POOL_WIDTH),
                  pl.BlockSpec((1, POOL_HALO, POOL_WIDTH),
                               lambda b, i: (b, jnp.minimum((i + 1) * halo_per_tile, n_halo - 1), 0)),
                  tok(DIFF_WIDTH),
                  pl.BlockSpec((None, POOL_WIDTH, POOL_WIDTH), lambda b, i: (layer, 0, 0)),
                  pl.BlockSpec((None, 1, POOL_WIDTH), lambda b, i: (layer, 0, 0)),
                  _resident((None, MIX_WIDTH, D_MODEL), lambda b, i: (layer, 0, 0)),
                  _mod_spec(layer, mod_row, 2),
                  _norm_spec(layer, 4, 2),
                  _resident((None, None, D_MODEL, 2 * D_FF), lambda b, i: (layer, 1, 0, 0)),
                  _resident((None, None, D_FF, D_MODEL), lambda b, i: (layer, 1, 0, 0))],
        out_specs=tok(D_MODEL),
        scratch_shapes=[pltpu.VMEM((tm, D_FF), BF16)],
        compiler_params=_params("parallel", "parallel"),
        name="mixer_out_proj_ffn",
    )(x, mods, norm_g.reshape(DEPTH, 6, 1, D_MODEL), a, pin, pin, pin, c, pool_bd, pool_scale, w,
      mods, norm_g.reshape(DEPTH, 3, 2, D_MODEL), w1, w2)


def _rope_tables(n_tokens):
    t = np.arange(n_tokens)
    row = (t // GRID_W).astype(np.float32)
    col = (t % GRID_W).astype(np.float32)
    n_freq = DIFF_QK_DIM // 4
    inv_freq = np.power(np.float32(ROPE_THETA), -np.arange(n_freq, dtype=np.float32) / np.float32(n_freq))
    ang = np.concatenate([row[:, None] * inv_freq, col[:, None] * inv_freq], axis=-1).astype(np.float32)
    cos, sin = np.cos(ang), np.sin(ang)
    tables = (np.tile(cos, (1, 4)), np.concatenate([-sin, sin, -sin, sin], axis=-1), cos.T, sin.T)
    return tuple(jnp.asarray(np.ascontiguousarray(a), F32) for a in tables)


def _identity_rope_tables(n_tokens):
    half = DIFF_QK_DIM // 2
    return (jnp.ones((n_tokens, 2 * DIFF_QK_DIM), F32), jnp.zeros((n_tokens, 2 * DIFF_QK_DIM), F32),
            jnp.ones((half, n_tokens), F32), jnp.zeros((half, n_tokens), F32))


def _block_diag(w):
    n_l, n_g, cg, _ = w.shape
    eye = jnp.eye(n_g, dtype=w.dtype)
    return (eye[None, :, None, :, None] * w[:, :, :, None, :]).reshape(n_l, n_g * cg, n_g * cg)


def kernel(x, c, ctx, c_ctx, w_ada, b_ada, norm_g, ffn_w1, ffn_w2, w_in, w_out, na_rpb, pool_w, pool_scale,
           diff_lambda, diff_subln_g):
    nb, s, _ = x.shape
    n_ctx = ctx.shape[1]
    assert nb < MOD_ROWS and s % (GRID_W * NA_QROWS) == 0 and s % DIFF_TK == 0 and n_ctx % POOL_HALO == 0
    assert n_ctx <= DIFF_TK

    w1 = ffn_w1.astype(BF16)
    w2 = ffn_w2.astype(BF16)
    q_lo = 3 * NA_WIDTH + POOL_WIDTH
    k_lo = q_lo + DIFF_QK_WIDTH
    v_lo = k_lo + DIFF_QK_WIDTH
    w_in_b = w_in.astype(BF16)
    w_tok = w_in_b
    w_feat = jnp.swapaxes(jnp.concatenate([w_in_b[:, :, q_lo:k_lo], w_in_b[:, :, v_lo:], w_in_b[:, :, :NA_WIDTH],
                                           w_in_b[:, :, 2 * NA_WIDTH:3 * NA_WIDTH]], axis=2), 1, 2)
    w_out_b = w_out.astype(BF16)

    ctx_row = nb
    cvec = jnp.zeros((MOD_ROWS, D_MODEL), F32).at[:nb].set(c).at[ctx_row].set(c_ctx)
    mods = _ada(cvec, w_ada, b_ada).reshape(DEPTH, MOD_ROWS, N_MOD // 3, 3, D_MODEL)
    lat_row = lambda b: b
    ctx_mod_row = lambda b: ctx_row

    rope_lat = _rope_tables(s)
    rope_ctx = _identity_rope_tables(n_ctx)
    bias = _na_bias_tables(na_rpb, s // GRID_W)
    pool_bd = _block_diag(pool_w).astype(BF16)
    pool_sc = pool_scale.reshape(DEPTH, 1, POOL_WIDTH)

    x_lat, x_ctx = x, ctx
    for layer in range(DEPTH):
        last = layer == DEPTH - 1
        lam_init = 0.8 - 0.6 * math.exp(-0.3 * layer)

        x_lat = _ffn(x_lat, mods, lat_row, norm_g, w1, w2, layer, 0, 0.5)
        x_ctx = _ffn(x_ctx, mods, ctx_mod_row, norm_g, w1, w2, layer, 0, 0.5)

        aq, ak, av, pin, dq, dk, dv = _inproj(x_lat, mods, lat_row, norm_g, w_tok, w_feat, layer, rope_lat)
        aqc, akc, avc, pinc, dqc, dkc, dvc = _inproj(x_ctx, mods, ctx_mod_row, norm_g, w_tok, w_feat, layer, rope_ctx)
        a_lat = _na(aq, ak, av, akc, avc, bias, layer)
        c_lat = _diff(dq, dk, dv, (dkc, dvc), diff_lambda, diff_subln_g, layer, lam_init)
        x_lat = _outproj_ffn(x_lat, mods, lat_row, norm_g, a_lat, pin, c_lat, pool_bd, pool_sc, w_out_b, w1, w2,
                             layer, 0.5)

        if not last:
            a_ctx = _ctx_attn(aqc, akc, avc)
            c_ctx_out = _diff(dqc, dkc, dvc, None, diff_lambda, diff_subln_g, layer, lam_init)
            x_ctx = _outproj_ffn(x_ctx, mods, ctx_mod_row, norm_g, a_ctx, pinc, c_ctx_out, pool_bd, pool_sc, w_out_b,
                                 w1, w2, layer, 0.5)
    return x_lat
```
